```python
import math
import jax, jax.numpy as jnp
from jax import lax
import numpy as np

D_MODEL = 2048
BATCH = 4
SEQ = 8192
DEPTH = 1

N_META = 16
CONV_K = 4
GDN_HEADS = 8
GDN_DK = 128
GDN_DV = 128
GDN_CHUNK = 64
GLA_HEADS = 4
GLA_DK = 128
GLA_DV = 256
GLA_CHUNK = 16
GLA_GATE_RANK = 16
GLA_GATE_NORMALIZER = 16.0
GDN_QK = GDN_HEADS * GDN_DK
GDN_V = GDN_HEADS * GDN_DV
GLA_QK = GLA_HEADS * GLA_DK
GLA_V = GLA_HEADS * GLA_DV
MIX_WIDTH = GDN_V + GLA_V
D_FF = -(-8 * D_MODEL // (3 * 256)) * 256
IN_SPLITS = (2 * GDN_QK + GDN_V, GDN_V, GDN_HEADS, GDN_HEADS, GLA_QK, GLA_QK, GLA_V, GLA_V, GLA_GATE_RANK)
D_IN = sum(IN_SPLITS)
IN_OFFSETS = tuple(int(i) for i in np.cumsum(IN_SPLITS)[:-1])
NORM_EPS = 1e-6

kernel_name = "hybrid_gdn_gla_meta_block"


def rms_norm(x, w):
    xf = x.astype(jnp.float32)
    y = xf * lax.rsqrt(jnp.mean(xf * xf, axis=-1, keepdims=True) + NORM_EPS)
    return (y * w.astype(jnp.float32)).astype(x.dtype)


def l2_normalize(x):
    xf = x.astype(jnp.float32)
    return (xf * lax.rsqrt(jnp.sum(xf * xf, axis=-1, keepdims=True) + NORM_EPS)).astype(x.dtype)


def causal_short_conv(x, w):
    L = x.shape[1]
    xp = jnp.pad(x, ((0, 0), (CONV_K - 1, 0), (0, 0)))
    y = xp[:, 0:L] * w[0]
    for i in range(1, CONV_K):
        y = y + xp[:, i:i + L] * w[i]
    return jax.nn.silu(y)


def to_chunks(t, chunk, pad):
    t = jnp.pad(t, ((0, 0), (pad, 0), (0, 0), (0, 0)))
    b, lp, h, d = t.shape
    return t.reshape(b, lp // chunk, chunk, h, d).transpose(0, 3, 1, 2, 4)


def from_chunks(o, pad):
    b, h, n, c, d = o.shape
    return o.transpose(0, 2, 3, 1, 4).reshape(b, n * c, h, d)[:, pad:]


def gated_delta_rule(q, k, v, beta, g):
    out_dtype = v.dtype
    f32 = jnp.float32
    C = GDN_CHUNK
    pad = (-N_META) % C
    q, k, v = (to_chunks(t.astype(f32), C, pad) for t in (q, k, v))
    beta, g = (to_chunks(t.astype(f32)[..., None], C, pad)[..., 0] for t in (beta, g))
    gc = jnp.cumsum(g, axis=-1)
    causal = jnp.tril(jnp.ones((C, C), bool))
    strict = jnp.tril(jnp.ones((C, C), bool), -1)
    decay = jnp.exp(jnp.where(causal, gc[..., :, None] - gc[..., None, :], -jnp.inf))
    kb = k * beta[..., None]
    a_low = jnp.where(strict, jnp.einsum('bhncd,bhnsd->bhncs', kb, k) * decay, 0.0)
    t_mat = a_low + jnp.eye(C, dtype=f32)
    u = lax.linalg.triangular_solve(t_mat, v * beta[..., None], left_side=True, lower=True, unit_diagonal=True)
    w = lax.linalg.triangular_solve(t_mat, kb * jnp.exp(gc)[..., None], left_side=True, lower=True, unit_diagonal=True)
    qk = jnp.einsum('bhncd,bhnsd->bhncs', q, k) * decay
    q_dec = q * jnp.exp(gc)[..., None]
    k_dec = k * jnp.exp(gc[..., -1:] - gc)[..., None]
    g_last = jnp.exp(gc[..., -1])

    def step(S, inp):
        qd, kd, u_c, w_c, qk_c, gl = inp
        v_new = u_c - jnp.einsum('bhcd,bhde->bhce', w_c, S)
        o = jnp.einsum('bhcd,bhde->bhce', qd, S) + jnp.einsum('bhcs,bhse->bhce', qk_c, v_new)
        S = S * gl[..., None, None] + jnp.einsum('bhcd,bhce->bhde', kd, v_new)
        return S, o

    xs = tuple(jnp.moveaxis(t, 2, 0) for t in (q_dec, k_dec, u, w, qk, g_last))
    b, h = q.shape[0], q.shape[1]
    S0 = jnp.zeros((b, h, GDN_DK, GDN_DV), f32)
    _, o = lax.scan(step, S0, xs)
    return from_chunks(jnp.moveaxis(o, 0, 2), pad).astype(out_dtype)


def gla_chunked(q, k, v, log_a):
    out_dtype = v.dtype
    f32 = jnp.float32
    C = GLA_CHUNK
    pad = (-N_META) % C
    q, k, v, log_a = (to_chunks(t.astype(f32), C, pad) for t in (q, k, v, log_a))
    bcum = jnp.cumsum(log_a, axis=-2)
    causal = jnp.tril(jnp.ones((C, C), bool))

    def step(S, inp):
        q_c, k_c, v_c, b_c = inp
        diff = jnp.where(causal[..., None], b_c[..., :, None, :] - b_c[..., None, :, :], -jnp.inf)
        scores = jnp.einsum('bhid,bhjd,bhijd->bhij', q_c, k_c, jnp.exp(diff))
        o = jnp.einsum('bhid,bhde->bhie', q_c * jnp.exp(b_c), S) + jnp.einsum('bhij,bhje->bhie', scores, v_c)
        b_last = b_c[..., -1, :]
        S = S * jnp.exp(b_last)[..., None] + jnp.einsum(
            'bhjd,bhje->bhde', k_c * jnp.exp(b_last[..., None, :] - b_c), v_c)
        return S, o

    xs = tuple(jnp.moveaxis(t, 2, 0) for t in (q, k, v, bcum))
    b, h = q.shape[0], q.shape[1]
    S0 = jnp.zeros((b, h, GLA_DK, GLA_DV), f32)
    _, o = lax.scan(step, S0, xs)
    return from_chunks(jnp.moveaxis(o, 0, 2), pad).astype(out_dtype)


def setup_inputs(seed: int = 0) -> dict:
    key = jax.random.key(seed)
    ks = jax.random.split(key, 20)
    f32 = jnp.float32

    def nrm(k, shape, scale):
        return jax.random.normal(k, shape, f32) * scale

    def gain(k, shape):
        return 1.0 + 0.01 * jax.random.normal(k, shape, f32)

    dt = jnp.exp(jax.random.uniform(ks[7], (DEPTH, GDN_HEADS), f32, math.log(1e-3), math.log(1e-1)))
    return {
        "x": nrm(ks[0], (BATCH, SEQ, D_MODEL), 1.0),
        "meta_tokens": nrm(ks[1], (N_META, D_MODEL), 1.0),
        "attn_norm_w": gain(ks[2], (DEPTH, D_MODEL)),
        "w_in": nrm(ks[3], (DEPTH, D_MODEL, D_IN), D_MODEL ** -0.5),
        "gdn_conv_w": nrm(ks[4], (DEPTH, CONV_K, 2 * GDN_QK + GDN_V), CONV_K ** -0.5),
        "gdn_a_log": jnp.log(jax.random.uniform(ks[5], (DEPTH, GDN_HEADS), f32, 1.0, 16.0)),
        "gdn_dt_bias": dt + jnp.log(-jnp.expm1(-dt)),
        "gdn_norm_w": gain(ks[6], (DEPTH, GDN_DV)),
        "gla_gate_w2": nrm(ks[8], (DEPTH, GLA_GATE_RANK, GLA_QK), GLA_GATE_RANK ** -0.5),
        "gla_gate_b": nrm(ks[9], (DEPTH, GLA_QK), 0.01),
        "gla_norm_w": gain(ks[10], (DEPTH, GLA_DV)),
        "w_out": nrm(ks[11], (DEPTH, MIX_WIDTH, D_MODEL), MIX_WIDTH ** -0.5),
        "ffn_norm_w": gain(ks[12], (DEPTH, D_MODEL)),
        "w_gate": nrm(ks[13], (DEPTH, D_MODEL, D_FF), D_MODEL ** -0.5),
        "w_up": nrm(ks[14], (DEPTH, D_MODEL, D_FF), D_MODEL ** -0.5),
        "w_down": nrm(ks[15], (DEPTH, D_FF, D_MODEL), D_FF ** -0.5),
        "final_norm_w": gain(ks[16], (D_MODEL,)),
    }


def reference(x, meta_tokens, attn_norm_w, w_in, gdn_conv_w, gdn_a_log, gdn_dt_bias, gdn_norm_w,
              gla_gate_w2, gla_gate_b, gla_norm_w, w_out, ffn_norm_w, w_gate, w_up, w_down, final_norm_w):
    f32 = jnp.float32
    bsz = x.shape[0]
    meta = jnp.broadcast_to(meta_tokens.astype(x.dtype)[None], (bsz, N_META, D_MODEL))
    h = jnp.concatenate([meta, x], axis=1)
    L = h.shape[1]
    for layer in range(DEPTH):
        n = rms_norm(h, attn_norm_w[layer])
        proj = n @ w_in[layer]
        (gdn_qkv, gdn_z, gdn_a, gdn_b, gla_q, gla_k, gla_v, gla_r, gla_lr) = jnp.split(proj, IN_OFFSETS, axis=-1)

        qkv = causal_short_conv(gdn_qkv, gdn_conv_w[layer])
        q, k, v = jnp.split(qkv, (GDN_QK, 2 * GDN_QK), axis=-1)
        q = l2_normalize(q.reshape(bsz, L, GDN_HEADS, GDN_DK)) * (GDN_DK ** -0.5)
        k = l2_normalize(k.reshape(bsz, L, GDN_HEADS, GDN_DK))
        v = v.reshape(bsz, L, GDN_HEADS, GDN_DV)
        beta = jax.nn.sigmoid(gdn_b.astype(f32))
        g = -jnp.exp(gdn_a_log[layer].astype(f32)) * jax.nn.softplus(
            gdn_a.astype(f32) + gdn_dt_bias[layer].astype(f32))
        o_gdn = gated_delta_rule(q, k, v, beta, g)
        o_gdn = rms_norm(o_gdn, gdn_norm_w[layer]) * jax.nn.silu(gdn_z.reshape(bsz, L, GDN_HEADS, GDN_DV))

        gq = gla_q.reshape(bsz, L, GLA_HEADS, GLA_DK) * (GLA_DK ** -0.5)
        gk = gla_k.reshape(bsz, L, GLA_HEADS, GLA_DK)
        gv = gla_v.reshape(bsz, L, GLA_HEADS, GLA_DV)
        log_a = jax.nn.log_sigmoid((gla_lr @ gla_gate_w2[layer] + gla_gate_b[layer]).astype(f32)) / GLA_GATE_NORMALIZER
        o_gla = gla_chunked(gq, gk, gv, log_a.reshape(bsz, L, GLA_HEADS, GLA_DK))
        o_gla = rms_norm(o_gla, gla_norm_w[layer]) * jax.nn.silu(gla_r.reshape(bsz, L, GLA_HEADS, GLA_DV))

        mixed = jnp.concatenate([o_gdn.reshape(bsz, L, GDN_V), o_gla.reshape(bsz, L, GLA_V)], axis=-1)
        h = h + mixed @ w_out[layer]

        n = rms_norm(h, ffn_norm_w[layer])
        h = h + (jax.nn.silu(n @ w_gate[layer]) * (n @ w_up[layer])) @ w_down[layer]
    return rms_norm(h[:, N_META:], final_norm_w)
```

```python
import functools

import jax
import jax.numpy as jnp
from jax import lax
from jax.experimental import pallas as pl
from jax.experimental.pallas import tpu as pltpu

F32 = jnp.float32
BF16 = jnp.bfloat16

N_META = 16
CONV_K = 4
GDN_HEADS = 8
GDN_DK = 128
GDN_DV = 128
GLA_HEADS = 4
GLA_DK = 128
GLA_DV = 256
GLA_GATE_RANK = 16
GLA_GATE_NORMALIZER = 16.0
GDN_QK = GDN_HEADS * GDN_DK
GDN_V = GDN_HEADS * GDN_DV
GLA_QK = GLA_HEADS * GLA_DK
GLA_V = GLA_HEADS * GLA_DV
NORM_EPS = 1e-6

CHUNK = 64
SUB = 16
LANES = 128
MAIN_COLS = 2 * GDN_QK + 2 * GDN_V + 2 * GLA_QK + 2 * GLA_V
VMEM_LIMIT = 56 * 1024 * 1024


def _dot(a, b):
    return jnp.dot(a.astype(BF16), b.astype(BF16), preferred_element_type=F32)


def _dot_nt(a, b):
    return lax.dot_general(a.astype(BF16), b.astype(BF16), (((1,), (1,)), ((), ())),
                           preferred_element_type=F32)


def _dot_hilo(a_exact, b):
    bh = b.astype(BF16)
    bl = (b - bh.astype(F32)).astype(BF16)
    ab = a_exact.astype(BF16)
    return (jnp.dot(ab, bh, preferred_element_type=F32)
            + jnp.dot(ab, bl, preferred_element_type=F32))


def _sigmoid(x):
    return 1.0 / (1.0 + jnp.exp(-x))


def _silu(x):
    return x * _sigmoid(x)


def _softplus(x):
    return jnp.maximum(x, 0.0) + jnp.log(1.0 + jnp.exp(-jnp.abs(x)))


def _iota2(shape, dim):
    return lax.broadcasted_iota(jnp.int32, shape, dim)


def _inproj_kernel(x_ref, nw_ref, w_ref, ws_ref, o_ref, os_ref, n_scr):
    j = pl.program_id(1)

    @pl.when(j == 0)
    def _():
        x = x_ref[...]
        ms = jnp.mean(x * x, axis=-1, keepdims=True)
        n = (x * lax.rsqrt(ms + NORM_EPS)) * nw_ref[...]
        nb = n.astype(BF16)
        n_scr[...] = nb
        os_ref[...] = jnp.dot(nb, ws_ref[...], preferred_element_type=F32)

    o_ref[...] = jnp.dot(n_scr[...], w_ref[...], preferred_element_type=F32)


def _inproj(x2d, norm_w, w_main, w_small, tm, tn):
    rows, d = x2d.shape
    ncol = w_main.shape[1]
    return pl.pallas_call(
        _inproj_kernel,
        grid=(rows // tm, ncol // tn),
        in_specs=[
            pl.BlockSpec((tm, d), lambda i, j: (i, 0)),
            pl.BlockSpec((1, d), lambda i, j: (0, 0)),
            pl.BlockSpec((d, tn), lambda i, j: (0, j)),
            pl.BlockSpec((d, LANES), lambda i, j: (0, 0)),
        ],
        out_specs=[
            pl.BlockSpec((tm, tn), lambda i, j: (i, j)),
            pl.BlockSpec((tm, LANES), lambda i, j: (i, 0)),
        ],
        out_shape=[
            jax.ShapeDtypeStruct((rows, ncol), F32),
            jax.ShapeDtypeStruct((rows, LANES), F32),
        ],
        scratch_shapes=[pltpu.VMEM((tm, d), BF16)],
        compiler_params=pltpu.CompilerParams(
            dimension_semantics=("arbitrary", "arbitrary"), vmem_limit_bytes=VMEM_LIMIT),
        name="inproj",
    )(x2d, norm_w, w_main, w_small)


def _unit_lower_inverse(a, row, col):
    same16 = (row >> 4) == (col >> 4)
    same32 = (row >> 5) == (col >> 5)
    eye = jnp.where(row == col, 1.0, 0.0).astype(F32)
    a0 = jnp.where(same16, a, 0.0)
    x = eye - a0
    p = _dot(a0, a0)
    x = x + _dot(x, p)
    p = _dot(p, p)
    x = x + _dot(x, p)
    p = _dot(p, p)
    x = x + _dot(x, p)
    l1 = jnp.where(jnp.logical_and(same32, jnp.logical_not(same16)), a, 0.0)
    x = x - _dot(_dot(x, l1), x)
    l2 = jnp.where(same32, 0.0, a)
    x = x - _dot(_dot(x, l2), x)
    return x


def _gdn_kernel(pre_ref, main_ref, spre_ref, smain_ref, convw_ref, gp_ref, nw_ref,
                o_ref, s_scr, tail_scr, xbuf_scr):
    c = pl.program_id(1)
    is_pre = c == 0

    @pl.when(is_pre)
    def _():
        s_scr[...] = jnp.zeros_like(s_scr)
        tail_scr[...] = jnp.zeros_like(tail_scr)

    proj = jnp.where(is_pre, pre_ref[...], main_ref[...])
    small = jnp.where(is_pre, spre_ref[...], smain_ref[...])

    nqkv = 2 * GDN_QK + GDN_V
    xq = proj[:, :nqkv]
    xbuf_scr[0:8, :] = tail_scr[...]
    xbuf_scr[8:8 + CHUNK, :] = xq
    tail_scr[...] = xq[CHUNK - 8:, :]
    cw = convw_ref[...]
    y = xq * cw[CONV_K - 1:CONV_K, :]
    for i in range(1, CONV_K):
        y = y + xbuf_scr[8 - i:8 - i + CHUNK, :] * cw[CONV_K - 1 - i:CONV_K - i, :]
    qkv = _silu(y)

    lane = _iota2((CHUNK, LANES), 1)
    gp = gp_ref[...]
    g_all = -jnp.exp(gp[0:1, :]) * _softplus(small + gp[1:2, :])
    g_all = jnp.where(lane < GDN_HEADS, g_all, 0.0)
    row = _iota2((CHUNK, CHUNK), 0)
    col = _iota2((CHUNK, CHUNK), 1)
    causal = row >= col
    strict = row > col
    tril = jnp.where(causal, 1.0, 0.0).astype(F32)
    gc_all = _dot_hilo(tril, g_all)
    gc_t = gc_all.T
    beta_all = _sigmoid(small)

    nw = nw_ref[...]
    for h in range(GDN_HEADS):
        q = qkv[:, h * GDN_DK:(h + 1) * GDN_DK]
        k = qkv[:, GDN_QK + h * GDN_DK:GDN_QK + (h + 1) * GDN_DK]
        v = qkv[:, 2 * GDN_QK + h * GDN_DV:2 * GDN_QK + (h + 1) * GDN_DV]
        z = proj[:, nqkv + h * GDN_DV:nqkv + (h + 1) * GDN_DV]
        q = q * lax.rsqrt(jnp.sum(q * q, axis=-1, keepdims=True) + NORM_EPS) * (GDN_DK ** -0.5)
        k = k * lax.rsqrt(jnp.sum(k * k, axis=-1, keepdims=True) + NORM_EPS)
        beta = beta_all[:, GDN_HEADS + h:GDN_HEADS + h + 1]
        gcol = gc_all[:, h:h + 1]
        grow = gc_t[h:h + 1, :]
        glast = gc_all[CHUNK - 1:CHUNK, h:h + 1]
        decay = jnp.exp(jnp.where(causal, gcol - grow, -jnp.inf))
        kb = k * beta
        qk_kk = _dot_nt(jnp.concatenate([q, kb], axis=0), k)
        qk = qk_kk[:CHUNK] * decay
        a_low = jnp.where(strict, qk_kk[CHUNK:] * decay, 0.0)
        tinv = _unit_lower_inverse(a_low, row, col)
        egc = jnp.exp(gcol)
        uw = _dot(tinv, jnp.concatenate([v * beta, kb * egc], axis=1))
        u = uw[:, :GDN_DV]
        w = uw[:, GDN_DV:]
        qd = q * egc
        kd = k * jnp.exp(glast - gcol)
        s = s_scr[h]
        ws_qs = _dot(jnp.concatenate([w, qd], axis=0), s)
        v_new = u - ws_qs[:CHUNK]
        o = ws_qs[CHUNK:] + _dot(qk, v_new)
        s_scr[h] = s * jnp.exp(glast) + _dot(kd.T, v_new)
        on = o * lax.rsqrt(jnp.mean(o * o, axis=-1, keepdims=True) + NORM_EPS) * nw
        o_ref[:, h * GDN_DV:(h + 1) * GDN_DV] = (on * _silu(z)).astype(o_ref.dtype)


def _gdn(proj_pre, proj_main, small_pre, small_main, conv_w, gparams, norm_w, bsz, nchunk):
    rows = proj_main.shape[0]
    ncols = 2 * GDN_QK + 2 * GDN_V
    nqkv = 2 * GDN_QK + GDN_V

    def main_map(b, c):
        return (b * nchunk + jnp.maximum(c - 1, 0), 0)

    return pl.pallas_call(
        _gdn_kernel,
        grid=(bsz, nchunk + 1),
        in_specs=[
            pl.BlockSpec((CHUNK, ncols), lambda b, c: (0, 0)),
            pl.BlockSpec((CHUNK, ncols), main_map),
            pl.BlockSpec((CHUNK, LANES), lambda b, c: (0, 0)),
            pl.BlockSpec((CHUNK, LANES), main_map),
            pl.BlockSpec((CONV_K, nqkv), lambda b, c: (0, 0)),
            pl.BlockSpec((8, LANES), lambda b, c: (0, 0)),
            pl.BlockSpec((1, GDN_DV), lambda b, c: (0, 0)),
        ],
        out_specs=pl.BlockSpec((CHUNK, GDN_V), main_map),
        out_shape=jax.ShapeDtypeStruct((rows, GDN_V), BF16),
        scratch_shapes=[
            pltpu.VMEM((GDN_HEADS, GDN_DK, GDN_DV), F32),
            pltpu.VMEM((8, nqkv), F32),
            pltpu.VMEM((8 + CHUNK, nqkv), F32),
        ],
        compiler_params=pltpu.CompilerParams(
            dimension_semantics=("arbitrary", "arbitrary"), vmem_limit_bytes=VMEM_LIMIT),
        name="gdn",
    )(proj_pre, proj_main, small_pre, small_main, conv_w, gparams, norm_w)


def _gla_kernel(qp_ref, kp_ref, vp_ref, rp_ref, sp_ref,
                qm_ref, km_ref, vm_ref, rm_ref, sm_ref,
                w2_ref, gb_ref, nw_ref, o_ref, s_scr):
    c = pl.program_id(1)
    is_pre = c == 0

    @pl.when(is_pre)
    def _():
        s_scr[...] = jnp.zeros_like(s_scr)

    q_all = jnp.where(is_pre, qp_ref[...], qm_ref[...]) * (GLA_DK ** -0.5)
    k_all = jnp.where(is_pre, kp_ref[...], km_ref[...])
    v_all = jnp.where(is_pre, vp_ref[...], vm_ref[...])
    r_all = jnp.where(is_pre, rp_ref[...], rm_ref[...])
    small = jnp.where(is_pre, sp_ref[...], sm_ref[...])

    gate = _dot(small, w2_ref[...]) + gb_ref[...]
    log_a = -_softplus(-gate) * (1.0 / GLA_GATE_NORMALIZER)

    row = _iota2((CHUNK, CHUNK), 0)
    col = _iota2((CHUNK, CHUNK), 1)
    tril = jnp.where(row >= col, 1.0, 0.0).astype(F32)
    b_all = _dot_hilo(tril, log_a)

    half = CHUNK // 2
    m_lvl1 = jnp.logical_and(row >= half, col < half)
    m_lvl2 = jnp.logical_and((row >> 5) == (col >> 5),
                             jnp.logical_and((row & (half - 1)) >= SUB, (col & (half - 1)) < SUB))
    rsel = _iota2((CHUNK, GLA_DK), 0) < half
    lane_s = _iota2((SUB, LANES), 1)
    row_s = _iota2((SUB, LANES), 0)
    nw = nw_ref[...]

    for h in range(GLA_HEADS):
        q = q_all[:, h * GLA_DK:(h + 1) * GLA_DK]
        k = k_all[:, h * GLA_DK:(h + 1) * GLA_DK]
        b = b_all[:, h * GLA_DK:(h + 1) * GLA_DK]
        v = v_all[:, h * GLA_DV:(h + 1) * GLA_DV]
        r = r_all[:, h * GLA_DV:(h + 1) * GLA_DV]

        ref1 = b[half - 1:half, :]
        p1 = _dot_nt(q * jnp.exp(jnp.minimum(b - ref1, 0.0)),
                     k * jnp.exp(jnp.minimum(ref1 - b, 0.0)))
        ref2 = jnp.where(rsel, b[SUB - 1:SUB, :], b[half + SUB - 1:half + SUB, :])
        p2 = _dot_nt(q * jnp.exp(jnp.minimum(b - ref2, 0.0)),
                     k * jnp.exp(jnp.minimum(ref2 - b, 0.0)))
        scores = jnp.where(m_lvl1, p1, 0.0) + jnp.where(m_lvl2, p2, 0.0)

        diag_tiles = []
        for sb in range(CHUNK // SUB):
            qs = q[sb * SUB:(sb + 1) * SUB, :]
            ks = k[sb * SUB:(sb + 1) * SUB, :]
            bs = b[sb * SUB:(sb + 1) * SUB, :]
            acc = jnp.zeros((SUB, LANES), F32)
            for j in range(SUB):
                e = jnp.exp(jnp.minimum(bs - bs[j:j + 1, :], 0.0))
                colsum = jnp.sum(qs * (ks[j:j + 1, :] * e), axis=-1, keepdims=True)
                acc = acc + jnp.where(lane_s == sb * SUB + j, colsum, 0.0)
            acc = jnp.where(row_s + sb * SUB >= lane_s, acc, 0.0)
            diag_tiles.append(acc)
        scores = scores + jnp.concatenate(diag_tiles, axis=0)[:, :CHUNK]

        st = s_scr[h]
        blast = b[CHUNK - 1:CHUNK, :]
        o = _dot_nt(q * jnp.exp(b), st) + _dot(scores, v)
        kdec = k * jnp.exp(blast - b)
        s_scr[h] = st * jnp.exp(blast) + _dot(v.T, kdec)
        on = o * lax.rsqrt(jnp.mean(o * o, axis=-1, keepdims=True) + NORM_EPS) * nw
        o_ref[:, h * GLA_DV:(h + 1) * GLA_DV] = (on * _silu(r)).astype(o_ref.dtype)


def _gla(proj_pre, proj_main, small_pre, small_main, w2_pad, gate_b, norm_w, bsz, nchunk):
    rows = proj_main.shape[0]
    q_blk = (2 * GDN_QK + 2 * GDN_V) // GLA_QK
    v_blk = (2 * GDN_QK + 2 * GDN_V + 2 * GLA_QK) // GLA_V

    def rmap(blk):
        return lambda b, c: (b * nchunk + jnp.maximum(c - 1, 0), blk)

    def pmap(blk):
        return lambda b, c: (0, blk)

    return pl.pallas_call(
        _gla_kernel,
        grid=(bsz, nchunk + 1),
        in_specs=[
            pl.BlockSpec((CHUNK, GLA_QK), pmap(q_blk)),
            pl.BlockSpec((CHUNK, GLA_QK), pmap(q_blk + 1)),
            pl.BlockSpec((CHUNK, GLA_V), pmap(v_blk)),
            pl.BlockSpec((CHUNK, GLA_V), pmap(v_blk + 1)),
            pl.BlockSpec((CHUNK, LANES), pmap(0)),
            pl.BlockSpec((CHUNK, GLA_QK), rmap(q_blk)),
            pl.BlockSpec((CHUNK, GLA_QK), rmap(q_blk + 1)),
            pl.BlockSpec((CHUNK, GLA_V), rmap(v_blk)),
            pl.BlockSpec((CHUNK, GLA_V), rmap(v_blk + 1)),
            pl.BlockSpec((CHUNK, LANES), rmap(0)),
            pl.BlockSpec((LANES, GLA_QK), lambda b, c: (0, 0)),
            pl.BlockSpec((1, GLA_QK), lambda b, c: (0, 0)),
            pl.BlockSpec((1, GLA_DV), lambda b, c: (0, 0)),
        ],
        out_specs=pl.BlockSpec((CHUNK, GLA_V), rmap(0)),
        out_shape=jax.ShapeDtypeStruct((rows, GLA_V), BF16),
        scratch_shapes=[pltpu.VMEM((GLA_HEADS, GLA_DV, GLA_DK), F32)],
        compiler_params=pltpu.CompilerParams(
            dimension_semantics=("arbitrary", "arbitrary"), vmem_limit_bytes=VMEM_LIMIT),
        name="gla",
    )(proj_pre, proj_pre, proj_pre, proj_pre, small_pre,
      proj_main, proj_main, proj_main, proj_main, small_main,
      w2_pad, gate_b, norm_w)


def _outproj_kernel(x_ref, mg_ref, ml_ref, wg_ref, wl_ref, o_ref):
    o_ref[...] = (x_ref[...]
                  + jnp.dot(mg_ref[...], wg_ref[...], preferred_element_type=F32)
                  + jnp.dot(ml_ref[...], wl_ref[...], preferred_element_type=F32))


def _outproj(x2d, mg, ml, wo_g, wo_l, tm):
    rows, d = x2d.shape
    return pl.pallas_call(
        _outproj_kernel,
        grid=(rows // tm,),
        in_specs=[
            pl.BlockSpec((tm, d), lambda i: (i, 0)),
            pl.BlockSpec((tm, GDN_V), lambda i: (i, 0)),
            pl.BlockSpec((tm, GLA_V), lambda i: (i, 0)),
            pl.BlockSpec((GDN_V, d), lambda i: (0, 0)),
            pl.BlockSpec((GLA_V, d), lambda i: (0, 0)),
        ],
        out_specs=pl.BlockSpec((tm, d), lambda i: (i, 0)),
        out_shape=jax.ShapeDtypeStruct((rows, d), F32),
        compiler_params=pltpu.CompilerParams(
            dimension_semantics=("arbitrary",), vmem_limit_bytes=VMEM_LIMIT),
        name="outproj",
    )(x2d, mg, ml, wo_g, wo_l)


def _ffn_kernel(h_ref, nw_ref, wg_ref, wu_ref, wd_ref, fw_ref, o_ref, n_scr):
    f = pl.program_id(1)

    @pl.when(f == 0)
    def _():
        h = h_ref[...]
        ms = jnp.mean(h * h, axis=-1, keepdims=True)
        n_scr[...] = ((h * lax.rsqrt(ms + NORM_EPS)) * nw_ref[...]).astype(BF16)
        o_ref[...] = h

    n = n_scr[...]
    g = jnp.dot(n, wg_ref[...], preferred_element_type=F32)
    u = jnp.dot(n, wu_ref[...], preferred_element_type=F32)
    act = (_silu(g) * u).astype(BF16)
    o_ref[...] += jnp.dot(act, wd_ref[...], preferred_element_type=F32)

    @pl.when(f == pl.num_programs(1) - 1)
    def _():
        y = o_ref[...]
        ms = jnp.mean(y * y, axis=-1, keepdims=True)
        o_ref[...] = (y * lax.rsqrt(ms + NORM_EPS)) * fw_ref[...]


def _ffn(h2d, norm_w, w_gate, w_up, w_down, final_w, tm, tf):
    rows, d = h2d.shape
    dff = w_gate.shape[1]
    return pl.pallas_call(
        _ffn_kernel,
        grid=(rows // tm, dff // tf),
        in_specs=[
            pl.BlockSpec((tm, d), lambda i, f: (i, 0)),
            pl.BlockSpec((1, d), lambda i, f: (0, 0)),
            pl.BlockSpec((d, tf), lambda i, f: (0, f)),
            pl.BlockSpec((d, tf), lambda i, f: (0, f)),
            pl.BlockSpec((tf, d), lambda i, f: (f, 0)),
            pl.BlockSpec((1, d), lambda i, f: (0, 0)),
        ],
        out_specs=pl.BlockSpec((tm, d), lambda i, f: (i, 0)),
        out_shape=jax.ShapeDtypeStruct((rows, d), F32),
        scratch_shapes=[pltpu.VMEM((tm, d), BF16)],
        compiler_params=pltpu.CompilerParams(
            dimension_semantics=("arbitrary", "arbitrary"), vmem_limit_bytes=VMEM_LIMIT),
        name="ffn",
    )(h2d, norm_w, w_gate, w_up, w_down, final_w)


def _pick_tile(n, pref):
    t = min(n, pref)
    while n % t:
        t //= 2
    return t


def kernel(x, meta_tokens, attn_norm_w, w_in, gdn_conv_w, gdn_a_log, gdn_dt_bias, gdn_norm_w,
           gla_gate_w2, gla_gate_b, gla_norm_w, w_out, ffn_norm_w, w_gate, w_up, w_down, final_norm_w):
    bsz, seq, d = x.shape
    assert seq % CHUNK == 0 and attn_norm_w.shape[0] == 1
    nchunk = seq // CHUNK
    rows = bsz * seq
    x2d = x.reshape(rows, d)

    wi = w_in[0]
    off_small = 2 * GDN_QK + 2 * GDN_V
    off_gla = off_small + 2 * GDN_HEADS
    off_lr = off_gla + 2 * GLA_QK + 2 * GLA_V
    w_main = jnp.concatenate([wi[:, :off_small], wi[:, off_gla:off_lr]], axis=1).astype(BF16)
    w_small = jnp.concatenate(
        [wi[:, off_small:off_gla], wi[:, off_lr:],
         jnp.zeros((d, LANES - 2 * GDN_HEADS - GLA_GATE_RANK), F32)], axis=1).astype(BF16)
    assert w_main.shape[1] == MAIN_COLS
    nw_attn = attn_norm_w[0].reshape(1, d)
    gparams = jnp.zeros((8, LANES), F32)
    gparams = gparams.at[0, :GDN_HEADS].set(gdn_a_log[0]).at[1, :GDN_HEADS].set(gdn_dt_bias[0])
    w2_pad = jnp.zeros((LANES, GLA_QK), F32).at[2 * GDN_HEADS:2 * GDN_HEADS + GLA_GATE_RANK].set(gla_gate_w2[0])

    h_pre = jnp.concatenate([jnp.zeros((CHUNK - N_META, d), x.dtype), meta_tokens.astype(x.dtype)], axis=0)

    tm_in = _pick_tile(rows, 1024)
    proj_main, small_main = _inproj(x2d, nw_attn, w_main, w_small, tm_in, 512)
    proj_pre, small_pre = _inproj(h_pre, nw_attn, w_main, w_small, CHUNK, 512)

    mg = _gdn(proj_pre, proj_main, small_pre, small_main, gdn_conv_w[0], gparams,
              gdn_norm_w[0].reshape(1, GDN_DV), bsz, nchunk)
    ml = _gla(proj_pre, proj_main, small_pre, small_main, w2_pad.astype(BF16),
              gla_gate_b[0].reshape(1, GLA_QK), gla_norm_w[0].reshape(1, GLA_DV), bsz, nchunk)

    wo = w_out[0].astype(BF16)
    h2d = _outproj(x2d, mg, ml, wo[:GDN_V], wo[GDN_V:], _pick_tile(rows, 512))
    out = _ffn(h2d, ffn_norm_w[0].reshape(1, d), w_gate[0].astype(BF16), w_up[0].astype(BF16),
               w_down[0].astype(BF16), final_norm_w.reshape(1, d), _pick_tile(rows, 512), 512)
    return out.reshape(bsz, seq, d)
```

```python
import jax
import jax.numpy as jnp
from jax import lax
from jax.experimental import pallas as pl
from jax.experimental.pallas import tpu as pltpu

F32 = jnp.float32
BF16 = jnp.bfloat16

N_META = 16
CONV_K = 4
GDN_HEADS = 8
GDN_DK = 128
GDN_DV = 128
GLA_HEADS = 4
GLA_DK = 128
GLA_DV = 256
GLA_GATE_RANK = 16
GLA_GATE_NORMALIZER = 16.0
GDN_QK = GDN_HEADS * GDN_DK
GDN_V = GDN_HEADS * GDN_DV
GLA_QK = GLA_HEADS * GLA_DK
GLA_V = GLA_HEADS * GLA_DV
NORM_EPS = 1e-6

CHUNK = 64
SUB = 16
GDN_STEP_CHUNKS = 4
LANES = 128
MAIN_COLS = 2 * GDN_QK + 2 * GDN_V + 2 * GLA_QK + 2 * GLA_V
VMEM_LIMIT = 56 * 1024 * 1024


def _dot(a, b):
    return jnp.dot(a.astype(BF16), b.astype(BF16), preferred_element_type=F32)


def _dot_nt(a, b):
    return lax.dot_general(a.astype(BF16), b.astype(BF16), (((1,), (1,)), ((), ())),
                           preferred_element_type=F32)


def _dot_hilo(a_exact, b):
    bh = b.astype(BF16)
    bl = (b - bh.astype(F32)).astype(BF16)
    ab = a_exact.astype(BF16)
    return (jnp.dot(ab, bh, preferred_element_type=F32)
            + jnp.dot(ab, bl, preferred_element_type=F32))


def _bmm(a, b):
    return jnp.einsum("gij,gjk->gik", a.astype(BF16), b.astype(BF16), preferred_element_type=F32)


def _bmm_nt(a, b):
    return jnp.einsum("gik,gjk->gij", a.astype(BF16), b.astype(BF16), preferred_element_type=F32)


def _sigmoid(x):
    return 1.0 / (1.0 + jnp.exp(-x))


def _silu(x):
    return x * _sigmoid(x)


def _softplus(x):
    return jnp.maximum(x, 0.0) + jnp.log(1.0 + jnp.exp(-jnp.abs(x)))


def _iota2(shape, dim):
    return lax.broadcasted_iota(jnp.int32, shape, dim)


def _inproj_kernel(x_ref, nw_ref, w_ref, ws_ref, o_ref, os_ref, n_scr):
    j = pl.program_id(1)

    @pl.when(j == 0)
    def _():
        x = x_ref[...]
        ms = jnp.mean(x * x, axis=-1, keepdims=True)
        n = (x * lax.rsqrt(ms + NORM_EPS)) * nw_ref[...]
        nb = n.astype(BF16)
        n_scr[...] = nb
        os_ref[...] = jnp.dot(nb, ws_ref[...], preferred_element_type=F32)

    o_ref[...] = jnp.dot(n_scr[...], w_ref[...], preferred_element_type=F32)


def _inproj(x2d, norm_w, w_main, w_small, tm, tn):
    rows, d = x2d.shape
    ncol = w_main.shape[1]
    return pl.pallas_call(
        _inproj_kernel,
        grid=(rows // tm, ncol // tn),
        in_specs=[
            pl.BlockSpec((tm, d), lambda i, j: (i, 0)),
            pl.BlockSpec((1, d), lambda i, j: (0, 0)),
            pl.BlockSpec((d, tn), lambda i, j: (0, j)),
            pl.BlockSpec((d, LANES), lambda i, j: (0, 0)),
        ],
        out_specs=[
            pl.BlockSpec((tm, tn), lambda i, j: (i, j)),
            pl.BlockSpec((tm, LANES), lambda i, j: (i, 0)),
        ],
        out_shape=[
            jax.ShapeDtypeStruct((rows, ncol), F32),
            jax.ShapeDtypeStruct((rows, LANES), F32),
        ],
        scratch_shapes=[pltpu.VMEM((tm, d), BF16)],
        compiler_params=pltpu.CompilerParams(
            dimension_semantics=("arbitrary", "arbitrary"), vmem_limit_bytes=VMEM_LIMIT),
        name="inproj",
    )(x2d, norm_w, w_main, w_small)


def _unit_lower_inverse(a, row, col):
    same16 = (row >> 4) == (col >> 4)
    same32 = (row >> 5) == (col >> 5)
    eye = jnp.where(row == col, 1.0, 0.0).astype(F32)
    a0 = jnp.where(same16, a, 0.0)
    x = eye - a0
    p = _bmm(a0, a0)
    x = x + _bmm(x, p)
    p = _bmm(p, p)
    x = x + _bmm(x, p)
    p = _bmm(p, p)
    x = x + _bmm(x, p)
    l1 = jnp.where(jnp.logical_and(same32, jnp.logical_not(same16)), a, 0.0)
    x = x - _bmm(_bmm(x, l1), x)
    l2 = jnp.where(same32, 0.0, a)
    x = x - _bmm(_bmm(x, l2), x)
    return x


def _gdn_kernel(pre_ref, main_ref, spre_ref, smain_ref, convw_ref, gp_ref, nw_ref,
                o_ref, s_scr, tail_scr, xbuf_scr):
    c = pl.program_id(1)
    is_pre = c == 0
    rows = GDN_STEP_CHUNKS * CHUNK

    @pl.when(is_pre)
    def _():
        s_scr[...] = jnp.zeros_like(s_scr)
        tail_scr[...] = jnp.zeros_like(tail_scr)

    proj = jnp.where(is_pre, pre_ref[...], main_ref[...])
    small = jnp.where(is_pre, spre_ref[...], smain_ref[...])

    nqkv = 2 * GDN_QK + GDN_V
    xq = proj[:, :nqkv]
    xbuf_scr[0:8, :] = tail_scr[...]
    xbuf_scr[8:8 + rows, :] = xq
    tail_scr[...] = xq[rows - 8:, :]
    cw = convw_ref[...]
    y = xq * cw[CONV_K - 1:CONV_K, :]
    for i in range(1, CONV_K):
        y = y + xbuf_scr[8 - i:8 - i + rows, :] * cw[CONV_K - 1 - i:CONV_K - i, :]
    qkv = _silu(y)

    lane = _iota2((rows, LANES), 1)
    gp = gp_ref[...]
    g_all = -jnp.exp(gp[0:1, :]) * _softplus(small + gp[1:2, :])
    g_all = jnp.where(lane < GDN_HEADS, g_all, 0.0)
    rrow = _iota2((rows, rows), 0)
    rcol = _iota2((rows, rows), 1)
    tril_blk = jnp.where(jnp.logical_and(rrow >= rcol, (rrow >> 6) == (rcol >> 6)), 1.0, 0.0).astype(F32)
    gc_all = _dot_hilo(tril_blk, g_all)
    beta_all = _sigmoid(small)

    pairs = [(j, h) for j in range(GDN_STEP_CHUNKS) for h in range(GDN_HEADS)]

    def rs(j):
        return slice(j * CHUNK, (j + 1) * CHUNK)

    def gather(arr, col0, width):
        return jnp.stack([arr[rs(j), col0 + h * width:col0 + (h + 1) * width] for j, h in pairs])

    q = gather(qkv, 0, GDN_DK)
    k = gather(qkv, GDN_QK, GDN_DK)
    v = gather(qkv, 2 * GDN_QK, GDN_DV)
    q = q * lax.rsqrt(jnp.sum(q * q, axis=-1, keepdims=True) + NORM_EPS) * (GDN_DK ** -0.5)
    k = k * lax.rsqrt(jnp.sum(k * k, axis=-1, keepdims=True) + NORM_EPS)
    beta = gather(beta_all, GDN_HEADS, 1)
    gcol = gather(gc_all, 0, 1)
    gc_t = [gc_all[rs(j), :].T for j in range(GDN_STEP_CHUNKS)]
    grow = jnp.stack([gc_t[j][h:h + 1, :] for j, h in pairs])
    glast = jnp.stack([gc_all[(j + 1) * CHUNK - 1:(j + 1) * CHUNK, h:h + 1] for j, h in pairs])

    row = _iota2((CHUNK, CHUNK), 0)
    col = _iota2((CHUNK, CHUNK), 1)
    decay = jnp.exp(jnp.where(row >= col, gcol - grow, -jnp.inf))
    kb = k * beta
    qk_kk = _bmm_nt(jnp.concatenate([q, kb], axis=1), k)
    qk = qk_kk[:, :CHUNK] * decay
    a_low = jnp.where(row > col, qk_kk[:, CHUNK:] * decay, 0.0)
    tinv = _unit_lower_inverse(a_low, row, col)
    egc = jnp.exp(gcol)
    uw = _bmm(tinv, jnp.concatenate([v * beta, kb * egc], axis=2))
    kd = k * jnp.exp(glast - gcol)
    kd_t = jnp.stack([kd[g].T for g in range(len(pairs))])
    kd_uw = _bmm(kd_t, uw)
    qk_uw = _bmm(qk, uw)
    q_eff = q * egc - qk_uw[:, :, GDN_DV:]
    eg_last = jnp.exp(glast)

    nw = nw_ref[...]
    s = s_scr[...]
    for j in range(GDN_STEP_CHUNKS):
        gs = slice(j * GDN_HEADS, (j + 1) * GDN_HEADS)
        sb = s.astype(BF16)
        o = _bmm(q_eff[gs], sb) + qk_uw[gs, :, :GDN_DV]
        s = s * eg_last[gs] - _bmm(kd_uw[gs, :, GDN_DV:], sb) + kd_uw[gs, :, :GDN_DV]
        on = o * lax.rsqrt(jnp.mean(o * o, axis=-1, keepdims=True) + NORM_EPS) * nw
        for h in range(GDN_HEADS):
            z = proj[rs(j), nqkv + h * GDN_DV:nqkv + (h + 1) * GDN_DV]
            o_ref[rs(j), h * GDN_DV:(h + 1) * GDN_DV] = (on[h] * _silu(z)).astype(o_ref.dtype)
    s_scr[...] = s


def _gdn(proj_pre, proj_main, small_pre, small_main, conv_w, gparams, norm_w, bsz, nstep):
    rows = proj_main.shape[0]
    ncols = 2 * GDN_QK + 2 * GDN_V
    nqkv = 2 * GDN_QK + GDN_V
    step_rows = GDN_STEP_CHUNKS * CHUNK

    def main_map(b, c):
        return (b * nstep + jnp.maximum(c - 1, 0), 0)

    return pl.pallas_call(
        _gdn_kernel,
        grid=(bsz, nstep + 1),
        in_specs=[
            pl.BlockSpec((step_rows, ncols), lambda b, c: (0, 0)),
            pl.BlockSpec((step_rows, ncols), main_map),
            pl.BlockSpec((step_rows, LANES), lambda b, c: (0, 0)),
            pl.BlockSpec((step_rows, LANES), main_map),
            pl.BlockSpec((CONV_K, nqkv), lambda b, c: (0, 0)),
            pl.BlockSpec((8, LANES), lambda b, c: (0, 0)),
            pl.BlockSpec((1, GDN_DV), lambda b, c: (0, 0)),
        ],
        out_specs=pl.BlockSpec((step_rows, GDN_V), main_map),
        out_shape=jax.ShapeDtypeStruct((rows, GDN_V), BF16),
        scratch_shapes=[
            pltpu.VMEM((GDN_HEADS, GDN_DK, GDN_DV), F32),
            pltpu.VMEM((8, nqkv), F32),
            pltpu.VMEM((8 + step_rows, nqkv), F32),
        ],
        compiler_params=pltpu.CompilerParams(
            dimension_semantics=("arbitrary", "arbitrary"), vmem_limit_bytes=VMEM_LIMIT),
        name="gdn",
    )(proj_pre, proj_main, small_pre, small_main, conv_w, gparams, norm_w)


def _gla_kernel(qp_ref, kp_ref, vp_ref, rp_ref, sp_ref,
                qm_ref, km_ref, vm_ref, rm_ref, sm_ref,
                w2_ref, gb_ref, nw_ref, o_ref, s_scr):
    c = pl.program_id(1)
    is_pre = c == 0

    @pl.when(is_pre)
    def _():
        s_scr[...] = jnp.zeros_like(s_scr)

    q_all = jnp.where(is_pre, qp_ref[...], qm_ref[...]) * (GLA_DK ** -0.5)
    k_all = jnp.where(is_pre, kp_ref[...], km_ref[...])
    v_all = jnp.where(is_pre, vp_ref[...], vm_ref[...])
    r_all = jnp.where(is_pre, rp_ref[...], rm_ref[...])
    small = jnp.where(is_pre, sp_ref[...], sm_ref[...])

    gate = _dot(small, w2_ref[...]) + gb_ref[...]
    log_a = -_softplus(-gate) * (1.0 / GLA_GATE_NORMALIZER)

    row = _iota2((CHUNK, CHUNK), 0)
    col = _iota2((CHUNK, CHUNK), 1)
    tril = jnp.where(row >= col, 1.0, 0.0).astype(F32)
    b_all = _dot_hilo(tril, log_a)

    half = CHUNK // 2
    m_lvl1 = jnp.logical_and(row >= half, col < half)
    m_lvl2 = jnp.logical_and((row >> 5) == (col >> 5),
                             jnp.logical_and((row & (half - 1)) >= SUB, (col & (half - 1)) < SUB))
    rsel = _iota2((CHUNK, GLA_DK), 0) < half
    lane_s = _iota2((SUB, LANES), 1)
    row_s = _iota2((SUB, LANES), 0)
    nw = nw_ref[...]

    for h in range(GLA_HEADS):
        q = q_all[:, h * GLA_DK:(h + 1) * GLA_DK]
        k = k_all[:, h * GLA_DK:(h + 1) * GLA_DK]
        b = b_all[:, h * GLA_DK:(h + 1) * GLA_DK]
        v = v_all[:, h * GLA_DV:(h + 1) * GLA_DV]
        r = r_all[:, h * GLA_DV:(h + 1) * GLA_DV]

        ref1 = b[half - 1:half, :]
        p1 = _dot_nt(q * jnp.exp(jnp.minimum(b - ref1, 0.0)),
                     k * jnp.exp(jnp.minimum(ref1 - b, 0.0)))
        ref2 = jnp.where(rsel, b[SUB - 1:SUB, :], b[half + SUB - 1:half + SUB, :])
        p2 = _dot_nt(q * jnp.exp(jnp.minimum(b - ref2, 0.0)),
                     k * jnp.exp(jnp.minimum(ref2 - b, 0.0)))
        scores = jnp.where(m_lvl1, p1, 0.0) + jnp.where(m_lvl2, p2, 0.0)

        diag_tiles = []
        for sb in range(CHUNK // SUB):
            qs = q[sb * SUB:(sb + 1) * SUB, :]
            ks = k[sb * SUB:(sb + 1) * SUB, :]
            bs = b[sb * SUB:(sb + 1) * SUB, :]
            acc = jnp.zeros((SUB, LANES), F32)
            for j in range(SUB):
                e = jnp.exp(jnp.minimum(bs - bs[j:j + 1, :], 0.0))
                colsum = jnp.sum(qs * (ks[j:j + 1, :] * e), axis=-1, keepdims=True)
                acc = acc + jnp.where(lane_s == sb * SUB + j, colsum, 0.0)
            acc = jnp.where(row_s + sb * SUB >= lane_s, acc, 0.0)
            diag_tiles.append(acc)
        scores = scores + jnp.concatenate(diag_tiles, axis=0)[:, :CHUNK]

        st = s_scr[h]
        blast = b[CHUNK - 1:CHUNK, :]
        o = _dot_nt(q * jnp.exp(b), st) + _dot(scores, v)
        kdec = k * jnp.exp(blast - b)
        s_scr[h] = st * jnp.exp(blast) + _dot(v.T, kdec)
        on = o * lax.rsqrt(jnp.mean(o * o, axis=-1, keepdims=True) + NORM_EPS) * nw
        o_ref[:, h * GLA_DV:(h + 1) * GLA_DV] = (on * _silu(r)).astype(o_ref.dtype)


def _gla(proj_pre, proj_main, small_pre, small_main, w2_pad, gate_b, norm_w, bsz, nchunk):
    rows = proj_main.shape[0]
    q_blk = (2 * GDN_QK + 2 * GDN_V) // GLA_QK
    v_blk = (2 * GDN_QK + 2 * GDN_V + 2 * GLA_QK) // GLA_V
    pre_blk = proj_pre.shape[0] // CHUNK - 1

    def rmap(blk):
        return lambda b, c: (b * nchunk + jnp.maximum(c - 1, 0), blk)

    def pmap(blk):
        return lambda b, c: (pre_blk, blk)

    return pl.pallas_call(
        _gla_kernel,
        grid=(bsz, nchunk + 1),
        in_specs=[
            pl.BlockSpec((CHUNK, GLA_QK), pmap(q_blk)),
            pl.BlockSpec((CHUNK, GLA_QK), pmap(q_blk + 1)),
            pl.BlockSpec((CHUNK, GLA_V), pmap(v_blk)),
            pl.BlockSpec((CHUNK, GLA_V), pmap(v_blk + 1)),
            pl.BlockSpec((CHUNK, LANES), pmap(0)),
            pl.BlockSpec((CHUNK, GLA_QK), rmap(q_blk)),
            pl.BlockSpec((CHUNK, GLA_QK), rmap(q_blk + 1)),
            pl.BlockSpec((CHUNK, GLA_V), rmap(v_blk)),
            pl.BlockSpec((CHUNK, GLA_V), rmap(v_blk + 1)),
            pl.BlockSpec((CHUNK, LANES), rmap(0)),
            pl.BlockSpec((LANES, GLA_QK), lambda b, c: (0, 0)),
            pl.BlockSpec((1, GLA_QK), lambda b, c: (0, 0)),
            pl.BlockSpec((1, GLA_DV), lambda b, c: (0, 0)),
        ],
        out_specs=pl.BlockSpec((CHUNK, GLA_V), rmap(0)),
        out_shape=jax.ShapeDtypeStruct((rows, GLA_V), BF16),
        scratch_shapes=[pltpu.VMEM((GLA_HEADS, GLA_DV, GLA_DK), F32)],
        compiler_params=pltpu.CompilerParams(
            dimension_semantics=("arbitrary", "arbitrary"), vmem_limit_bytes=VMEM_LIMIT),
        name="gla",
    )(proj_pre, proj_pre, proj_pre, proj_pre, small_pre,
      proj_main, proj_main, proj_main, proj_main, small_main,
      w2_pad, gate_b, norm_w)


def _outproj_kernel(x_ref, mg_ref, ml_ref, wg_ref, wl_ref, o_ref):
    o_ref[...] = (x_ref[...]
                  + jnp.dot(mg_ref[...], wg_ref[...], preferred_element_type=F32)
                  + jnp.dot(ml_ref[...], wl_ref[...], preferred_element_type=F32))


def _outproj(x2d, mg, ml, wo_g, wo_l, tm):
    rows, d = x2d.shape
    return pl.pallas_call(
        _outproj_kernel,
        grid=(rows // tm,),
        in_specs=[
            pl.BlockSpec((tm, d), lambda i: (i, 0)),
            pl.BlockSpec((tm, GDN_V), lambda i: (i, 0)),
            pl.BlockSpec((tm, GLA_V), lambda i: (i, 0)),
            pl.BlockSpec((GDN_V, d), lambda i: (0, 0)),
            pl.BlockSpec((GLA_V, d), lambda i: (0, 0)),
        ],
        out_specs=pl.BlockSpec((tm, d), lambda i: (i, 0)),
        out_shape=jax.ShapeDtypeStruct((rows, d), F32),
        compiler_params=pltpu.CompilerParams(
            dimension_semantics=("arbitrary",), vmem_limit_bytes=VMEM_LIMIT),
        name="outproj",
    )(x2d, mg, ml, wo_g, wo_l)


def _ffn_kernel(h_ref, nw_ref, wg_ref, wu_ref, wd_ref, fw_ref, o_ref, n_scr):
    f = pl.program_id(1)

    @pl.when(f == 0)
    def _():
        h = h_ref[...]
        ms = jnp.mean(h * h, axis=-1, keepdims=True)
        n_scr[...] = ((h * lax.rsqrt(ms + NORM_EPS)) * nw_ref[...]).astype(BF16)
        o_ref[...] = h

    n = n_scr[...]
    g = jnp.dot(n, wg_ref[...], preferred_element_type=F32)
    u = jnp.dot(n, wu_ref[...], preferred_element_type=F32)
    act = (_silu(g) * u).astype(BF16)
    o_ref[...] += jnp.dot(act, wd_ref[...], preferred_element_type=F32)

    @pl.when(f == pl.num_programs(1) - 1)
    def _():
        y = o_ref[...]
        ms = jnp.mean(y * y, axis=-1, keepdims=True)
        o_ref[...] = (y * lax.rsqrt(ms + NORM_EPS)) * fw_ref[...]


def _ffn(h2d, norm_w, w_gate, w_up, w_down, final_w, tm, tf):
    rows, d = h2d.shape
    dff = w_gate.shape[1]
    return pl.pallas_call(
        _ffn_kernel,
        grid=(rows // tm, dff // tf),
        in_specs=[
            pl.BlockSpec((tm, d), lambda i, f: (i, 0)),
            pl.BlockSpec((1, d), lambda i, f: (0, 0)),
            pl.BlockSpec((d, tf), lambda i, f: (0, f)),
            pl.BlockSpec((d, tf), lambda i, f: (0, f)),
            pl.BlockSpec((tf, d), lambda i, f: (f, 0)),
            pl.BlockSpec((1, d), lambda i, f: (0, 0)),
        ],
        out_specs=pl.BlockSpec((tm, d), lambda i, f: (i, 0)),
        out_shape=jax.ShapeDtypeStruct((rows, d), F32),
        scratch_shapes=[pltpu.VMEM((tm, d), BF16)],
        compiler_params=pltpu.CompilerParams(
            dimension_semantics=("arbitrary", "arbitrary"), vmem_limit_bytes=VMEM_LIMIT),
        name="ffn",
    )(h2d, norm_w, w_gate, w_up, w_down, final_w)


def _pick_tile(n, pref):
    t = min(n, pref)
    while n % t:
        t //= 2
    return t


def kernel(x, meta_tokens, attn_norm_w, w_in, gdn_conv_w, gdn_a_log, gdn_dt_bias, gdn_norm_w,
           gla_gate_w2, gla_gate_b, gla_norm_w, w_out, ffn_norm_w, w_gate, w_up, w_down, final_norm_w):
    bsz, seq, d = x.shape
    step_rows = GDN_STEP_CHUNKS * CHUNK
    assert seq % step_rows == 0 and attn_norm_w.shape[0] == 1
    nchunk = seq // CHUNK
    rows = bsz * seq
    x2d = x.reshape(rows, d)

    wi = w_in[0]
    off_small = 2 * GDN_QK + 2 * GDN_V
    off_gla = off_small + 2 * GDN_HEADS
    off_lr = off_gla + 2 * GLA_QK + 2 * GLA_V
    w_main = jnp.concatenate([wi[:, :off_small], wi[:, off_gla:off_lr]], axis=1).astype(BF16)
    w_small = jnp.concatenate(
        [wi[:, off_small:off_gla], wi[:, off_lr:],
         jnp.zeros((d, LANES - 2 * GDN_HEADS - GLA_GATE_RANK), F32)], axis=1).astype(BF16)
    assert w_main.shape[1] == MAIN_COLS
    nw_attn = attn_norm_w[0].reshape(1, d)
    gparams = jnp.zeros((8, LANES), F32)
    gparams = gparams.at[0, :GDN_HEADS].set(gdn_a_log[0]).at[1, :GDN_HEADS].set(gdn_dt_bias[0])
    w2_pad = jnp.zeros((LANES, GLA_QK), F32).at[2 * GDN_HEADS:2 * GDN_HEADS + GLA_GATE_RANK].set(gla_gate_w2[0])

    h_pre = jnp.concatenate([jnp.zeros((step_rows - N_META, d), x.dtype), meta_tokens.astype(x.dtype)], axis=0)

    tm_in = _pick_tile(rows, 1024)
    proj_main, small_main = _inproj(x2d, nw_attn, w_main, w_small, tm_in, 512)
    proj_pre, small_pre = _inproj(h_pre, nw_attn, w_main, w_small, step_rows, 512)

    mg = _gdn(proj_pre, proj_main, small_pre, small_main, gdn_conv_w[0], gparams,
              gdn_norm_w[0].reshape(1, GDN_DV), bsz, seq // step_rows)
    ml = _gla(proj_pre, proj_main, small_pre, small_main, w2_pad.astype(BF16),
              gla_gate_b[0].reshape(1, GLA_QK), gla_norm_w[0].reshape(1, GLA_DV), bsz, nchunk)

    wo = w_out[0].astype(BF16)
    h2d = _outproj(x2d, mg, ml, wo[:GDN_V], wo[GDN_V:], _pick_tile(rows, 512))
    out = _ffn(h2d, ffn_norm_w[0].reshape(1, d), w_gate[0].astype(BF16), w_up[0].astype(BF16),
               w_down[0].astype(BF16), final_norm_w.reshape(1, d), _pick_tile(rows, 512), 512)
    return out.reshape(bsz, seq, d)
```

```python
import functools

import jax
import jax.numpy as jnp
from jax import lax
from jax.experimental import pallas as pl
from jax.experimental.pallas import tpu as pltpu

F32 = jnp.float32
BF16 = jnp.bfloat16

N_META = 16
CONV_K = 4
GDN_HEADS = 8
GDN_DK = 128
GDN_DV = 128
GLA_HEADS = 4
GLA_DK = 128
GLA_DV = 256
GLA_GATE_RANK = 16
GLA_GATE_NORMALIZER = 16.0
GDN_QK = GDN_HEADS * GDN_DK
GDN_V = GDN_HEADS * GDN_DV
GLA_QK = GLA_HEADS * GLA_DK
GLA_V = GLA_HEADS * GLA_DV
NORM_EPS = 1e-6

CHUNK = 64
SUB = 16
MIX_STEP_CHUNKS = 4
LANES = 128
MAIN_COLS = 2 * GDN_QK + 2 * GDN_V + 2 * GLA_QK + 2 * GLA_V
VMEM_LIMIT = 56 * 1024 * 1024


def _dot(a, b):
    return jnp.dot(a.astype(BF16), b.astype(BF16), preferred_element_type=F32)


def _dot_nt(a, b):
    return lax.dot_general(a.astype(BF16), b.astype(BF16), (((1,), (1,)), ((), ())),
                           preferred_element_type=F32)


def _dot_hilo(a_exact, b):
    bh = b.astype(BF16)
    bl = (b - bh.astype(F32)).astype(BF16)
    ab = a_exact.astype(BF16)
    return (jnp.dot(ab, bh, preferred_element_type=F32)
            + jnp.dot(ab, bl, preferred_element_type=F32))


def _bmm(a, b):
    return jnp.einsum("gij,gjk->gik", a.astype(BF16), b.astype(BF16), preferred_element_type=F32)


def _bmm_nt(a, b):
    return jnp.einsum("gik,gjk->gij", a.astype(BF16), b.astype(BF16), preferred_element_type=F32)


def _sigmoid(x):
    return 1.0 / (1.0 + jnp.exp(-x))


def _silu(x):
    return x * _sigmoid(x)


def _softplus(x):
    return jnp.maximum(x, 0.0) + jnp.log(1.0 + jnp.exp(-jnp.abs(x)))


def _iota2(shape, dim):
    return lax.broadcasted_iota(jnp.int32, shape, dim)


def _inproj_kernel(x_ref, nw_ref, w_ref, ws_ref, o_ref, os_ref, *, tn):
    x = x_ref[...]
    ms = jnp.mean(x * x, axis=-1, keepdims=True)
    nb = ((x * lax.rsqrt(ms + NORM_EPS)) * nw_ref[...]).astype(BF16)
    os_ref[...] = jnp.dot(nb, ws_ref[...], preferred_element_type=F32)
    for j in range(w_ref.shape[1] // tn):
        cs = slice(j * tn, (j + 1) * tn)
        o_ref[:, cs] = jnp.dot(nb, w_ref[:, cs], preferred_element_type=F32).astype(o_ref.dtype)


def _inproj(x2d, norm_w, w_main, w_small, tm, tn):
    rows, d = x2d.shape
    ncol = w_main.shape[1]
    resident = pl.Buffered(1)
    return pl.pallas_call(
        functools.partial(_inproj_kernel, tn=tn),
        grid=(rows // tm,),
        in_specs=[
            pl.BlockSpec((tm, d), lambda i: (i, 0)),
            pl.BlockSpec((1, d), lambda i: (0, 0), pipeline_mode=resident),
            pl.BlockSpec((d, ncol), lambda i: (0, 0), pipeline_mode=resident),
            pl.BlockSpec((d, LANES), lambda i: (0, 0), pipeline_mode=resident),
        ],
        out_specs=[
            pl.BlockSpec((tm, ncol), lambda i: (i, 0)),
            pl.BlockSpec((tm, LANES), lambda i: (i, 0)),
        ],
        out_shape=[
            jax.ShapeDtypeStruct((rows, ncol), BF16),
            jax.ShapeDtypeStruct((rows, LANES), F32),
        ],
        compiler_params=pltpu.CompilerParams(
            dimension_semantics=("arbitrary",), vmem_limit_bytes=VMEM_LIMIT),
        name="inproj",
    )(x2d, norm_w, w_main, w_small)


def _unit_lower_inverse(a, row, col):
    same16 = (row >> 4) == (col >> 4)
    same32 = (row >> 5) == (col >> 5)
    eye = jnp.where(row == col, 1.0, 0.0).astype(F32)
    a0 = jnp.where(same16, a, 0.0)
    x = eye - a0
    p = _bmm(a0, a0)
    x = x + _bmm(x, p)
    p = _bmm(p, p)
    x = x + _bmm(x, p)
    p = _bmm(p, p)
    x = x + _bmm(x, p)
    l1 = jnp.where(jnp.logical_and(same32, jnp.logical_not(same16)), a, 0.0)
    x = x - _bmm(_bmm(x, l1), x)
    l2 = jnp.where(same32, 0.0, a)
    x = x - _bmm(_bmm(x, l2), x)
    return x


def _gdn_kernel(pre_ref, main_ref, spre_ref, smain_ref, convw_ref, gp_ref, nw_ref,
                o_ref, s_scr, tail_scr, xbuf_scr):
    c = pl.program_id(1)
    is_pre = c == 0
    rows = MIX_STEP_CHUNKS * CHUNK

    @pl.when(is_pre)
    def _():
        s_scr[...] = jnp.zeros_like(s_scr)
        tail_scr[...] = jnp.zeros_like(tail_scr)

    proj = jnp.where(is_pre, pre_ref[...], main_ref[...])
    small = jnp.where(is_pre, spre_ref[...], smain_ref[...])

    nqkv = 2 * GDN_QK + GDN_V
    xq = proj[:, :nqkv].astype(F32)
    xbuf_scr[0:8, :] = tail_scr[...]
    xbuf_scr[8:8 + rows, :] = xq
    tail_scr[...] = xq[rows - 8:, :]
    cw = convw_ref[...]
    y = xq * cw[CONV_K - 1:CONV_K, :]
    for i in range(1, CONV_K):
        y = y + xbuf_scr[8 - i:8 - i + rows, :] * cw[CONV_K - 1 - i:CONV_K - i, :]
    qkv = _silu(y)

    lane = _iota2((rows, LANES), 1)
    gp = gp_ref[...]
    g_all = -jnp.exp(gp[0:1, :]) * _softplus(small + gp[1:2, :])
    g_all = jnp.where(lane < GDN_HEADS, g_all, 0.0)
    rrow = _iota2((rows, rows), 0)
    rcol = _iota2((rows, rows), 1)
    tril_blk = jnp.where(jnp.logical_and(rrow >= rcol, (rrow >> 6) == (rcol >> 6)), 1.0, 0.0).astype(F32)
    gc_all = _dot_hilo(tril_blk, g_all)
    beta_all = _sigmoid(small)

    pairs = [(j, h) for j in range(MIX_STEP_CHUNKS) for h in range(GDN_HEADS)]

    def rs(j):
        return slice(j * CHUNK, (j + 1) * CHUNK)

    def gather(arr, col0, width):
        return jnp.stack([arr[rs(j), col0 + h * width:col0 + (h + 1) * width] for j, h in pairs])

    q = gather(qkv, 0, GDN_DK)
    k = gather(qkv, GDN_QK, GDN_DK)
    v = gather(qkv, 2 * GDN_QK, GDN_DV)
    q = q * lax.rsqrt(jnp.sum(q * q, axis=-1, keepdims=True) + NORM_EPS) * (GDN_DK ** -0.5)
    k = k * lax.rsqrt(jnp.sum(k * k, axis=-1, keepdims=True) + NORM_EPS)
    beta = gather(beta_all, GDN_HEADS, 1)
    gcol = gather(gc_all, 0, 1)
    gc_t = [gc_all[rs(j), :].T for j in range(MIX_STEP_CHUNKS)]
    grow = jnp.stack([gc_t[j][h:h + 1, :] for j, h in pairs])
    glast = jnp.stack([gc_all[(j + 1) * CHUNK - 1:(j + 1) * CHUNK, h:h + 1] for j, h in pairs])

    row = _iota2((CHUNK, CHUNK), 0)
    col = _iota2((CHUNK, CHUNK), 1)
    decay = jnp.exp(jnp.where(row >= col, gcol - grow, -jnp.inf))
    kb = k * beta
    qk_kk = _bmm_nt(jnp.concatenate([q, kb], axis=1), k)
    qk = qk_kk[:, :CHUNK] * decay
    a_low = jnp.where(row > col, qk_kk[:, CHUNK:] * decay, 0.0)
    tinv = _unit_lower_inverse(a_low, row, col)
    egc = jnp.exp(gcol)
    uw = _bmm(tinv, jnp.concatenate([v * beta, kb * egc], axis=2))
    kd = k * jnp.exp(glast - gcol)
    kd_t = jnp.stack([kd[g].T for g in range(len(pairs))])
    kd_uw = _bmm(kd_t, uw)
    qk_uw = _bmm(qk, uw)
    q_eff = q * egc - qk_uw[:, :, GDN_DV:]
    eg_last = jnp.exp(glast)

    nw = nw_ref[...]
    s = s_scr[...]
    for j in range(MIX_STEP_CHUNKS):
        gs = slice(j * GDN_HEADS, (j + 1) * GDN_HEADS)
        sb = s.astype(BF16)
        o = _bmm(q_eff[gs], sb) + qk_uw[gs, :, :GDN_DV]
        s = s * eg_last[gs] - _bmm(kd_uw[gs, :, GDN_DV:], sb) + kd_uw[gs, :, :GDN_DV]
        on = o * lax.rsqrt(jnp.mean(o * o, axis=-1, keepdims=True) + NORM_EPS) * nw
        for h in range(GDN_HEADS):
            z = proj[rs(j), nqkv + h * GDN_DV:nqkv + (h + 1) * GDN_DV].astype(F32)
            o_ref[rs(j), h * GDN_DV:(h + 1) * GDN_DV] = (on[h] * _silu(z)).astype(o_ref.dtype)
    s_scr[...] = s


def _gdn(proj_pre, proj_main, small_pre, small_main, conv_w, gparams, norm_w, bsz, nstep):
    rows = proj_main.shape[0]
    ncols = 2 * GDN_QK + 2 * GDN_V
    nqkv = 2 * GDN_QK + GDN_V
    step_rows = MIX_STEP_CHUNKS * CHUNK

    def main_map(b, c):
        return (b * nstep + jnp.maximum(c - 1, 0), 0)

    return pl.pallas_call(
        _gdn_kernel,
        grid=(bsz, nstep + 1),
        in_specs=[
            pl.BlockSpec((step_rows, ncols), lambda b, c: (0, 0)),
            pl.BlockSpec((step_rows, ncols), main_map),
            pl.BlockSpec((step_rows, LANES), lambda b, c: (0, 0)),
            pl.BlockSpec((step_rows, LANES), main_map),
            pl.BlockSpec((CONV_K, nqkv), lambda b, c: (0, 0)),
            pl.BlockSpec((8, LANES), lambda b, c: (0, 0)),
            pl.BlockSpec((1, GDN_DV), lambda b, c: (0, 0)),
        ],
        out_specs=pl.BlockSpec((step_rows, GDN_V), main_map),
        out_shape=jax.ShapeDtypeStruct((rows, GDN_V), BF16),
        scratch_shapes=[
            pltpu.VMEM((GDN_HEADS, GDN_DK, GDN_DV), F32),
            pltpu.VMEM((8, nqkv), F32),
            pltpu.VMEM((8 + step_rows, nqkv), F32),
        ],
        compiler_params=pltpu.CompilerParams(
            dimension_semantics=("arbitrary", "arbitrary"), vmem_limit_bytes=VMEM_LIMIT),
        name="gdn",
    )(proj_pre, proj_main, small_pre, small_main, conv_w, gparams, norm_w)


def _gla_level(q, k, b, m):
    g = q.shape[0]
    ref = jnp.concatenate(
        [jnp.broadcast_to(b[:, t + m - 1:t + m, :], (g, 2 * m, GLA_DK)) for t in range(0, CHUNK, 2 * m)], axis=1)
    p = _bmm_nt(q * jnp.exp(jnp.minimum(b - ref, 0.0)), k * jnp.exp(jnp.minimum(ref - b, 0.0)))
    row = _iota2((CHUNK, CHUNK), 0)
    col = _iota2((CHUNK, CHUNK), 1)
    blk = 2 * m
    shift = blk.bit_length() - 1
    keep = jnp.logical_and((row >> shift) == (col >> shift),
                           jnp.logical_and((row & (blk - 1)) >= m, (col & (blk - 1)) < m))
    return jnp.where(keep, p, 0.0)


def _gla_kernel(qp_ref, kp_ref, vp_ref, rp_ref, sp_ref,
                qm_ref, km_ref, vm_ref, rm_ref, sm_ref,
                w2_ref, gb_ref, nw_ref, sel_ref, o_ref, s_scr):
    c = pl.program_id(1)
    is_pre = c == 0
    rows = MIX_STEP_CHUNKS * CHUNK

    @pl.when(is_pre)
    def _():
        s_scr[...] = jnp.zeros_like(s_scr)

    q_all = jnp.where(is_pre, qp_ref[...], qm_ref[...]).astype(F32) * (GLA_DK ** -0.5)
    k_all = jnp.where(is_pre, kp_ref[...], km_ref[...]).astype(F32)
    v_all = jnp.where(is_pre, vp_ref[...], vm_ref[...]).astype(F32)
    r_all = jnp.where(is_pre, rp_ref[...], rm_ref[...]).astype(F32)
    small = jnp.where(is_pre, sp_ref[...], sm_ref[...])

    gate = _dot(small, w2_ref[...]) + gb_ref[...]
    log_a = -_softplus(-gate) * (1.0 / GLA_GATE_NORMALIZER)
    rrow = _iota2((rows, rows), 0)
    rcol = _iota2((rows, rows), 1)
    tril_blk = jnp.where(jnp.logical_and(rrow >= rcol, (rrow >> 6) == (rcol >> 6)), 1.0, 0.0).astype(F32)
    b_all = _dot_hilo(tril_blk, log_a)

    pairs = [(j, h) for j in range(MIX_STEP_CHUNKS) for h in range(GLA_HEADS)]
    ng = len(pairs)

    def rs(j):
        return slice(j * CHUNK, (j + 1) * CHUNK)

    def gather(arr, width):
        return jnp.stack([arr[rs(j), h * width:(h + 1) * width] for j, h in pairs])

    q = gather(q_all, GLA_DK)
    k = gather(k_all, GLA_DK)
    b = gather(b_all, GLA_DK)
    v = gather(v_all, GLA_DV)

    scores = _gla_level(q, k, b, 32) + _gla_level(q, k, b, 16) + _gla_level(q, k, b, 8)

    diag = []
    for blk in range(CHUNK // 8):
        bs = b[:, blk * 8:(blk + 1) * 8, :]
        qs = q[:, blk * 8:(blk + 1) * 8, :]
        ks = k[:, blk * 8:(blk + 1) * 8, :]
        pj = [qs * (ks[:, j:j + 1, :] * jnp.exp(jnp.minimum(bs - bs[:, j:j + 1, :], 0.0))) for j in range(8)]
        p = jnp.concatenate(pj, axis=2).astype(BF16).reshape(ng * 8, 8 * GLA_DK)
        d = jnp.dot(p, sel_ref[blk], preferred_element_type=F32)
        diag.append(d.reshape(ng, 8, CHUNK))
    row = _iota2((CHUNK, CHUNK), 0)
    col = _iota2((CHUNK, CHUNK), 1)
    scores = scores + jnp.where(row >= col, jnp.concatenate(diag, axis=1), 0.0)

    blast = b[:, CHUNK - 1:CHUNK, :]
    qe = q * jnp.exp(b)
    kdec = k * jnp.exp(blast - b)
    v_t = jnp.stack([v[g].T for g in range(ng)])
    kv = _bmm(v_t, kdec)
    o_intra = _bmm(scores, v)
    eb_last = jnp.exp(blast)

    nw = nw_ref[...]
    st = s_scr[...]
    for j in range(MIX_STEP_CHUNKS):
        gs = slice(j * GLA_HEADS, (j + 1) * GLA_HEADS)
        o = _bmm_nt(qe[gs], st) + o_intra[gs]
        st = st * eb_last[gs] + kv[gs]
        on = o * lax.rsqrt(jnp.mean(o * o, axis=-1, keepdims=True) + NORM_EPS) * nw
        for h in range(GLA_HEADS):
            r = r_all[rs(j), h * GLA_DV:(h + 1) * GLA_DV]
            o_ref[rs(j), h * GLA_DV:(h + 1) * GLA_DV] = (on[h] * _silu(r)).astype(o_ref.dtype)
    s_scr[...] = st


def _gla(proj_pre, proj_main, small_pre, small_main, w2_pad, gate_b, norm_w, sel, bsz, nstep):
    rows = proj_main.shape[0]
    step_rows = MIX_STEP_CHUNKS * CHUNK
    q_blk = (2 * GDN_QK + 2 * GDN_V) // GLA_QK
    v_blk = (2 * GDN_QK + 2 * GDN_V + 2 * GLA_QK) // GLA_V

    def rmap(blk):
        return lambda b, c: (b * nstep + jnp.maximum(c - 1, 0), blk)

    def pmap(blk):
        return lambda b, c: (0, blk)

    return pl.pallas_call(
        _gla_kernel,
        grid=(bsz, nstep + 1),
        in_specs=[
            pl.BlockSpec((step_rows, GLA_QK), pmap(q_blk)),
            pl.BlockSpec((step_rows, GLA_QK), pmap(q_blk + 1)),
            pl.BlockSpec((step_rows, GLA_V), pmap(v_blk)),
            pl.BlockSpec((step_rows, GLA_V), pmap(v_blk + 1)),
            pl.BlockSpec((step_rows, LANES), pmap(0)),
            pl.BlockSpec((step_rows, GLA_QK), rmap(q_blk)),
            pl.BlockSpec((step_rows, GLA_QK), rmap(q_blk + 1)),
            pl.BlockSpec((step_rows, GLA_V), rmap(v_blk)),
            pl.BlockSpec((step_rows, GLA_V), rmap(v_blk + 1)),
            pl.BlockSpec((step_rows, LANES), rmap(0)),
            pl.BlockSpec((LANES, GLA_QK), lambda b, c: (0, 0)),
            pl.BlockSpec((1, GLA_QK), lambda b, c: (0, 0)),
            pl.BlockSpec((1, GLA_DV), lambda b, c: (0, 0)),
            pl.BlockSpec((CHUNK // 8, 8 * GLA_DK, CHUNK), lambda b, c: (0, 0, 0)),
        ],
        out_specs=pl.BlockSpec((step_rows, GLA_V), rmap(0)),
        out_shape=jax.ShapeDtypeStruct((rows, GLA_V), BF16),
        scratch_shapes=[pltpu.VMEM((GLA_HEADS, GLA_DV, GLA_DK), F32)],
        compiler_params=pltpu.CompilerParams(
            dimension_semantics=("arbitrary", "arbitrary"), vmem_limit_bytes=VMEM_LIMIT),
        name="gla",
    )(proj_pre, proj_pre, proj_pre, proj_pre, small_pre,
      proj_main, proj_main, proj_main, proj_main, small_main,
      w2_pad, gate_b, norm_w, sel)


def _outproj_kernel(x_ref, mg_ref, ml_ref, wg_ref, wl_ref, o_ref):
    o_ref[...] = (x_ref[...]
                  + jnp.dot(mg_ref[...], wg_ref[...], preferred_element_type=F32)
                  + jnp.dot(ml_ref[...], wl_ref[...], preferred_element_type=F32))


def _outproj(x2d, mg, ml, wo_g, wo_l, tm):
    rows, d = x2d.shape
    return pl.pallas_call(
        _outproj_kernel,
        grid=(rows // tm,),
        in_specs=[
            pl.BlockSpec((tm, d), lambda i: (i, 0)),
            pl.BlockSpec((tm, GDN_V), lambda i: (i, 0)),
            pl.BlockSpec((tm, GLA_V), lambda i: (i, 0)),
            pl.BlockSpec((GDN_V, d), lambda i: (0, 0)),
            pl.BlockSpec((GLA_V, d), lambda i: (0, 0)),
        ],
        out_specs=pl.BlockSpec((tm, d), lambda i: (i, 0)),
        out_shape=jax.ShapeDtypeStruct((rows, d), F32),
        compiler_params=pltpu.CompilerParams(
            dimension_semantics=("arbitrary",), vmem_limit_bytes=VMEM_LIMIT),
        name="outproj",
    )(x2d, mg, ml, wo_g, wo_l)


def _ffn_kernel(h_ref, nw_ref, wg_ref, wu_ref, wd_ref, fw_ref, o_ref, n_scr):
    f = pl.program_id(1)

    @pl.when(f == 0)
    def _():
        h = h_ref[...]
        ms = jnp.mean(h * h, axis=-1, keepdims=True)
        n_scr[...] = ((h * lax.rsqrt(ms + NORM_EPS)) * nw_ref[...]).astype(BF16)
        o_ref[...] = h

    n = n_scr[...]
    g = jnp.dot(n, wg_ref[...], preferred_element_type=F32)
    u = jnp.dot(n, wu_ref[...], preferred_element_type=F32)
    act = (_silu(g) * u).astype(BF16)
    o_ref[...] += jnp.dot(act, wd_ref[...], preferred_element_type=F32)

    @pl.when(f == pl.num_programs(1) - 1)
    def _():
        y = o_ref[...]
        ms = jnp.mean(y * y, axis=-1, keepdims=True)
        o_ref[...] = (y * lax.rsqrt(ms + NORM_EPS)) * fw_ref[...]


def _ffn(h2d, norm_w, w_gate, w_up, w_down, final_w, tm, tf):
    rows, d = h2d.shape
    dff = w_gate.shape[1]
    return pl.pallas_call(
        _ffn_kernel,
        grid=(rows // tm, dff // tf),
        in_specs=[
            pl.BlockSpec((tm, d), lambda i, f: (i, 0)),
            pl.BlockSpec((1, d), lambda i, f: (0, 0)),
            pl.BlockSpec((d, tf), lambda i, f: (0, f)),
            pl.BlockSpec((d, tf), lambda i, f: (0, f)),
            pl.BlockSpec((tf, d), lambda i, f: (f, 0)),
            pl.BlockSpec((1, d), lambda i, f: (0, 0)),
        ],
        out_specs=pl.BlockSpec((tm, d), lambda i, f: (i, 0)),
        out_shape=jax.ShapeDtypeStruct((rows, d), F32),
        scratch_shapes=[pltpu.VMEM((tm, d), BF16)],
        compiler_params=pltpu.CompilerParams(
            dimension_semantics=("arbitrary", "arbitrary"), vmem_limit_bytes=VMEM_LIMIT),
        name="ffn",
    )(h2d, norm_w, w_gate, w_up, w_down, final_w)


def _pick_tile(n, pref):
    t = min(n, pref)
    while n % t:
        t //= 2
    return t


def kernel(x, meta_tokens, attn_norm_w, w_in, gdn_conv_w, gdn_a_log, gdn_dt_bias, gdn_norm_w,
           gla_gate_w2, gla_gate_b, gla_norm_w, w_out, ffn_norm_w, w_gate, w_up, w_down, final_norm_w):
    bsz, seq, d = x.shape
    step_rows = MIX_STEP_CHUNKS * CHUNK
    assert seq % step_rows == 0 and attn_norm_w.shape[0] == 1
    nchunk = seq // CHUNK
    rows = bsz * seq
    x2d = x.reshape(rows, d)

    wi = w_in[0]
    off_small = 2 * GDN_QK + 2 * GDN_V
    off_gla = off_small + 2 * GDN_HEADS
    off_lr = off_gla + 2 * GLA_QK + 2 * GLA_V
    w_main = jnp.concatenate([wi[:, :off_small], wi[:, off_gla:off_lr]], axis=1).astype(BF16)
    w_small = jnp.concatenate(
        [wi[:, off_small:off_gla], wi[:, off_lr:],
         jnp.zeros((d, LANES - 2 * GDN_HEADS - GLA_GATE_RANK), F32)], axis=1).astype(BF16)
    assert w_main.shape[1] == MAIN_COLS
    nw_attn = attn_norm_w[0].reshape(1, d)
    gparams = jnp.zeros((8, LANES), F32)
    gparams = gparams.at[0, :GDN_HEADS].set(gdn_a_log[0]).at[1, :GDN_HEADS].set(gdn_dt_bias[0])
    w2_pad = jnp.zeros((LANES, GLA_QK), F32).at[2 * GDN_HEADS:2 * GDN_HEADS + GLA_GATE_RANK].set(gla_gate_w2[0])

    h_pre = jnp.concatenate([jnp.zeros((step_rows - N_META, d), x.dtype), meta_tokens.astype(x.dtype)], axis=0)

    kk = jnp.arange(8 * GLA_DK) // GLA_DK
    sel = (kk[None, :, None] + 8 * jnp.arange(CHUNK // 8)[:, None, None]
           == jnp.arange(CHUNK)[None, None, :]).astype(BF16)

    proj_main, small_main = _inproj(x2d, nw_attn, w_main, w_small, _pick_tile(rows, 512), 512)
    proj_pre, small_pre = _inproj(h_pre, nw_attn, w_main, w_small, step_rows, 512)

    nstep = seq // step_rows
    mg = _gdn(proj_pre, proj_main, small_pre, small_main, gdn_conv_w[0], gparams,
              gdn_norm_w[0].reshape(1, GDN_DV), bsz, nstep)
    ml = _gla(proj_pre, proj_main, small_pre, small_main, w2_pad.astype(BF16),
              gla_gate_b[0].reshape(1, GLA_QK), gla_norm_w[0].reshape(1, GLA_DV), sel, bsz, nstep)

    wo = w_out[0].astype(BF16)
    h2d = _outproj(x2d, mg, ml, wo[:GDN_V], wo[GDN_V:], _pick_tile(rows, 512))
    out = _ffn(h2d, ffn_norm_w[0].reshape(1, d), w_gate[0].astype(BF16), w_up[0].astype(BF16),
               w_down[0].astype(BF16), final_norm_w.reshape(1, d), _pick_tile(rows, 1024), 512)
    return out.reshape(bsz, seq, d)
```

```python
import functools

import jax
import jax.numpy as jnp
from jax import lax
from jax.experimental import pallas as pl
from jax.experimental.pallas import tpu as pltpu

F32 = jnp.float32
BF16 = jnp.bfloat16

N_META = 16
CONV_K = 4
GDN_HEADS = 8
GDN_DK = 128
GDN_DV = 128
GLA_HEADS = 4
GLA_DK = 128
GLA_DV = 256
GLA_GATE_RANK = 16
GLA_GATE_NORMALIZER = 16.0
GDN_QK = GDN_HEADS * GDN_DK
GDN_V = GDN_HEADS * GDN_DV
GLA_QK = GLA_HEADS * GLA_DK
GLA_V = GLA_HEADS * GLA_DV
NORM_EPS = 1e-6

CHUNK = 64
SUB = 16
MIX_STEP_CHUNKS = 4
LANES = 128
MAIN_COLS = 2 * GDN_QK + 2 * GDN_V + 2 * GLA_QK + 2 * GLA_V
VMEM_LIMIT = 56 * 1024 * 1024


def _dot(a, b):
    return jnp.dot(a.astype(BF16), b.astype(BF16), preferred_element_type=F32)


def _dot_nt(a, b):
    return lax.dot_general(a.astype(BF16), b.astype(BF16), (((1,), (1,)), ((), ())),
                           preferred_element_type=F32)


def _dot_hilo(a_exact, b):
    bh = b.astype(BF16)
    bl = (b - bh.astype(F32)).astype(BF16)
    ab = a_exact.astype(BF16)
    return (jnp.dot(ab, bh, preferred_element_type=F32)
            + jnp.dot(ab, bl, preferred_element_type=F32))


def _bmm(a, b):
    return jnp.einsum("gij,gjk->gik", a.astype(BF16), b.astype(BF16), preferred_element_type=F32)


def _bmm_nt(a, b):
    return jnp.einsum("gik,gjk->gij", a.astype(BF16), b.astype(BF16), preferred_element_type=F32)


def _sigmoid(x):
    return 1.0 / (1.0 + jnp.exp(-x))


def _silu(x):
    return x * _sigmoid(x)


def _softplus(x):
    return jnp.maximum(x, 0.0) + jnp.log(1.0 + jnp.exp(-jnp.abs(x)))


def _iota2(shape, dim):
    return lax.broadcasted_iota(jnp.int32, shape, dim)


def _l2norm_heads(y, scale):
    outs = []
    for h in range(y.shape[1] // LANES):
        yh = y[:, h * LANES:(h + 1) * LANES]
        outs.append(yh * (lax.rsqrt(jnp.sum(yh * yh, axis=-1, keepdims=True) + NORM_EPS) * scale))
    return jnp.concatenate(outs, axis=1)


def _inproj_kernel(x_ref, nw_ref, w_ref, ws_ref, cw_ref, carry_ref, o_ref, os_ref, tail_ref, tail_scr, acc_scr,
                   *, tn, steps_per_seq):
    i = pl.program_id(0)
    tm = x_ref.shape[0]

    @pl.when(i % steps_per_seq == 0)
    def _():
        tail_scr[...] = carry_ref[...]

    x = x_ref[...]
    ms = jnp.mean(x * x, axis=-1, keepdims=True)
    nb = ((x * lax.rsqrt(ms + NORM_EPS)) * nw_ref[...]).astype(BF16)
    os_ref[...] = jnp.dot(nb, ws_ref[...], preferred_element_type=F32)
    nqkv = 2 * GDN_QK + GDN_V
    gla_q0 = nqkv + GDN_V
    gla_r0 = gla_q0 + 2 * GLA_QK + GLA_V
    ntile = w_ref.shape[1] // tn

    def matmul_tile(j):
        cs = slice(j * tn, (j + 1) * tn)
        slot = j % 2
        acc_scr[slot, 8:8 + tm, :] = jnp.dot(nb, w_ref[:, cs], preferred_element_type=F32)
        if j * tn < nqkv:
            acc_scr[slot, 0:8, :] = tail_scr[:, cs]
            tail_scr[:, cs] = acc_scr[slot, tm:tm + 8, :]

    def epilogue_tile(j):
        c0 = j * tn
        cs = slice(c0, c0 + tn)
        slot = j % 2
        acc = acc_scr[slot, 8:8 + tm, :]
        if c0 < nqkv:
            y = acc * cw_ref[CONV_K - 1:CONV_K, cs]
            for t in range(1, CONV_K):
                y = y + acc_scr[slot, 8 - t:8 - t + tm, :] * cw_ref[CONV_K - 1 - t:CONV_K - t, cs]
            y = _silu(y)
            if c0 < GDN_QK:
                y = _l2norm_heads(y, GDN_DK ** -0.5)
            elif c0 < 2 * GDN_QK:
                y = _l2norm_heads(y, 1.0)
        elif c0 < gla_q0 or c0 >= gla_r0:
            y = _silu(acc)
        elif c0 < gla_q0 + GLA_QK:
            y = acc * (GLA_DK ** -0.5)
        else:
            y = acc
        o_ref[:, cs] = y.astype(o_ref.dtype)

    for j in range(ntile + 1):
        if j < ntile:
            matmul_tile(j)
        if j >= 1:
            epilogue_tile(j - 1)
    tail_ref[...] = tail_scr[...]


def _inproj(x2d, norm_w, w_main, w_small, conv_w, carry, tm, tn, steps_per_seq):
    rows, d = x2d.shape
    ncol = w_main.shape[1]
    nqkv = 2 * GDN_QK + GDN_V
    assert GDN_QK % tn == 0 and GDN_V % tn == 0 and GLA_QK % tn == 0 and tn % LANES == 0
    resident = pl.Buffered(1)
    return pl.pallas_call(
        functools.partial(_inproj_kernel, tn=tn, steps_per_seq=steps_per_seq),
        grid=(rows // tm,),
        in_specs=[
            pl.BlockSpec((tm, d), lambda i: (i, 0)),
            pl.BlockSpec((1, d), lambda i: (0, 0), pipeline_mode=resident),
            pl.BlockSpec((d, ncol), lambda i: (0, 0), pipeline_mode=resident),
            pl.BlockSpec((d, LANES), lambda i: (0, 0), pipeline_mode=resident),
            pl.BlockSpec((CONV_K, nqkv), lambda i: (0, 0), pipeline_mode=resident),
            pl.BlockSpec((8, nqkv), lambda i: (0, 0), pipeline_mode=resident),
        ],
        out_specs=[
            pl.BlockSpec((tm, ncol), lambda i: (i, 0)),
            pl.BlockSpec((tm, LANES), lambda i: (i, 0)),
            pl.BlockSpec((8, nqkv), lambda i: (0, 0)),
        ],
        out_shape=[
            jax.ShapeDtypeStruct((rows, ncol), BF16),
            jax.ShapeDtypeStruct((rows, LANES), F32),
            jax.ShapeDtypeStruct((8, nqkv), F32),
        ],
        scratch_shapes=[pltpu.VMEM((8, nqkv), F32), pltpu.VMEM((2, 8 + tm, tn), F32)],
        compiler_params=pltpu.CompilerParams(
            dimension_semantics=("arbitrary",), vmem_limit_bytes=VMEM_LIMIT),
        name="inproj",
    )(x2d, norm_w, w_main, w_small, conv_w, carry)


def _unit_lower_inverse(a, row, col):
    same16 = (row >> 4) == (col >> 4)
    same32 = (row >> 5) == (col >> 5)
    eye = jnp.where(row == col, 1.0, 0.0).astype(F32)
    a0 = jnp.where(same16, a, 0.0)
    x = eye - a0
    p = _bmm(a0, a0)
    x = x + _bmm(x, p)
    p = _bmm(p, p)
    x = x + _bmm(x, p)
    p = _bmm(p, p)
    x = x + _bmm(x, p)
    l1 = jnp.where(jnp.logical_and(same32, jnp.logical_not(same16)), a, 0.0)
    x = x - _bmm(_bmm(x, l1), x)
    l2 = jnp.where(same32, 0.0, a)
    x = x - _bmm(_bmm(x, l2), x)
    return x


def _gdn_kernel(pre_ref, main_ref, spre_ref, smain_ref, gp_ref, nw_ref, o_ref, s_scr):
    c = pl.program_id(1)
    is_pre = c == 0
    rows = MIX_STEP_CHUNKS * CHUNK

    @pl.when(is_pre)
    def _():
        s_scr[...] = jnp.zeros_like(s_scr)

    proj = jnp.where(is_pre, pre_ref[...], main_ref[...])
    small = jnp.where(is_pre, spre_ref[...], smain_ref[...])
    nqkv = 2 * GDN_QK + GDN_V

    lane = _iota2((rows, LANES), 1)
    gp = gp_ref[...]
    g_all = -jnp.exp(gp[0:1, :]) * _softplus(small + gp[1:2, :])
    g_all = jnp.where(lane < GDN_HEADS, g_all, 0.0)
    rrow = _iota2((rows, rows), 0)
    rcol = _iota2((rows, rows), 1)
    tril_blk = jnp.where(jnp.logical_and(rrow >= rcol, (rrow >> 6) == (rcol >> 6)), 1.0, 0.0).astype(F32)
    gc_all = _dot_hilo(tril_blk, g_all)
    beta_all = _sigmoid(small)

    pairs = [(j, h) for j in range(MIX_STEP_CHUNKS) for h in range(GDN_HEADS)]

    def rs(j):
        return slice(j * CHUNK, (j + 1) * CHUNK)

    def gather(arr, col0, width):
        return jnp.stack([arr[rs(j), col0 + h * width:col0 + (h + 1) * width] for j, h in pairs])

    q = gather(proj, 0, GDN_DK).astype(F32)
    k = gather(proj, GDN_QK, GDN_DK).astype(F32)
    v = gather(proj, 2 * GDN_QK, GDN_DV).astype(F32)
    beta = gather(beta_all, GDN_HEADS, 1)
    gcol = gather(gc_all, 0, 1)
    gc_t = [gc_all[rs(j), :].T for j in range(MIX_STEP_CHUNKS)]
    grow = jnp.stack([gc_t[j][h:h + 1, :] for j, h in pairs])
    glast = jnp.stack([gc_all[(j + 1) * CHUNK - 1:(j + 1) * CHUNK, h:h + 1] for j, h in pairs])

    row = _iota2((CHUNK, CHUNK), 0)
    col = _iota2((CHUNK, CHUNK), 1)
    decay = jnp.exp(jnp.where(row >= col, gcol - grow, -jnp.inf))
    kb = k * beta
    qk_kk = _bmm_nt(jnp.concatenate([q, kb], axis=1), k)
    qk = qk_kk[:, :CHUNK] * decay
    a_low = jnp.where(row > col, qk_kk[:, CHUNK:] * decay, 0.0)
    tinv = _unit_lower_inverse(a_low, row, col)
    egc = jnp.exp(gcol)
    uw = _bmm(tinv, jnp.concatenate([v * beta, kb * egc], axis=2))
    kd = k * jnp.exp(glast - gcol)
    kd_t = jnp.stack([kd[g].T for g in range(len(pairs))])
    kd_uw = _bmm(kd_t, uw)
    qk_uw = _bmm(qk, uw)
    q_eff = q * egc - qk_uw[:, :, GDN_DV:]
    eg_last = jnp.exp(glast)

    nw = nw_ref[...]
    s = s_scr[...]
    for j in range(MIX_STEP_CHUNKS):
        gs = slice(j * GDN_HEADS, (j + 1) * GDN_HEADS)
        sb = s.astype(BF16)
        o = _bmm(q_eff[gs], sb) + qk_uw[gs, :, :GDN_DV]
        s = s * eg_last[gs] - _bmm(kd_uw[gs, :, GDN_DV:], sb) + kd_uw[gs, :, :GDN_DV]
        on = o * lax.rsqrt(jnp.mean(o * o, axis=-1, keepdims=True) + NORM_EPS) * nw
        for h in range(GDN_HEADS):
            zg = proj[rs(j), nqkv + h * GDN_DV:nqkv + (h + 1) * GDN_DV].astype(F32)
            o_ref[rs(j), h * GDN_DV:(h + 1) * GDN_DV] = (on[h] * zg).astype(o_ref.dtype)
    s_scr[...] = s


def _gdn(proj_pre, proj_main, small_pre, small_main, gparams, norm_w, bsz, nstep):
    rows = proj_main.shape[0]
    ncols = 2 * GDN_QK + 2 * GDN_V
    step_rows = MIX_STEP_CHUNKS * CHUNK

    def main_map(b, c):
        return (b * nstep + jnp.maximum(c - 1, 0), 0)

    return pl.pallas_call(
        _gdn_kernel,
        grid=(bsz, nstep + 1),
        in_specs=[
            pl.BlockSpec((step_rows, ncols), lambda b, c: (0, 0)),
            pl.BlockSpec((step_rows, ncols), main_map),
            pl.BlockSpec((step_rows, LANES), lambda b, c: (0, 0)),
            pl.BlockSpec((step_rows, LANES), main_map),
            pl.BlockSpec((8, LANES), lambda b, c: (0, 0)),
            pl.BlockSpec((1, GDN_DV), lambda b, c: (0, 0)),
        ],
        out_specs=pl.BlockSpec((step_rows, GDN_V), main_map),
        out_shape=jax.ShapeDtypeStruct((rows, GDN_V), BF16),
        scratch_shapes=[pltpu.VMEM((GDN_HEADS, GDN_DK, GDN_DV), F32)],
        compiler_params=pltpu.CompilerParams(
            dimension_semantics=("arbitrary", "arbitrary"), vmem_limit_bytes=VMEM_LIMIT),
        name="gdn",
    )(proj_pre, proj_main, small_pre, small_main, gparams, norm_w)


def _gla_level(q, k, b, m):
    g = q.shape[0]
    ref = jnp.concatenate(
        [jnp.broadcast_to(b[:, t + m - 1:t + m, :], (g, 2 * m, GLA_DK)) for t in range(0, CHUNK, 2 * m)], axis=1)
    p = _bmm_nt(q * jnp.exp(jnp.minimum(b - ref, 0.0)), k * jnp.exp(jnp.minimum(ref - b, 0.0)))
    row = _iota2((CHUNK, CHUNK), 0)
    col = _iota2((CHUNK, CHUNK), 1)
    blk = 2 * m
    shift = blk.bit_length() - 1
    keep = jnp.logical_and((row >> shift) == (col >> shift),
                           jnp.logical_and((row & (blk - 1)) >= m, (col & (blk - 1)) < m))
    return jnp.where(keep, p, 0.0)


def _gla_kernel(qp_ref, kp_ref, vp_ref, rp_ref, sp_ref,
                qm_ref, km_ref, vm_ref, rm_ref, sm_ref,
                w2_ref, gb_ref, nw_ref, sel_ref, o_ref, s_scr):
    c = pl.program_id(1)
    is_pre = c == 0
    rows = MIX_STEP_CHUNKS * CHUNK

    @pl.when(is_pre)
    def _():
        s_scr[...] = jnp.zeros_like(s_scr)

    q_all = jnp.where(is_pre, qp_ref[...], qm_ref[...]).astype(F32)
    k_all = jnp.where(is_pre, kp_ref[...], km_ref[...]).astype(F32)
    v_all = jnp.where(is_pre, vp_ref[...], vm_ref[...]).astype(F32)
    r_all = jnp.where(is_pre, rp_ref[...], rm_ref[...]).astype(F32)
    small = jnp.where(is_pre, sp_ref[...], sm_ref[...])

    gate = _dot(small, w2_ref[...]) + gb_ref[...]
    log_a = -_softplus(-gate) * (1.0 / GLA_GATE_NORMALIZER)
    rrow = _iota2((rows, rows), 0)
    rcol = _iota2((rows, rows), 1)
    tril_blk = jnp.where(jnp.logical_and(rrow >= rcol, (rrow >> 6) == (rcol >> 6)), 1.0, 0.0).astype(F32)
    b_all = _dot_hilo(tril_blk, log_a)

    pairs = [(j, h) for j in range(MIX_STEP_CHUNKS) for h in range(GLA_HEADS)]
    ng = len(pairs)

    def rs(j):
        return slice(j * CHUNK, (j + 1) * CHUNK)

    def gather(arr, width):
        return jnp.stack([arr[rs(j), h * width:(h + 1) * width] for j, h in pairs])

    q = gather(q_all, GLA_DK)
    k = gather(k_all, GLA_DK)
    b = gather(b_all, GLA_DK)
    v = gather(v_all, GLA_DV)

    scores = _gla_level(q, k, b, 32) + _gla_level(q, k, b, 16) + _gla_level(q, k, b, 8)

    diag = []
    for blk in range(CHUNK // 8):
        bs = b[:, blk * 8:(blk + 1) * 8, :]
        qs = q[:, blk * 8:(blk + 1) * 8, :]
        ks = k[:, blk * 8:(blk + 1) * 8, :]
        pj = [qs * (ks[:, j:j + 1, :] * jnp.exp(jnp.minimum(bs - bs[:, j:j + 1, :], 0.0))) for j in range(8)]
        p = jnp.concatenate(pj, axis=2).astype(BF16).reshape(ng * 8, 8 * GLA_DK)
        d = jnp.dot(p, sel_ref[blk], preferred_element_type=F32)
        diag.append(d.reshape(ng, 8, CHUNK))
    row = _iota2((CHUNK, CHUNK), 0)
    col = _iota2((CHUNK, CHUNK), 1)
    scores = scores + jnp.where(row >= col, jnp.concatenate(diag, axis=1), 0.0)

    blast = b[:, CHUNK - 1:CHUNK, :]
    qe = q * jnp.exp(b)
    kdec = k * jnp.exp(blast - b)
    v_t = jnp.stack([v[g].T for g in range(ng)])
    kv = _bmm(v_t, kdec)
    o_intra = _bmm(scores, v)
    eb_last = jnp.exp(blast)

    nw = nw_ref[...]
    st = s_scr[...]
    for j in range(MIX_STEP_CHUNKS):
        gs = slice(j * GLA_HEADS, (j + 1) * GLA_HEADS)
        o = _bmm_nt(qe[gs], st) + o_intra[gs]
        st = st * eb_last[gs] + kv[gs]
        on = o * lax.rsqrt(jnp.mean(o * o, axis=-1, keepdims=True) + NORM_EPS) * nw
        for h in range(GLA_HEADS):
            rg = r_all[rs(j), h * GLA_DV:(h + 1) * GLA_DV]
            o_ref[rs(j), h * GLA_DV:(h + 1) * GLA_DV] = (on[h] * rg).astype(o_ref.dtype)
    s_scr[...] = st


def _gla(proj_pre, proj_main, small_pre, small_main, w2_pad, gate_b, norm_w, sel, bsz, nstep):
    rows = proj_main.shape[0]
    step_rows = MIX_STEP_CHUNKS * CHUNK
    q_blk = (2 * GDN_QK + 2 * GDN_V) // GLA_QK
    v_blk = (2 * GDN_QK + 2 * GDN_V + 2 * GLA_QK) // GLA_V

    def rmap(blk):
        return lambda b, c: (b * nstep + jnp.maximum(c - 1, 0), blk)

    def pmap(blk):
        return lambda b, c: (0, blk)

    return pl.pallas_call(
        _gla_kernel,
        grid=(bsz, nstep + 1),
        in_specs=[
            pl.BlockSpec((step_rows, GLA_QK), pmap(q_blk)),
            pl.BlockSpec((step_rows, GLA_QK), pmap(q_blk + 1)),
            pl.BlockSpec((step_rows, GLA_V), pmap(v_blk)),
            pl.BlockSpec((step_rows, GLA_V), pmap(v_blk + 1)),
            pl.BlockSpec((step_rows, LANES), pmap(0)),
            pl.BlockSpec((step_rows, GLA_QK), rmap(q_blk)),
            pl.BlockSpec((step_rows, GLA_QK), rmap(q_blk + 1)),
            pl.BlockSpec((step_rows, GLA_V), rmap(v_blk)),
            pl.BlockSpec((step_rows, GLA_V), rmap(v_blk + 1)),
            pl.BlockSpec((step_rows, LANES), rmap(0)),
            pl.BlockSpec((LANES, GLA_QK), lambda b, c: (0, 0)),
            pl.BlockSpec((1, GLA_QK), lambda b, c: (0, 0)),
            pl.BlockSpec((1, GLA_DV), lambda b, c: (0, 0)),
            pl.BlockSpec((CHUNK // 8, 8 * GLA_DK, CHUNK), lambda b, c: (0, 0, 0)),
        ],
        out_specs=pl.BlockSpec((step_rows, GLA_V), rmap(0)),
        out_shape=jax.ShapeDtypeStruct((rows, GLA_V), BF16),
        scratch_shapes=[pltpu.VMEM((GLA_HEADS, GLA_DV, GLA_DK), F32)],
        compiler_params=pltpu.CompilerParams(
            dimension_semantics=("arbitrary", "arbitrary"), vmem_limit_bytes=VMEM_LIMIT),
        name="gla",
    )(proj_pre, proj_pre, proj_pre, proj_pre, small_pre,
      proj_main, proj_main, proj_main, proj_main, small_main,
      w2_pad, gate_b, norm_w, sel)


def _outproj_kernel(x_ref, mg_ref, ml_ref, wg_ref, wl_ref, o_ref):
    o_ref[...] = (x_ref[...]
                  + jnp.dot(mg_ref[...], wg_ref[...], preferred_element_type=F32)
                  + jnp.dot(ml_ref[...], wl_ref[...], preferred_element_type=F32))


def _outproj(x2d, mg, ml, wo_g, wo_l, tm):
    rows, d = x2d.shape
    return pl.pallas_call(
        _outproj_kernel,
        grid=(rows // tm,),
        in_specs=[
            pl.BlockSpec((tm, d), lambda i: (i, 0)),
            pl.BlockSpec((tm, GDN_V), lambda i: (i, 0)),
            pl.BlockSpec((tm, GLA_V), lambda i: (i, 0)),
            pl.BlockSpec((GDN_V, d), lambda i: (0, 0)),
            pl.BlockSpec((GLA_V, d), lambda i: (0, 0)),
        ],
        out_specs=pl.BlockSpec((tm, d), lambda i: (i, 0)),
        out_shape=jax.ShapeDtypeStruct((rows, d), F32),
        compiler_params=pltpu.CompilerParams(
            dimension_semantics=("arbitrary",), vmem_limit_bytes=VMEM_LIMIT),
        name="outproj",
    )(x2d, mg, ml, wo_g, wo_l)


def _ffn_kernel(h_ref, nw_ref, wg_ref, wu_ref, wd_ref, fw_ref, o_ref, n_scr):
    f = pl.program_id(1)

    @pl.when(f == 0)
    def _():
        h = h_ref[...]
        ms = jnp.mean(h * h, axis=-1, keepdims=True)
        n_scr[...] = ((h * lax.rsqrt(ms + NORM_EPS)) * nw_ref[...]).astype(BF16)
        o_ref[...] = h

    n = n_scr[...]
    g = jnp.dot(n, wg_ref[...], preferred_element_type=F32)
    u = jnp.dot(n, wu_ref[...], preferred_element_type=F32)
    act = (_silu(g) * u).astype(BF16)
    o_ref[...] += jnp.dot(act, wd_ref[...], preferred_element_type=F32)

    @pl.when(f == pl.num_programs(1) - 1)
    def _():
        y = o_ref[...]
        ms = jnp.mean(y * y, axis=-1, keepdims=True)
        o_ref[...] = (y * lax.rsqrt(ms + NORM_EPS)) * fw_ref[...]


def _ffn(h2d, norm_w, w_gate, w_up, w_down, final_w, tm, tf):
    rows, d = h2d.shape
    dff = w_gate.shape[1]
    return pl.pallas_call(
        _ffn_kernel,
        grid=(rows // tm, dff // tf),
        in_specs=[
            pl.BlockSpec((tm, d), lambda i, f: (i, 0)),
            pl.BlockSpec((1, d), lambda i, f: (0, 0)),
            pl.BlockSpec((d, tf), lambda i, f: (0, f)),
            pl.BlockSpec((d, tf), lambda i, f: (0, f)),
            pl.BlockSpec((tf, d), lambda i, f: (f, 0)),
            pl.BlockSpec((1, d), lambda i, f: (0, 0)),
        ],
        out_specs=pl.BlockSpec((tm, d), lambda i, f: (i, 0)),
        out_shape=jax.ShapeDtypeStruct((rows, d), F32),
        scratch_shapes=[pltpu.VMEM((tm, d), BF16)],
        compiler_params=pltpu.CompilerParams(
            dimension_semantics=("arbitrary", "arbitrary"), vmem_limit_bytes=VMEM_LIMIT),
        name="ffn",
    )(h2d, norm_w, w_gate, w_up, w_down, final_w)


def _pick_tile(n, pref):
    t = min(n, pref)
    while n % t:
        t //= 2
    return t


def kernel(x, meta_tokens, attn_norm_w, w_in, gdn_conv_w, gdn_a_log, gdn_dt_bias, gdn_norm_w,
           gla_gate_w2, gla_gate_b, gla_norm_w, w_out, ffn_norm_w, w_gate, w_up, w_down, final_norm_w):
    bsz, seq, d = x.shape
    step_rows = MIX_STEP_CHUNKS * CHUNK
    assert seq % step_rows == 0 and attn_norm_w.shape[0] == 1
    nchunk = seq // CHUNK
    rows = bsz * seq
    x2d = x.reshape(rows, d)

    wi = w_in[0]
    off_small = 2 * GDN_QK + 2 * GDN_V
    off_gla = off_small + 2 * GDN_HEADS
    off_lr = off_gla + 2 * GLA_QK + 2 * GLA_V
    w_main = jnp.concatenate([wi[:, :off_small], wi[:, off_gla:off_lr]], axis=1).astype(BF16)
    w_small = jnp.concatenate(
        [wi[:, off_small:off_gla], wi[:, off_lr:],
         jnp.zeros((d, LANES - 2 * GDN_HEADS - GLA_GATE_RANK), F32)], axis=1).astype(BF16)
    assert w_main.shape[1] == MAIN_COLS
    nw_attn = attn_norm_w[0].reshape(1, d)
    gparams = jnp.zeros((8, LANES), F32)
    gparams = gparams.at[0, :GDN_HEADS].set(gdn_a_log[0]).at[1, :GDN_HEADS].set(gdn_dt_bias[0])
    w2_pad = jnp.zeros((LANES, GLA_QK), F32).at[2 * GDN_HEADS:2 * GDN_HEADS + GLA_GATE_RANK].set(gla_gate_w2[0])

    h_pre = jnp.concatenate([jnp.zeros((step_rows - N_META, d), x.dtype), meta_tokens.astype(x.dtype)], axis=0)

    kk = jnp.arange(8 * GLA_DK) // GLA_DK
    sel = (kk[None, :, None] + 8 * jnp.arange(CHUNK // 8)[:, None, None]
           == jnp.arange(CHUNK)[None, None, :]).astype(BF16)

    conv_w = gdn_conv_w[0]
    proj_pre, small_pre, carry = _inproj(h_pre, nw_attn, w_main, w_small, conv_w,
                                         jnp.zeros((8, conv_w.shape[1]), F32), step_rows, 512, 1)
    tm_in = _pick_tile(seq, 512)
    proj_main, small_main, _ = _inproj(x2d, nw_attn, w_main, w_small, conv_w, carry, tm_in, 256, seq // tm_in)

    nstep = seq // step_rows
    mg = _gdn(proj_pre, proj_main, small_pre, small_main, gparams,
              gdn_norm_w[0].reshape(1, GDN_DV), bsz, nstep)
    ml = _gla(proj_pre, proj_main, small_pre, small_main, w2_pad.astype(BF16),
              gla_gate_b[0].reshape(1, GLA_QK), gla_norm_w[0].reshape(1, GLA_DV), sel, bsz, nstep)

    wo = w_out[0].astype(BF16)
    h2d = _outproj(x2d, mg, ml, wo[:GDN_V], wo[GDN_V:], _pick_tile(rows, 512))
    out = _ffn(h2d, ffn_norm_w[0].reshape(1, d), w_gate[0].astype(BF16), w_up[0].astype(BF16),
               w_down[0].astype(BF16), final_norm_w.reshape(1, d), _pick_tile(rows, 1024), 512)
    return out.reshape(bsz, seq, d)
```

```python
import functools

import jax
import jax.numpy as jnp
from jax import lax
from jax.experimental import pallas as pl
from jax.experimental.pallas import tpu as pltpu

F32 = jnp.float32
BF16 = jnp.bfloat16

N_META = 16
CONV_K = 4
GDN_HEADS = 8
GDN_DK = 128
GDN_DV = 128
GLA_HEADS = 4
GLA_DK = 128
GLA_DV = 256
GLA_GATE_RANK = 16
GLA_GATE_NORMALIZER = 16.0
GDN_QK = GDN_HEADS * GDN_DK
GDN_V = GDN_HEADS * GDN_DV
GLA_QK = GLA_HEADS * GLA_DK
GLA_V = GLA_HEADS * GLA_DV
NORM_EPS = 1e-6

CHUNK = 64
SUB = 16
MIX_STEP_CHUNKS = 4
LANES = 128
MAIN_COLS = 2 * GDN_QK + 2 * GDN_V + 2 * GLA_QK + 2 * GLA_V
VMEM_LIMIT = 56 * 1024 * 1024
LOG2E = 1.4426950408889634


def _dot(a, b):
    return jnp.dot(a.astype(BF16), b.astype(BF16), preferred_element_type=F32)


def _dot_nt(a, b):
    return lax.dot_general(a.astype(BF16), b.astype(BF16), (((1,), (1,)), ((), ())),
                           preferred_element_type=F32)


def _dot_hilo(a_exact, b):
    bh = b.astype(BF16)
    bl = (b - bh.astype(F32)).astype(BF16)
    ab = a_exact.astype(BF16)
    return (jnp.dot(ab, bh, preferred_element_type=F32)
            + jnp.dot(ab, bl, preferred_element_type=F32))


def _bmm(a, b):
    return jnp.einsum("gij,gjk->gik", a.astype(BF16), b.astype(BF16), preferred_element_type=F32)


def _bmm_nt(a, b):
    return jnp.einsum("gik,gjk->gij", a.astype(BF16), b.astype(BF16), preferred_element_type=F32)


def _sigmoid(x):
    return 1.0 / (1.0 + jnp.exp2(x * (-LOG2E)))


def _silu(x):
    return x * _sigmoid(x)


def _softplus(x):
    return jnp.maximum(x, 0.0) + jnp.log(1.0 + jnp.exp(-jnp.abs(x)))


def _iota2(shape, dim):
    return lax.broadcasted_iota(jnp.int32, shape, dim)


def _l2norm_heads(y, scale):
    outs = []
    for h in range(y.shape[1] // LANES):
        yh = y[:, h * LANES:(h + 1) * LANES]
        outs.append(yh * (lax.rsqrt(jnp.sum(yh * yh, axis=-1, keepdims=True) + NORM_EPS) * scale))
    return jnp.concatenate(outs, axis=1)


def _inproj_kernel(x_ref, nw_ref, w_ref, ws_ref, cw_ref, carry_ref, o_ref, os_ref, tail_ref, tail_scr,
                   acc_scr, *, tn, steps_per_seq):
    i = pl.program_id(0)
    tm = x_ref.shape[0]

    @pl.when(i % steps_per_seq == 0)
    def _():
        tail_scr[...] = carry_ref[...]

    x = x_ref[...]
    ms = jnp.mean(x * x, axis=-1, keepdims=True)
    nb = ((x * lax.rsqrt(ms + NORM_EPS)) * nw_ref[...]).astype(BF16)
    os_ref[...] = jnp.dot(nb, ws_ref[...], preferred_element_type=F32)
    nqkv = 2 * GDN_QK + GDN_V
    gla_q0 = nqkv + GDN_V
    gla_r0 = gla_q0 + 2 * GLA_QK + GLA_V
    ntile = w_ref.shape[1] // tn

    def matmul_tile(j):
        cs = slice(j * tn, (j + 1) * tn)
        acc_scr[8:8 + tm, :] = jnp.dot(nb, w_ref[:, cs], preferred_element_type=F32)
        if j * tn < nqkv:
            acc_scr[0:8, :] = tail_scr[:, cs]
            tail_scr[:, cs] = acc_scr[tm:tm + 8, :]

    def epilogue_tile(j):
        c0 = j * tn
        cs = slice(c0, c0 + tn)
        acc = acc_scr[8:8 + tm, :]
        if c0 < nqkv:
            y = acc * cw_ref[CONV_K - 1:CONV_K, cs]
            for t in range(1, CONV_K):
                y = y + acc_scr[8 - t:8 - t + tm, :] * cw_ref[CONV_K - 1 - t:CONV_K - t, cs]
            y = _silu(y)
            if c0 < GDN_QK:
                y = _l2norm_heads(y, GDN_DK ** -0.5)
            elif c0 < 2 * GDN_QK:
                y = _l2norm_heads(y, 1.0)
        elif c0 < gla_q0 or c0 >= gla_r0:
            y = _silu(acc)
        elif c0 < gla_q0 + GLA_QK:
            y = acc * (GLA_DK ** -0.5)
        else:
            y = acc
        o_ref[:, cs] = y.astype(o_ref.dtype)

    for j in range(ntile):
        matmul_tile(j)
        epilogue_tile(j)
    tail_ref[...] = tail_scr[...]


def _inproj(x2d, norm_w, w_main, w_small, conv_w, carry, tm, tn, steps_per_seq):
    rows, d = x2d.shape
    ncol = w_main.shape[1]
    nqkv = 2 * GDN_QK + GDN_V
    assert GDN_QK % tn == 0 and GDN_V % tn == 0 and GLA_QK % tn == 0 and tn % LANES == 0
    resident = pl.Buffered(1)
    return pl.pallas_call(
        functools.partial(_inproj_kernel, tn=tn, steps_per_seq=steps_per_seq),
        grid=(rows // tm,),
        in_specs=[
            pl.BlockSpec((tm, d), lambda i: (i, 0)),
            pl.BlockSpec((1, d), lambda i: (0, 0), pipeline_mode=resident),
            pl.BlockSpec((d, ncol), lambda i: (0, 0), pipeline_mode=resident),
            pl.BlockSpec((d, LANES), lambda i: (0, 0), pipeline_mode=resident),
            pl.BlockSpec((CONV_K, nqkv), lambda i: (0, 0), pipeline_mode=resident),
            pl.BlockSpec((8, nqkv), lambda i: (0, 0), pipeline_mode=resident),
        ],
        out_specs=[
            pl.BlockSpec((tm, ncol), lambda i: (i, 0)),
            pl.BlockSpec((tm, LANES), lambda i: (i, 0)),
            pl.BlockSpec((8, nqkv), lambda i: (0, 0)),
        ],
        out_shape=[
            jax.ShapeDtypeStruct((rows, ncol), BF16),
            jax.ShapeDtypeStruct((rows, LANES), F32),
            jax.ShapeDtypeStruct((8, nqkv), F32),
        ],
        scratch_shapes=[pltpu.VMEM((8, nqkv), F32), pltpu.VMEM((8 + tm, tn), F32)],
        compiler_params=pltpu.CompilerParams(
            dimension_semantics=("arbitrary",), vmem_limit_bytes=VMEM_LIMIT),
        name="inproj",
    )(x2d, norm_w, w_main, w_small, conv_w, carry)


def _unit_lower_inverse(a, row, col):
    same16 = (row >> 4) == (col >> 4)
    same32 = (row >> 5) == (col >> 5)
    eye = jnp.where(row == col, 1.0, 0.0).astype(F32)
    a0 = jnp.where(same16, a, 0.0)
    x = eye - a0
    p = _bmm(a0, a0)
    yield
    x = x + _bmm(x, p)
    p = _bmm(p, p)
    yield
    x = x + _bmm(x, p)
    p = _bmm(p, p)
    yield
    x = x + _bmm(x, p)
    yield
    l1 = jnp.where(jnp.logical_and(same32, jnp.logical_not(same16)), a, 0.0)
    t = _bmm(x, l1)
    yield
    x = x - _bmm(t, x)
    yield
    l2 = jnp.where(same32, 0.0, a)
    t = _bmm(x, l2)
    yield
    x = x - _bmm(t, x)
    return x


def _gdn_kernel(pre_ref, main_ref, spre_ref, smain_ref, gp_ref, nw_ref, o_ref, s_scr):
    c = pl.program_id(1)
    is_pre = c == 0
    rows = MIX_STEP_CHUNKS * CHUNK

    @pl.when(is_pre)
    def _():
        s_scr[...] = jnp.zeros_like(s_scr)

    proj = jnp.where(is_pre, pre_ref[...], main_ref[...])
    small = jnp.where(is_pre, spre_ref[...], smain_ref[...])
    nqkv = 2 * GDN_QK + GDN_V

    lane = _iota2((rows, LANES), 1)
    gp = gp_ref[...]
    g_all = -jnp.exp(gp[0:1, :]) * _softplus(small + gp[1:2, :])
    g_all = jnp.where(lane < GDN_HEADS, g_all, 0.0)
    rrow = _iota2((rows, rows), 0)
    rcol = _iota2((rows, rows), 1)
    tril_blk = jnp.where(jnp.logical_and(rrow >= rcol, (rrow >> 6) == (rcol >> 6)), 1.0, 0.0).astype(F32)
    gc_all = _dot_hilo(tril_blk, g_all) * LOG2E
    beta_all = _sigmoid(small)

    pairs = [(j, h) for j in range(MIX_STEP_CHUNKS) for h in range(GDN_HEADS)]

    def rs(j):
        return slice(j * CHUNK, (j + 1) * CHUNK)

    def gather(arr, col0, width):
        return jnp.stack([arr[rs(j), col0 + h * width:col0 + (h + 1) * width] for j, h in pairs])

    q = gather(proj, 0, GDN_DK).astype(F32)
    k = gather(proj, GDN_QK, GDN_DK).astype(F32)
    v = gather(proj, 2 * GDN_QK, GDN_DV).astype(F32)
    beta = gather(beta_all, GDN_HEADS, 1)
    gcol = gather(gc_all, 0, 1)
    gc_t = [gc_all[rs(j), :].T for j in range(MIX_STEP_CHUNKS)]
    grow = jnp.stack([gc_t[j][h:h + 1, :] for j, h in pairs])
    glast = jnp.stack([gc_all[(j + 1) * CHUNK - 1:(j + 1) * CHUNK, h:h + 1] for j, h in pairs])

    row = _iota2((CHUNK, CHUNK), 0)
    col = _iota2((CHUNK, CHUNK), 1)
    decay = jnp.exp2(jnp.where(row >= col, gcol - grow, -jnp.inf))
    kb = k * beta
    qk_kk = _bmm_nt(jnp.concatenate([q, kb], axis=1), k)
    yield
    qk = qk_kk[:, :CHUNK] * decay
    a_low = jnp.where(row > col, qk_kk[:, CHUNK:] * decay, 0.0)
    tinv = yield from _unit_lower_inverse(a_low, row, col)
    egc = jnp.exp2(gcol)
    uw = _bmm(tinv, jnp.concatenate([v * beta, kb * egc], axis=2))
    yield
    kd = k * jnp.exp2(glast - gcol)
    kd_t = jnp.stack([kd[g].T for g in range(len(pairs))])
    kd_uw = _bmm(kd_t, uw)
    yield
    qk_uw = _bmm(qk, uw)
    q_eff = q * egc - qk_uw[:, :, GDN_DV:]
    eg_last = jnp.exp2(glast)
    yield

    nw = nw_ref[...]
    s = s_scr[...]
    for j in range(MIX_STEP_CHUNKS):
        gs = slice(j * GDN_HEADS, (j + 1) * GDN_HEADS)
        sb = s.astype(BF16)
        o = _bmm(q_eff[gs], sb) + qk_uw[gs, :, :GDN_DV]
        s = s * eg_last[gs] - _bmm(kd_uw[gs, :, GDN_DV:], sb) + kd_uw[gs, :, :GDN_DV]
        on = o * lax.rsqrt(jnp.mean(o * o, axis=-1, keepdims=True) + NORM_EPS) * nw
        for h in range(GDN_HEADS):
            zg = proj[rs(j), nqkv + h * GDN_DV:nqkv + (h + 1) * GDN_DV].astype(F32)
            o_ref[rs(j), h * GDN_DV:(h + 1) * GDN_DV] = (on[h] * zg).astype(o_ref.dtype)
        yield
    s_scr[...] = s


def _gla_level(q, k, b, m):
    g = q.shape[0]
    ref = jnp.concatenate(
        [jnp.broadcast_to(b[:, t + m - 1:t + m, :], (g, 2 * m, GLA_DK)) for t in range(0, CHUNK, 2 * m)], axis=1)
    p = _bmm_nt(q * jnp.exp2(jnp.minimum(b - ref, 0.0)), k * jnp.exp2(jnp.minimum(ref - b, 0.0)))
    row = _iota2((CHUNK, CHUNK), 0)
    col = _iota2((CHUNK, CHUNK), 1)
    blk = 2 * m
    shift = blk.bit_length() - 1
    keep = jnp.logical_and((row >> shift) == (col >> shift),
                           jnp.logical_and((row & (blk - 1)) >= m, (col & (blk - 1)) < m))
    return jnp.where(keep, p, 0.0)


def _gla_kernel(qp_ref, kp_ref, vp_ref, rp_ref, sp_ref,
                qm_ref, km_ref, vm_ref, rm_ref, sm_ref,
                w2_ref, gb_ref, nw_ref, sel_ref, o_ref, s_scr):
    c = pl.program_id(1)
    is_pre = c == 0
    rows = MIX_STEP_CHUNKS * CHUNK

    @pl.when(is_pre)
    def _():
        s_scr[...] = jnp.zeros_like(s_scr)

    q_all = jnp.where(is_pre, qp_ref[...], qm_ref[...]).astype(F32)
    k_all = jnp.where(is_pre, kp_ref[...], km_ref[...]).astype(F32)
    v_all = jnp.where(is_pre, vp_ref[...], vm_ref[...]).astype(F32)
    r_all = jnp.where(is_pre, rp_ref[...], rm_ref[...]).astype(F32)
    small = jnp.where(is_pre, sp_ref[...], sm_ref[...])

    gate = _dot(small, w2_ref[...]) + gb_ref[...]
    log_a = -_softplus(-gate) * (1.0 / GLA_GATE_NORMALIZER)
    rrow = _iota2((rows, rows), 0)
    rcol = _iota2((rows, rows), 1)
    tril_blk = jnp.where(jnp.logical_and(rrow >= rcol, (rrow >> 6) == (rcol >> 6)), 1.0, 0.0).astype(F32)
    b_all = _dot_hilo(tril_blk, log_a) * LOG2E

    pairs = [(j, h) for j in range(MIX_STEP_CHUNKS) for h in range(GLA_HEADS)]
    ng = len(pairs)

    def rs(j):
        return slice(j * CHUNK, (j + 1) * CHUNK)

    def gather(arr, width):
        return jnp.stack([arr[rs(j), h * width:(h + 1) * width] for j, h in pairs])

    q = gather(q_all, GLA_DK)
    k = gather(k_all, GLA_DK)
    b = gather(b_all, GLA_DK)
    v = gather(v_all, GLA_DV)

    yield
    scores = _gla_level(q, k, b, 32)
    yield
    scores = scores + _gla_level(q, k, b, 16)
    yield
    scores = scores + _gla_level(q, k, b, 8)
    yield

    diag = []
    for blk in range(CHUNK // 8):
        bs = b[:, blk * 8:(blk + 1) * 8, :]
        qs = q[:, blk * 8:(blk + 1) * 8, :]
        ks = k[:, blk * 8:(blk + 1) * 8, :]
        pj = [qs * (ks[:, j:j + 1, :] * jnp.exp2(jnp.minimum(bs - bs[:, j:j + 1, :], 0.0))) for j in range(8)]
        p = jnp.concatenate(pj, axis=2).astype(BF16).reshape(ng * 8, 8 * GLA_DK)
        d = jnp.dot(p, sel_ref[blk], preferred_element_type=F32)
        diag.append(d.reshape(ng, 8, CHUNK))
        yield
    row = _iota2((CHUNK, CHUNK), 0)
    col = _iota2((CHUNK, CHUNK), 1)
    scores = scores + jnp.where(row >= col, jnp.concatenate(diag, axis=1), 0.0)

    blast = b[:, CHUNK - 1:CHUNK, :]
    qe = q * jnp.exp2(b)
    kdec = k * jnp.exp2(blast - b)
    v_t = jnp.stack([v[g].T for g in range(ng)])
    kv = _bmm(v_t, kdec)
    o_intra = _bmm(scores, v)
    eb_last = jnp.exp2(blast)
    yield

    nw = nw_ref[...]
    st = s_scr[...]
    for j in range(MIX_STEP_CHUNKS):
        gs = slice(j * GLA_HEADS, (j + 1) * GLA_HEADS)
        o = _bmm_nt(qe[gs], st) + o_intra[gs]
        st = st * eb_last[gs] + kv[gs]
        on = o * lax.rsqrt(jnp.mean(o * o, axis=-1, keepdims=True) + NORM_EPS) * nw
        for h in range(GLA_HEADS):
            rg = r_all[rs(j), h * GLA_DV:(h + 1) * GLA_DV]
            o_ref[rs(j), h * GLA_DV:(h + 1) * GLA_DV] = (on[h] * rg).astype(o_ref.dtype)
        yield
    s_scr[...] = st


def _mixer_kernel(gpre_ref, gmain_ref, spre_ref, smain_ref, gp_ref, gnw_ref,
                  qp_ref, kp_ref, vp_ref, rp_ref, qm_ref, km_ref, vm_ref, rm_ref,
                  w2_ref, gb_ref, lnw_ref, sel_ref, o_ref, gdn_s, gla_s):
    streams = [
        _gdn_kernel(gpre_ref, gmain_ref, spre_ref, smain_ref, gp_ref, gnw_ref, o_ref.at[:, :GDN_V], gdn_s),
        _gla_kernel(qp_ref, kp_ref, vp_ref, rp_ref, spre_ref, qm_ref, km_ref, vm_ref, rm_ref, smain_ref,
                    w2_ref, gb_ref, lnw_ref, sel_ref, o_ref.at[:, GDN_V:], gla_s),
    ]
    while streams:
        for stream in list(streams):
            if next(stream, StopIteration) is StopIteration:
                streams.remove(stream)


def _mixer(proj_pre, proj_main, small_pre, small_main, gparams, gdn_nw, w2_pad, gate_b, gla_nw, sel,
           bsz, nstep):
    rows = proj_main.shape[0]
    step_rows = MIX_STEP_CHUNKS * CHUNK
    gdn_cols = 2 * GDN_QK + 2 * GDN_V
    q_blk = gdn_cols // GLA_QK
    v_blk = (gdn_cols + 2 * GLA_QK) // GLA_V

    def rmap(blk):
        return lambda b, c: (b * nstep + jnp.maximum(c - 1, 0), blk)

    def pmap(blk):
        return lambda b, c: (0, blk)

    def const(*shape):
        return pl.BlockSpec(shape, lambda b, c: (0,) * len(shape))

    return pl.pallas_call(
        _mixer_kernel,
        grid=(bsz, nstep + 1),
        in_specs=[
            pl.BlockSpec((step_rows, gdn_cols), pmap(0)),
            pl.BlockSpec((step_rows, gdn_cols), rmap(0)),
            pl.BlockSpec((step_rows, LANES), pmap(0)),
            pl.BlockSpec((step_rows, LANES), rmap(0)),
            const(8, LANES),
            const(1, GDN_DV),
            pl.BlockSpec((step_rows, GLA_QK), pmap(q_blk)),
            pl.BlockSpec((step_rows, GLA_QK), pmap(q_blk + 1)),
            pl.BlockSpec((step_rows, GLA_V), pmap(v_blk)),
            pl.BlockSpec((step_rows, GLA_V), pmap(v_blk + 1)),
            pl.BlockSpec((step_rows, GLA_QK), rmap(q_blk)),
            pl.BlockSpec((step_rows, GLA_QK), rmap(q_blk + 1)),
            pl.BlockSpec((step_rows, GLA_V), rmap(v_blk)),
            pl.BlockSpec((step_rows, GLA_V), rmap(v_blk + 1)),
            const(LANES, GLA_QK),
            const(1, GLA_QK),
            const(1, GLA_DV),
            const(CHUNK // 8, 8 * GLA_DK, CHUNK),
        ],
        out_specs=pl.BlockSpec((step_rows, GDN_V + GLA_V), rmap(0)),
        out_shape=jax.ShapeDtypeStruct((rows, GDN_V + GLA_V), BF16),
        scratch_shapes=[pltpu.VMEM((GDN_HEADS, GDN_DK, GDN_DV), F32),
                        pltpu.VMEM((GLA_HEADS, GLA_DV, GLA_DK), F32)],
        compiler_params=pltpu.CompilerParams(
            dimension_semantics=("arbitrary", "arbitrary"), vmem_limit_bytes=VMEM_LIMIT),
        name="mixer",
    )(proj_pre, proj_main, small_pre, small_main, gparams, gdn_nw,
      proj_pre, proj_pre, proj_pre, proj_pre, proj_main, proj_main, proj_main, proj_main,
      w2_pad, gate_b, gla_nw, sel)


def _outproj_kernel(x_ref, m_ref, w_ref, o_ref):
    o_ref[...] = x_ref[...] + jnp.dot(m_ref[...], w_ref[...], preferred_element_type=F32)


def _outproj(x2d, mixed, wo, tm):
    rows, d = x2d.shape
    width = mixed.shape[1]
    return pl.pallas_call(
        _outproj_kernel,
        grid=(rows // tm,),
        in_specs=[
            pl.BlockSpec((tm, d), lambda i: (i, 0)),
            pl.BlockSpec((tm, width), lambda i: (i, 0)),
            pl.BlockSpec((width, d), lambda i: (0, 0), pipeline_mode=pl.Buffered(1)),
        ],
        out_specs=pl.BlockSpec((tm, d), lambda i: (i, 0)),
        out_shape=jax.ShapeDtypeStruct((rows, d), F32),
        compiler_params=pltpu.CompilerParams(
            dimension_semantics=("arbitrary",), vmem_limit_bytes=VMEM_LIMIT),
        name="outproj",
    )(x2d, mixed, wo)


def _ffn_kernel(h_ref, nw_ref, wg_ref, wu_ref, wd_ref, fw_ref, o_ref, n_scr):
    f = pl.program_id(1)

    @pl.when(f == 0)
    def _():
        h = h_ref[...]
        ms = jnp.mean(h * h, axis=-1, keepdims=True)
        n_scr[...] = ((h * lax.rsqrt(ms + NORM_EPS)) * nw_ref[...]).astype(BF16)
        o_ref[...] = h

    n = n_scr[...]
    g = jnp.dot(n, wg_ref[...], preferred_element_type=F32)
    u = jnp.dot(n, wu_ref[...], preferred_element_type=F32)
    act = (_silu(g) * u).astype(BF16)
    o_ref[...] += jnp.dot(act, wd_ref[...], preferred_element_type=F32)

    @pl.when(f == pl.num_programs(1) - 1)
    def _():
        y = o_ref[...]
        ms = jnp.mean(y * y, axis=-1, keepdims=True)
        o_ref[...] = (y * lax.rsqrt(ms + NORM_EPS)) * fw_ref[...]


def _ffn(h2d, norm_w, w_gate, w_up, w_down, final_w, tm, tf):
    rows, d = h2d.shape
    dff = w_gate.shape[1]
    return pl.pallas_call(
        _ffn_kernel,
        grid=(rows // tm, dff // tf),
        in_specs=[
            pl.BlockSpec((tm, d), lambda i, f: (i, 0)),
            pl.BlockSpec((1, d), lambda i, f: (0, 0)),
            pl.BlockSpec((d, tf), lambda i, f: (0, f)),
            pl.BlockSpec((d, tf), lambda i, f: (0, f)),
            pl.BlockSpec((tf, d), lambda i, f: (f, 0)),
            pl.BlockSpec((1, d), lambda i, f: (0, 0)),
        ],
        out_specs=pl.BlockSpec((tm, d), lambda i, f: (i, 0)),
        out_shape=jax.ShapeDtypeStruct((rows, d), F32),
        scratch_shapes=[pltpu.VMEM((tm, d), BF16)],
        compiler_params=pltpu.CompilerParams(
            dimension_semantics=("arbitrary", "arbitrary"), vmem_limit_bytes=VMEM_LIMIT),
        name="ffn",
    )(h2d, norm_w, w_gate, w_up, w_down, final_w)


def _pick_tile(n, pref):
    t = min(n, pref)
    while n % t:
        t //= 2
    return t


def kernel(x, meta_tokens, attn_norm_w, w_in, gdn_conv_w, gdn_a_log, gdn_dt_bias, gdn_norm_w,
           gla_gate_w2, gla_gate_b, gla_norm_w, w_out, ffn_norm_w, w_gate, w_up, w_down, final_norm_w):
    bsz, seq, d = x.shape
    step_rows = MIX_STEP_CHUNKS * CHUNK
    assert seq % step_rows == 0 and attn_norm_w.shape[0] == 1
    nchunk = seq // CHUNK
    rows = bsz * seq
    x2d = x.reshape(rows, d)

    wi = w_in[0]
    off_small = 2 * GDN_QK + 2 * GDN_V
    off_gla = off_small + 2 * GDN_HEADS
    off_lr = off_gla + 2 * GLA_QK + 2 * GLA_V
    w_main = jnp.concatenate([wi[:, :off_small], wi[:, off_gla:off_lr]], axis=1).astype(BF16)
    w_small = jnp.concatenate(
        [wi[:, off_small:off_gla], wi[:, off_lr:],
         jnp.zeros((d, LANES - 2 * GDN_HEADS - GLA_GATE_RANK), F32)], axis=1).astype(BF16)
    assert w_main.shape[1] == MAIN_COLS
    nw_attn = attn_norm_w[0].reshape(1, d)
    gparams = jnp.zeros((8, LANES), F32)
    gparams = gparams.at[0, :GDN_HEADS].set(gdn_a_log[0]).at[1, :GDN_HEADS].set(gdn_dt_bias[0])
    w2_pad = jnp.zeros((LANES, GLA_QK), F32).at[2 * GDN_HEADS:2 * GDN_HEADS + GLA_GATE_RANK].set(gla_gate_w2[0])

    h_pre = jnp.concatenate([jnp.zeros((step_rows - N_META, d), x.dtype), meta_tokens.astype(x.dtype)], axis=0)

    kk = jnp.arange(8 * GLA_DK) // GLA_DK
    sel = (kk[None, :, None] + 8 * jnp.arange(CHUNK // 8)[:, None, None]
           == jnp.arange(CHUNK)[None, None, :]).astype(BF16)

    conv_w = gdn_conv_w[0]
    proj_pre, small_pre, carry = _inproj(h_pre, nw_attn, w_main, w_small, conv_w,
                                         jnp.zeros((8, conv_w.shape[1]), F32), step_rows, 512, 1)
    tm_in = _pick_tile(seq, 512)
    proj_main, small_main, _ = _inproj(x2d, nw_attn, w_main, w_small, conv_w, carry, tm_in, 256, seq // tm_in)

    nstep = seq // step_rows
    mixed = _mixer(proj_pre, proj_main, small_pre, small_main, gparams, gdn_norm_w[0].reshape(1, GDN_DV),
                   w2_pad.astype(BF16), gla_gate_b[0].reshape(1, GLA_QK), gla_norm_w[0].reshape(1, GLA_DV),
                   sel, bsz, nstep)

    h2d = _outproj(x2d, mixed, w_out[0].astype(BF16), _pick_tile(rows, 512))
    out = _ffn(h2d, ffn_norm_w[0].reshape(1, d), w_gate[0].astype(BF16), w_up[0].astype(BF16),
               w_down[0].astype(BF16), final_norm_w.reshape(1, d), _pick_tile(rows, 1024), 512)
    return out.reshape(bsz, seq, d)
```

```python
import functools

import jax
import jax.numpy as jnp
from jax import lax
from jax.experimental import pallas as pl
from jax.experimental.pallas import tpu as pltpu

F32 = jnp.float32
BF16 = jnp.bfloat16

N_META = 16
CONV_K = 4
GDN_HEADS = 8
GDN_DK = 128
GDN_DV = 128
GLA_HEADS = 4
GLA_DK = 128
GLA_DV = 256
GLA_GATE_RANK = 16
GLA_GATE_NORMALIZER = 16.0
GDN_QK = GDN_HEADS * GDN_DK
GDN_V = GDN_HEADS * GDN_DV
GLA_QK = GLA_HEADS * GLA_DK
GLA_V = GLA_HEADS * GLA_DV
NORM_EPS = 1e-6

CHUNK = 64
SUB = 16
MIX_STEP_CHUNKS = 4
LANES = 128
MAIN_COLS = 2 * GDN_QK + 2 * GDN_V + 2 * GLA_QK + 2 * GLA_V
VMEM_LIMIT = 56 * 1024 * 1024
LOG2E = 1.4426950408889634


def _dot(a, b):
    return jnp.dot(a.astype(BF16), b.astype(BF16), preferred_element_type=F32)


def _dot_nt(a, b):
    return lax.dot_general(a.astype(BF16), b.astype(BF16), (((1,), (1,)), ((), ())),
                           preferred_element_type=F32)


def _dot_hilo(a_exact, b):
    bh = b.astype(BF16)
    bl = (b - bh.astype(F32)).astype(BF16)
    ab = a_exact.astype(BF16)
    return (jnp.dot(ab, bh, preferred_element_type=F32)
            + jnp.dot(ab, bl, preferred_element_type=F32))


def _bmm(a, b):
    return jnp.einsum("gij,gjk->gik", a.astype(BF16), b.astype(BF16), preferred_element_type=F32)


def _bmm_nt(a, b):
    return jnp.einsum("gik,gjk->gij", a.astype(BF16), b.astype(BF16), preferred_element_type=F32)


def _sigmoid(x):
    return 1.0 / (1.0 + jnp.exp2(x * (-LOG2E)))


def _silu(x):
    return x * _sigmoid(x)


def _softplus(x):
    return jnp.maximum(x, 0.0) + jnp.log(1.0 + jnp.exp(-jnp.abs(x)))


def _iota2(shape, dim):
    return lax.broadcasted_iota(jnp.int32, shape, dim)


def _l2norm_heads(y, scale):
    outs = []
    for h in range(y.shape[1] // LANES):
        yh = y[:, h * LANES:(h + 1) * LANES]
        outs.append(yh * (lax.rsqrt(jnp.sum(yh * yh, axis=-1, keepdims=True) + NORM_EPS) * scale))
    return jnp.concatenate(outs, axis=1)


def _inproj_kernel(x_ref, nw_ref, w_ref, ws_ref, cw_ref, carry_ref, o_ref, os_ref, tail_ref, tail_scr,
                   acc_scr, *, tn, steps_per_seq):
    i = pl.program_id(0)
    tm = x_ref.shape[0]

    @pl.when(i % steps_per_seq == 0)
    def _():
        tail_scr[...] = carry_ref[...]

    x = x_ref[...]
    ms = jnp.mean(x * x, axis=-1, keepdims=True)
    nb = ((x * lax.rsqrt(ms + NORM_EPS)) * nw_ref[...]).astype(BF16)
    os_ref[...] = jnp.dot(nb, ws_ref[...], preferred_element_type=F32)
    nqkv = 2 * GDN_QK + GDN_V
    gla_q0 = nqkv + GDN_V
    gla_r0 = gla_q0 + 2 * GLA_QK + GLA_V
    ntile = w_ref.shape[1] // tn

    def matmul_tile(j):
        cs = slice(j * tn, (j + 1) * tn)
        acc_scr[8:8 + tm, :] = jnp.dot(nb, w_ref[:, cs], preferred_element_type=F32)
        if j * tn < nqkv:
            acc_scr[0:8, :] = tail_scr[:, cs]
            tail_scr[:, cs] = acc_scr[tm:tm + 8, :]

    def epilogue_tile(j):
        c0 = j * tn
        cs = slice(c0, c0 + tn)
        acc = acc_scr[8:8 + tm, :]
        if c0 < nqkv:
            y = acc * cw_ref[CONV_K - 1:CONV_K, cs]
            for t in range(1, CONV_K):
                y = y + acc_scr[8 - t:8 - t + tm, :] * cw_ref[CONV_K - 1 - t:CONV_K - t, cs]
            y = _silu(y)
            if c0 < GDN_QK:
                y = _l2norm_heads(y, GDN_DK ** -0.5)
            elif c0 < 2 * GDN_QK:
                y = _l2norm_heads(y, 1.0)
        elif c0 < gla_q0 or c0 >= gla_r0:
            y = _silu(acc)
        elif c0 < gla_q0 + GLA_QK:
            y = acc * (GLA_DK ** -0.5)
        else:
            y = acc
        o_ref[:, cs] = y.astype(o_ref.dtype)

    for j in range(ntile):
        matmul_tile(j)
        epilogue_tile(j)
    tail_ref[...] = tail_scr[...]


def _inproj(x2d, norm_w, w_main, w_small, conv_w, carry, tm, tn, steps_per_seq):
    rows, d = x2d.shape
    ncol = w_main.shape[1]
    nqkv = 2 * GDN_QK + GDN_V
    assert GDN_QK % tn == 0 and GDN_V % tn == 0 and GLA_QK % tn == 0 and tn % LANES == 0
    resident = pl.Buffered(1)
    return pl.pallas_call(
        functools.partial(_inproj_kernel, tn=tn, steps_per_seq=steps_per_seq),
        grid=(rows // tm,),
        in_specs=[
            pl.BlockSpec((tm, d), lambda i: (i, 0)),
            pl.BlockSpec((1, d), lambda i: (0, 0), pipeline_mode=resident),
            pl.BlockSpec((d, ncol), lambda i: (0, 0), pipeline_mode=resident),
            pl.BlockSpec((d, LANES), lambda i: (0, 0), pipeline_mode=resident),
            pl.BlockSpec((CONV_K, nqkv), lambda i: (0, 0), pipeline_mode=resident),
            pl.BlockSpec((8, nqkv), lambda i: (0, 0), pipeline_mode=resident),
        ],
        out_specs=[
            pl.BlockSpec((tm, ncol), lambda i: (i, 0)),
            pl.BlockSpec((tm, LANES), lambda i: (i, 0)),
            pl.BlockSpec((8, nqkv), lambda i: (0, 0)),
        ],
        out_shape=[
            jax.ShapeDtypeStruct((rows, ncol), BF16),
            jax.ShapeDtypeStruct((rows, LANES), F32),
            jax.ShapeDtypeStruct((8, nqkv), F32),
        ],
        scratch_shapes=[pltpu.VMEM((8, nqkv), F32), pltpu.VMEM((8 + tm, tn), F32)],
        compiler_params=pltpu.CompilerParams(
            dimension_semantics=("arbitrary",), vmem_limit_bytes=VMEM_LIMIT),
        name="inproj",
    )(x2d, norm_w, w_main, w_small, conv_w, carry)


def _pair_matmul(x, p):
    pb = p.astype(BF16)
    lane = _iota2((CHUNK, 2 * CHUNK), 1)
    zero = jnp.zeros_like(pb)
    blockdiag = jnp.concatenate([jnp.where(lane < CHUNK, pb, zero), jnp.where(lane >= CHUNK, pb, zero)], axis=1)
    return _bmm(x, blockdiag)


def _unit_lower_inverse(a, row, col):
    same16 = (row >> 4) == (col >> 4)
    same32 = (row >> 5) == (col >> 5)
    eye = jnp.where(row == col, 1.0, 0.0).astype(F32)
    a0 = jnp.where(same16, a, 0.0)
    x = eye - a0
    p = _pair_matmul(a0, a0)
    yield
    x = x + _pair_matmul(x, p)
    p = _pair_matmul(p, p)
    yield
    x = x + _pair_matmul(x, p)
    p = _pair_matmul(p, p)
    yield
    x = x + _pair_matmul(x, p)
    yield
    l1 = jnp.where(jnp.logical_and(same32, jnp.logical_not(same16)), a, 0.0)
    t = _pair_matmul(x, l1)
    yield
    x = x - _pair_matmul(t, x)
    yield
    l2 = jnp.where(same32, 0.0, a)
    t = _pair_matmul(x, l2)
    yield
    x = x - _pair_matmul(t, x)
    return x


def _gdn_kernel(pre_ref, main_ref, spre_ref, smain_ref, gp_ref, nw_ref, o_ref, s_scr):
    c = pl.program_id(1)
    is_pre = c == 0
    rows = MIX_STEP_CHUNKS * CHUNK

    @pl.when(is_pre)
    def _():
        s_scr[...] = jnp.zeros_like(s_scr)

    proj = jnp.where(is_pre, pre_ref[...], main_ref[...])
    small = jnp.where(is_pre, spre_ref[...], smain_ref[...])
    nqkv = 2 * GDN_QK + GDN_V

    lane = _iota2((rows, LANES), 1)
    gp = gp_ref[...]
    g_all = -jnp.exp(gp[0:1, :]) * _softplus(small + gp[1:2, :])
    g_all = jnp.where(lane < GDN_HEADS, g_all, 0.0)
    rrow = _iota2((rows, rows), 0)
    rcol = _iota2((rows, rows), 1)
    tril_blk = jnp.where(jnp.logical_and(rrow >= rcol, (rrow >> 6) == (rcol >> 6)), 1.0, 0.0).astype(F32)
    gc_all = _dot_hilo(tril_blk, g_all) * LOG2E
    beta_all = _sigmoid(small)

    pairs = [(j, h) for j in range(MIX_STEP_CHUNKS) for h in range(GDN_HEADS)]

    def rs(j):
        return slice(j * CHUNK, (j + 1) * CHUNK)

    def gather(arr, col0, width):
        return jnp.stack([arr[rs(j), col0 + h * width:col0 + (h + 1) * width] for j, h in pairs])

    q = gather(proj, 0, GDN_DK).astype(F32)
    k = gather(proj, GDN_QK, GDN_DK).astype(F32)
    v = gather(proj, 2 * GDN_QK, GDN_DV).astype(F32)
    beta = gather(beta_all, GDN_HEADS, 1)
    gcol = gather(gc_all, 0, 1)
    glast = jnp.stack([gc_all[(j + 1) * CHUNK - 1:(j + 1) * CHUNK, h:h + 1] for j, h in pairs])
    kb = k * beta

    ng = len(pairs)
    npair = ng // 2
    lane_p = _iota2((CHUNK, 2 * CHUNK), 1)
    row = _iota2((CHUNK, 2 * CHUNK), 0)
    col = lane_p & (CHUNK - 1)
    first = lane_p < CHUNK

    def interleave(even, odd):
        return jnp.stack([even, odd], axis=1).reshape((ng,) + even.shape[1:])

    def pad_even(t):
        return jnp.concatenate([t, jnp.zeros_like(t)], axis=0)

    def pad_odd(t):
        return jnp.concatenate([jnp.zeros_like(t), t], axis=0)

    gc_t = [jnp.concatenate([gc_all[rs(j), :], gc_all[rs(j), :]], axis=0).T for j in range(MIX_STEP_CHUNKS)]
    gcol_p = jnp.stack([jnp.where(first, gcol[2 * p], gcol[2 * p + 1]) for p in range(npair)])
    grow_p = jnp.stack([jnp.where(first[0:1], gc_t[pairs[2 * p][0]][pairs[2 * p][1]:pairs[2 * p][1] + 1, :],
                                  gc_t[pairs[2 * p + 1][0]][pairs[2 * p + 1][1]:pairs[2 * p + 1][1] + 1, :])
                        for p in range(npair)])
    decay = jnp.exp2(jnp.where(row >= col, gcol_p - grow_p, -jnp.inf))
    zk = jnp.zeros((CHUNK, GDN_DK), F32)
    lhs = jnp.stack([jnp.concatenate([jnp.concatenate([q[2 * p], kb[2 * p]], axis=0),
                                      jnp.concatenate([q[2 * p + 1], kb[2 * p + 1]], axis=0)], axis=1)
                     for p in range(npair)])
    k_diag = jnp.stack([jnp.concatenate([jnp.concatenate([k[2 * p], zk], axis=1),
                                         jnp.concatenate([zk, k[2 * p + 1]], axis=1)], axis=0)
                        for p in range(npair)])
    qk_kk = _bmm_nt(lhs, k_diag)
    yield
    qk = qk_kk[:, :CHUNK] * decay
    a_low = jnp.where(row > col, qk_kk[:, CHUNK:] * decay, 0.0)
    tinv = yield from _unit_lower_inverse(a_low, row, col)
    egc = jnp.exp2(gcol)
    rhs = jnp.concatenate([v * beta, kb * egc], axis=2)
    uw = interleave(_bmm(tinv, jnp.stack([pad_even(rhs[2 * p]) for p in range(npair)])),
                    _bmm(tinv, jnp.stack([pad_odd(rhs[2 * p + 1]) for p in range(npair)])))
    yield
    kd = k * jnp.exp2(glast - gcol)
    kd_t = jnp.stack([kd[g].T for g in range(ng)])
    kd_uw = _bmm(kd_t, uw)
    yield
    qk_uw = interleave(_bmm(qk, jnp.stack([pad_even(uw[2 * p]) for p in range(npair)])),
                       _bmm(qk, jnp.stack([pad_odd(uw[2 * p + 1]) for p in range(npair)])))
    q_eff = q * egc - qk_uw[:, :, GDN_DV:]
    eg_last = jnp.exp2(glast)
    yield

    nw = nw_ref[...]
    s = s_scr[...]
    for j in range(MIX_STEP_CHUNKS):
        gs = slice(j * GDN_HEADS, (j + 1) * GDN_HEADS)
        sb = s.astype(BF16)
        o = _bmm(q_eff[gs], sb) + qk_uw[gs, :, :GDN_DV]
        s = s * eg_last[gs] - _bmm(kd_uw[gs, :, GDN_DV:], sb) + kd_uw[gs, :, :GDN_DV]
        on = o * lax.rsqrt(jnp.mean(o * o, axis=-1, keepdims=True) + NORM_EPS) * nw
        for h in range(GDN_HEADS):
            zg = proj[rs(j), nqkv + h * GDN_DV:nqkv + (h + 1) * GDN_DV].astype(F32)
            o_ref[rs(j), h * GDN_DV:(h + 1) * GDN_DV] = (on[h] * zg).astype(o_ref.dtype)
        yield
    s_scr[...] = s


def _gla_level(q, k, b, m):
    g = q.shape[0]
    ref = jnp.concatenate(
        [jnp.broadcast_to(b[:, t + m - 1:t + m, :], (g, 2 * m, GLA_DK)) for t in range(0, CHUNK, 2 * m)], axis=1)
    p = _bmm_nt(q * jnp.exp2(jnp.minimum(b - ref, 0.0)), k * jnp.exp2(jnp.minimum(ref - b, 0.0)))
    row = _iota2((CHUNK, CHUNK), 0)
    col = _iota2((CHUNK, CHUNK), 1)
    blk = 2 * m
    shift = blk.bit_length() - 1
    keep = jnp.logical_and((row >> shift) == (col >> shift),
                           jnp.logical_and((row & (blk - 1)) >= m, (col & (blk - 1)) < m))
    return jnp.where(keep, p, 0.0)


def _gla_kernel(qp_ref, kp_ref, vp_ref, rp_ref, sp_ref,
                qm_ref, km_ref, vm_ref, rm_ref, sm_ref,
                w2_ref, gb_ref, nw_ref, sel_ref, o_ref, s_scr):
    c = pl.program_id(1)
    is_pre = c == 0
    rows = MIX_STEP_CHUNKS * CHUNK

    @pl.when(is_pre)
    def _():
        s_scr[...] = jnp.zeros_like(s_scr)

    q_all = jnp.where(is_pre, qp_ref[...], qm_ref[...]).astype(F32)
    k_all = jnp.where(is_pre, kp_ref[...], km_ref[...]).astype(F32)
    v_all = jnp.where(is_pre, vp_ref[...], vm_ref[...]).astype(F32)
    r_all = jnp.where(is_pre, rp_ref[...], rm_ref[...]).astype(F32)
    small = jnp.where(is_pre, sp_ref[...], sm_ref[...])

    gate = _dot(small, w2_ref[...]) + gb_ref[...]
    log_a = -_softplus(-gate) * (1.0 / GLA_GATE_NORMALIZER)
    rrow = _iota2((rows, rows), 0)
    rcol = _iota2((rows, rows), 1)
    tril_blk = jnp.where(jnp.logical_and(rrow >= rcol, (rrow >> 6) == (rcol >> 6)), 1.0, 0.0).astype(F32)
    b_all = _dot_hilo(tril_blk, log_a) * LOG2E

    pairs = [(j, h) for j in range(MIX_STEP_CHUNKS) for h in range(GLA_HEADS)]
    ng = len(pairs)

    def rs(j):
        return slice(j * CHUNK, (j + 1) * CHUNK)

    def gather(arr, width):
        return jnp.stack([arr[rs(j), h * width:(h + 1) * width] for j, h in pairs])

    q = gather(q_all, GLA_DK)
    k = gather(k_all, GLA_DK)
    b = gather(b_all, GLA_DK)
    v = gather(v_all, GLA_DV)

    yield
    scores = _gla_level(q, k, b, 32)
    yield
    scores = scores + _gla_level(q, k, b, 16)
    yield
    scores = scores + _gla_level(q, k, b, 8)
    yield

    diag = []
    for blk in range(CHUNK // 8):
        bs = b[:, blk * 8:(blk + 1) * 8, :]
        qs = q[:, blk * 8:(blk + 1) * 8, :]
        ks = k[:, blk * 8:(blk + 1) * 8, :]
        pj = [qs * (ks[:, j:j + 1, :] * jnp.exp2(jnp.minimum(bs - bs[:, j:j + 1, :], 0.0))) for j in range(8)]
        p = jnp.concatenate(pj, axis=2).astype(BF16).reshape(ng * 8, 8 * GLA_DK)
        d = jnp.dot(p, sel_ref[blk], preferred_element_type=F32)
        diag.append(d.reshape(ng, 8, CHUNK))
        yield
    row = _iota2((CHUNK, CHUNK), 0)
    col = _iota2((CHUNK, CHUNK), 1)
    scores = scores + jnp.where(row >= col, jnp.concatenate(diag, axis=1), 0.0)

    blast = b[:, CHUNK - 1:CHUNK, :]
    qe = q * jnp.exp2(b)
    kdec = k * jnp.exp2(blast - b)
    v_t = jnp.stack([v[g].T for g in range(ng)])
    kv = _bmm(v_t, kdec)
    o_intra = _bmm(scores, v)
    eb_last = jnp.exp2(blast)
    yield

    nw = nw_ref[...]
    st = s_scr[...]
    for j in range(MIX_STEP_CHUNKS):
        gs = slice(j * GLA_HEADS, (j + 1) * GLA_HEADS)
        o = _bmm_nt(qe[gs], st) + o_intra[gs]
        st = st * eb_last[gs] + kv[gs]
        on = o * lax.rsqrt(jnp.mean(o * o, axis=-1, keepdims=True) + NORM_EPS) * nw
        for h in range(GLA_HEADS):
            rg = r_all[rs(j), h * GLA_DV:(h + 1) * GLA_DV]
            o_ref[rs(j), h * GLA_DV:(h + 1) * GLA_DV] = (on[h] * rg).astype(o_ref.dtype)
        yield
    s_scr[...] = st


def _mixer_kernel(gpre_ref, gmain_ref, spre_ref, smain_ref, gp_ref, gnw_ref,
                  qp_ref, kp_ref, vp_ref, rp_ref, qm_ref, km_ref, vm_ref, rm_ref,
                  w2_ref, gb_ref, lnw_ref, sel_ref, o_ref, gdn_s, gla_s):
    streams = [
        _gdn_kernel(gpre_ref, gmain_ref, spre_ref, smain_ref, gp_ref, gnw_ref, o_ref.at[:, :GDN_V], gdn_s),
        _gla_kernel(qp_ref, kp_ref, vp_ref, rp_ref, spre_ref, qm_ref, km_ref, vm_ref, rm_ref, smain_ref,
                    w2_ref, gb_ref, lnw_ref, sel_ref, o_ref.at[:, GDN_V:], gla_s),
    ]
    while streams:
        for stream in list(streams):
            if next(stream, StopIteration) is StopIteration:
                streams.remove(stream)


def _mixer(proj_pre, proj_main, small_pre, small_main, gparams, gdn_nw, w2_pad, gate_b, gla_nw, sel,
           bsz, nstep):
    rows = proj_main.shape[0]
    step_rows = MIX_STEP_CHUNKS * CHUNK
    gdn_cols = 2 * GDN_QK + 2 * GDN_V
    q_blk = gdn_cols // GLA_QK
    v_blk = (gdn_cols + 2 * GLA_QK) // GLA_V

    def rmap(blk):
        return lambda b, c: (b * nstep + jnp.maximum(c - 1, 0), blk)

    def pmap(blk):
        return lambda b, c: (0, blk)

    def const(*shape):
        return pl.BlockSpec(shape, lambda b, c: (0,) * len(shape))

    return pl.pallas_call(
        _mixer_kernel,
        grid=(bsz, nstep + 1),
        in_specs=[
            pl.BlockSpec((step_rows, gdn_cols), pmap(0)),
            pl.BlockSpec((step_rows, gdn_cols), rmap(0)),
            pl.BlockSpec((step_rows, LANES), pmap(0)),
            pl.BlockSpec((step_rows, LANES), rmap(0)),
            const(8, LANES),
            const(1, GDN_DV),
            pl.BlockSpec((step_rows, GLA_QK), pmap(q_blk)),
            pl.BlockSpec((step_rows, GLA_QK), pmap(q_blk + 1)),
            pl.BlockSpec((step_rows, GLA_V), pmap(v_blk)),
            pl.BlockSpec((step_rows, GLA_V), pmap(v_blk + 1)),
            pl.BlockSpec((step_rows, GLA_QK), rmap(q_blk)),
            pl.BlockSpec((step_rows, GLA_QK), rmap(q_blk + 1)),
            pl.BlockSpec((step_rows, GLA_V), rmap(v_blk)),
            pl.BlockSpec((step_rows, GLA_V), rmap(v_blk + 1)),
            const(LANES, GLA_QK),
            const(1, GLA_QK),
            const(1, GLA_DV),
            const(CHUNK // 8, 8 * GLA_DK, CHUNK),
        ],
        out_specs=pl.BlockSpec((step_rows, GDN_V + GLA_V), rmap(0)),
        out_shape=jax.ShapeDtypeStruct((rows, GDN_V + GLA_V), BF16),
        scratch_shapes=[pltpu.VMEM((GDN_HEADS, GDN_DK, GDN_DV), F32),
                        pltpu.VMEM((GLA_HEADS, GLA_DV, GLA_DK), F32)],
        compiler_params=pltpu.CompilerParams(
            dimension_semantics=("arbitrary", "arbitrary"), vmem_limit_bytes=VMEM_LIMIT),
        name="mixer",
    )(proj_pre, proj_main, small_pre, small_main, gparams, gdn_nw,
      proj_pre, proj_pre, proj_pre, proj_pre, proj_main, proj_main, proj_main, proj_main,
      w2_pad, gate_b, gla_nw, sel)


def _outproj_kernel(x_ref, m_ref, w_ref, o_ref):
    o_ref[...] = x_ref[...] + jnp.dot(m_ref[...], w_ref[...], preferred_element_type=F32)


def _outproj(x2d, mixed, wo, tm):
    rows, d = x2d.shape
    width = mixed.shape[1]
    return pl.pallas_call(
        _outproj_kernel,
        grid=(rows // tm,),
        in_specs=[
            pl.BlockSpec((tm, d), lambda i: (i, 0)),
            pl.BlockSpec((tm, width), lambda i: (i, 0)),
            pl.BlockSpec((width, d), lambda i: (0, 0), pipeline_mode=pl.Buffered(1)),
        ],
        out_specs=pl.BlockSpec((tm, d), lambda i: (i, 0)),
        out_shape=jax.ShapeDtypeStruct((rows, d), F32),
        compiler_params=pltpu.CompilerParams(
            dimension_semantics=("arbitrary",), vmem_limit_bytes=VMEM_LIMIT),
        name="outproj",
    )(x2d, mixed, wo)


def _ffn_kernel(h_ref, nw_ref, wg_ref, wu_ref, wd_ref, fw_ref, o_ref, n_scr):
    f = pl.program_id(1)

    @pl.when(f == 0)
    def _():
        h = h_ref[...]
        ms = jnp.mean(h * h, axis=-1, keepdims=True)
        n_scr[...] = ((h * lax.rsqrt(ms + NORM_EPS)) * nw_ref[...]).astype(BF16)
        o_ref[...] = h

    n = n_scr[...]
    g = jnp.dot(n, wg_ref[...], preferred_element_type=F32)
    u = jnp.dot(n, wu_ref[...], preferred_element_type=F32)
    act = (_silu(g) * u).astype(BF16)
    o_ref[...] += jnp.dot(act, wd_ref[...], preferred_element_type=F32)

    @pl.when(f == pl.num_programs(1) - 1)
    def _():
        y = o_ref[...]
        ms = jnp.mean(y * y, axis=-1, keepdims=True)
        o_ref[...] = (y * lax.rsqrt(ms + NORM_EPS)) * fw_ref[...]


def _ffn(h2d, norm_w, w_gate, w_up, w_down, final_w, tm, tf):
    rows, d = h2d.shape
    dff = w_gate.shape[1]
    return pl.pallas_call(
        _ffn_kernel,
        grid=(rows // tm, dff // tf),
        in_specs=[
            pl.BlockSpec((tm, d), lambda i, f: (i, 0)),
            pl.BlockSpec((1, d), lambda i, f: (0, 0)),
            pl.BlockSpec((d, tf), lambda i, f: (0, f)),
            pl.BlockSpec((d, tf), lambda i, f: (0, f)),
            pl.BlockSpec((tf, d), lambda i, f: (f, 0)),
            pl.BlockSpec((1, d), lambda i, f: (0, 0)),
        ],
        out_specs=pl.BlockSpec((tm, d), lambda i, f: (i, 0)),
        out_shape=jax.ShapeDtypeStruct((rows, d), F32),
        scratch_shapes=[pltpu.VMEM((tm, d), BF16)],
        compiler_params=pltpu.CompilerParams(
            dimension_semantics=("arbitrary", "arbitrary"), vmem_limit_bytes=VMEM_LIMIT),
        name="ffn",
    )(h2d, norm_w, w_gate, w_up, w_down, final_w)


def _pick_tile(n, pref):
    t = min(n, pref)
    while n % t:
        t //= 2
    return t


def kernel(x, meta_tokens, attn_norm_w, w_in, gdn_conv_w, gdn_a_log, gdn_dt_bias, gdn_norm_w,
           gla_gate_w2, gla_gate_b, gla_norm_w, w_out, ffn_norm_w, w_gate, w_up, w_down, final_norm_w):
    bsz, seq, d = x.shape
    step_rows = MIX_STEP_CHUNKS * CHUNK
    assert seq % step_rows == 0 and attn_norm_w.shape[0] == 1
    nchunk = seq // CHUNK
    rows = bsz * seq
    x2d = x.reshape(rows, d)

    wi = w_in[0]
    off_small = 2 * GDN_QK + 2 * GDN_V
    off_gla = off_small + 2 * GDN_HEADS
    off_lr = off_gla + 2 * GLA_QK + 2 * GLA_V
    w_main = jnp.concatenate([wi[:, :off_small], wi[:, off_gla:off_lr]], axis=1).astype(BF16)
    w_small = jnp.concatenate(
        [wi[:, off_small:off_gla], wi[:, off_lr:],
         jnp.zeros((d, LANES - 2 * GDN_HEADS - GLA_GATE_RANK), F32)], axis=1).astype(BF16)
    assert w_main.shape[1] == MAIN_COLS
    nw_attn = attn_norm_w[0].reshape(1, d)
    gparams = jnp.zeros((8, LANES), F32)
    gparams = gparams.at[0, :GDN_HEADS].set(gdn_a_log[0]).at[1, :GDN_HEADS].set(gdn_dt_bias[0])
    w2_pad = jnp.zeros((LANES, GLA_QK), F32).at[2 * GDN_HEADS:2 * GDN_HEADS + GLA_GATE_RANK].set(gla_gate_w2[0])

    h_pre = jnp.concatenate([jnp.zeros((step_rows - N_META, d), x.dtype), meta_tokens.astype(x.dtype)], axis=0)

    kk = jnp.arange(8 * GLA_DK) // GLA_DK
    sel = (kk[None, :, None] + 8 * jnp.arange(CHUNK // 8)[:, None, None]
           == jnp.arange(CHUNK)[None, None, :]).astype(BF16)

    conv_w = gdn_conv_w[0]
    proj_pre, small_pre, carry = _inproj(h_pre, nw_attn, w_main, w_small, conv_w,
                                         jnp.zeros((8, conv_w.shape[1]), F32), step_rows, 512, 1)
    tm_in = _pick_tile(seq, 512)
    proj_main, small_main, _ = _inproj(x2d, nw_attn, w_main, w_small, conv_w, carry, tm_in, 256, seq // tm_in)

    nstep = seq // step_rows
    mixed = _mixer(proj_pre, proj_main, small_pre, small_main, gparams, gdn_norm_w[0].reshape(1, GDN_DV),
                   w2_pad.astype(BF16), gla_gate_b[0].reshape(1, GLA_QK), gla_norm_w[0].reshape(1, GLA_DV),
                   sel, bsz, nstep)

    h2d = _outproj(x2d, mixed, w_out[0].astype(BF16), _pick_tile(rows, 512))
    out = _ffn(h2d, ffn_norm_w[0].reshape(1, d), w_gate[0].astype(BF16), w_up[0].astype(BF16),
               w_down[0].astype(BF16), final_norm_w.reshape(1, d), _pick_tile(rows, 1024), 512)
    return out.reshape(bsz, seq, d)
```

```python
import functools

import jax
import jax.numpy as jnp
from jax import lax
from jax.experimental import pallas as pl
from jax.experimental.pallas import tpu as pltpu

F32 = jnp.float32
BF16 = jnp.bfloat16

N_META = 16
CONV_K = 4
GDN_HEADS = 8
GDN_DK = 128
GDN_DV = 128
GLA_HEADS = 4
GLA_DK = 128
GLA_DV = 256
GLA_GATE_RANK = 16
GLA_GATE_NORMALIZER = 16.0
GDN_QK = GDN_HEADS * GDN_DK
GDN_V = GDN_HEADS * GDN_DV
GLA_QK = GLA_HEADS * GLA_DK
GLA_V = GLA_HEADS * GLA_DV
NORM_EPS = 1e-6

CHUNK = 64
SUB = 16
MIX_STEP_CHUNKS = 4
LANES = 128
MAIN_COLS = 2 * GDN_QK + 2 * GDN_V + 2 * GLA_QK + 2 * GLA_V
VMEM_LIMIT = 56 * 1024 * 1024
LOG2E = 1.4426950408889634


def _dot(a, b):
    return jnp.dot(a.astype(BF16), b.astype(BF16), preferred_element_type=F32)


def _dot_nt(a, b):
    return lax.dot_general(a.astype(BF16), b.astype(BF16), (((1,), (1,)), ((), ())),
                           preferred_element_type=F32)


def _dot_hilo(a_exact, b):
    bh = b.astype(BF16)
    bl = (b - bh.astype(F32)).astype(BF16)
    ab = a_exact.astype(BF16)
    return (jnp.dot(ab, bh, preferred_element_type=F32)
            + jnp.dot(ab, bl, preferred_element_type=F32))


def _bmm(a, b):
    return jnp.einsum("gij,gjk->gik", a.astype(BF16), b.astype(BF16), preferred_element_type=F32)


def _bmm_nt(a, b):
    return jnp.einsum("gik,gjk->gij", a.astype(BF16), b.astype(BF16), preferred_element_type=F32)


def _sigmoid(x):
    return 1.0 / (1.0 + jnp.exp2(x * (-LOG2E)))


def _silu(x):
    return x * _sigmoid(x)


def _softplus(x):
    return jnp.maximum(x, 0.0) + jnp.log(1.0 + jnp.exp(-jnp.abs(x)))


def _iota2(shape, dim):
    return lax.broadcasted_iota(jnp.int32, shape, dim)


def _l2norm_heads(y, scale):
    outs = []
    for h in range(y.shape[1] // LANES):
        yh = y[:, h * LANES:(h + 1) * LANES]
        outs.append(yh * (lax.rsqrt(jnp.sum(yh * yh, axis=-1, keepdims=True) + NORM_EPS) * scale))
    return jnp.concatenate(outs, axis=1)


def _inproj_kernel(x_ref, nw_ref, wa_ref, wb_ref, ws_ref, cw_ref, carry_ref, o_ref, os_ref, tail_ref, tail_scr,
                   acc_scr, *, tn, steps_per_seq):
    i = pl.program_id(0)
    tm = x_ref.shape[0]

    @pl.when(i % steps_per_seq == 0)
    def _():
        tail_scr[...] = carry_ref[...]

    x = x_ref[...]
    ms = jnp.mean(x * x, axis=-1, keepdims=True)
    nb = ((x * lax.rsqrt(ms + NORM_EPS)) * nw_ref[...]).astype(BF16)
    os_ref[...] = jnp.dot(nb, ws_ref[...], preferred_element_type=F32)
    nqkv = 2 * GDN_QK + GDN_V
    gla_q0 = nqkv + GDN_V
    gla_r0 = gla_q0 + 2 * GLA_QK + GLA_V
    na = wa_ref.shape[1]
    ntile = (na + wb_ref.shape[1]) // tn

    def matmul_tile(j):
        cs = slice(j * tn, (j + 1) * tn)
        w_tile = wa_ref[:, cs] if j * tn < na else wb_ref[:, j * tn - na:(j + 1) * tn - na]
        acc_scr[8:8 + tm, :] = jnp.dot(nb, w_tile, preferred_element_type=F32)
        if j * tn < nqkv:
            acc_scr[0:8, :] = tail_scr[:, cs]
            tail_scr[:, cs] = acc_scr[tm:tm + 8, :]

    def epilogue_tile(j):
        c0 = j * tn
        cs = slice(c0, c0 + tn)
        acc = acc_scr[8:8 + tm, :]
        if c0 < nqkv:
            y = acc * cw_ref[CONV_K - 1:CONV_K, cs]
            for t in range(1, CONV_K):
                y = y + acc_scr[8 - t:8 - t + tm, :] * cw_ref[CONV_K - 1 - t:CONV_K - t, cs]
            y = _silu(y)
            if c0 < GDN_QK:
                y = _l2norm_heads(y, GDN_DK ** -0.5)
            elif c0 < 2 * GDN_QK:
                y = _l2norm_heads(y, 1.0)
        elif c0 < gla_q0 or c0 >= gla_r0:
            y = _silu(acc)
        elif c0 < gla_q0 + GLA_QK:
            y = acc * (GLA_DK ** -0.5)
        else:
            y = acc
        o_ref[:, cs] = y.astype(o_ref.dtype)

    for j in range(ntile):
        matmul_tile(j)
        epilogue_tile(j)
    tail_ref[...] = tail_scr[...]


def _inproj(x2d, norm_w, w_all, w_gla, w_small, conv_w, carry, tm, tn, steps_per_seq):
    rows, d = x2d.shape
    gdn_cols = 2 * GDN_QK + 2 * GDN_V
    ncol = gdn_cols + w_gla.shape[1]
    nqkv = 2 * GDN_QK + GDN_V
    assert GDN_QK % tn == 0 and GDN_V % tn == 0 and GLA_QK % tn == 0 and tn % LANES == 0
    resident = pl.Buffered(1)
    return pl.pallas_call(
        functools.partial(_inproj_kernel, tn=tn, steps_per_seq=steps_per_seq),
        grid=(rows // tm,),
        in_specs=[
            pl.BlockSpec((tm, d), lambda i: (i, 0)),
            pl.BlockSpec((1, d), lambda i: (0, 0), pipeline_mode=resident),
            pl.BlockSpec((d, gdn_cols), lambda i: (0, 0), pipeline_mode=resident),
            pl.BlockSpec((d, ncol - gdn_cols), lambda i: (0, 0), pipeline_mode=resident),
            pl.BlockSpec((d, LANES), lambda i: (0, 0), pipeline_mode=resident),
            pl.BlockSpec((CONV_K, nqkv), lambda i: (0, 0), pipeline_mode=resident),
            pl.BlockSpec((8, nqkv), lambda i: (0, 0), pipeline_mode=resident),
        ],
        out_specs=[
            pl.BlockSpec((tm, ncol), lambda i: (i, 0)),
            pl.BlockSpec((tm, LANES), lambda i: (i, 0)),
            pl.BlockSpec((8, nqkv), lambda i: (0, 0)),
        ],
        out_shape=[
            jax.ShapeDtypeStruct((rows, ncol), BF16),
            jax.ShapeDtypeStruct((rows, LANES), F32),
            jax.ShapeDtypeStruct((8, nqkv), F32),
        ],
        scratch_shapes=[pltpu.VMEM((8, nqkv), F32), pltpu.VMEM((8 + tm, tn), F32)],
        compiler_params=pltpu.CompilerParams(
            dimension_semantics=("arbitrary",), vmem_limit_bytes=VMEM_LIMIT),
        name="inproj",
    )(x2d, norm_w, w_all, w_gla, w_small, conv_w, carry)


def _pair_matmul(x, p):
    pb = p.astype(BF16)
    lane = _iota2((CHUNK, 2 * CHUNK), 1)
    zero = jnp.zeros_like(pb)
    blockdiag = jnp.concatenate([jnp.where(lane < CHUNK, pb, zero), jnp.where(lane >= CHUNK, pb, zero)], axis=1)
    return _bmm(x, blockdiag)


def _unit_lower_inverse(a, row, col):
    same16 = (row >> 4) == (col >> 4)
    same32 = (row >> 5) == (col >> 5)
    eye = jnp.where(row == col, 1.0, 0.0).astype(F32)
    a0 = jnp.where(same16, a, 0.0)
    x = eye - a0
    p = _pair_matmul(a0, a0)
    yield
    x = x + _pair_matmul(x, p)
    p = _pair_matmul(p, p)
    yield
    x = x + _pair_matmul(x, p)
    p = _pair_matmul(p, p)
    yield
    x = x + _pair_matmul(x, p)
    yield
    l1 = jnp.where(jnp.logical_and(same32, jnp.logical_not(same16)), a, 0.0)
    t = _pair_matmul(x, l1)
    yield
    x = x - _pair_matmul(t, x)
    yield
    l2 = jnp.where(same32, 0.0, a)
    t = _pair_matmul(x, l2)
    yield
    x = x - _pair_matmul(t, x)
    return x


def _gdn_kernel(pre_ref, main_ref, spre_ref, smain_ref, gp_ref, nw_ref, o_ref, s_scr):
    c = pl.program_id(1)
    is_pre = c == 0
    rows = MIX_STEP_CHUNKS * CHUNK

    @pl.when(is_pre)
    def _():
        s_scr[...] = jnp.zeros_like(s_scr)

    proj = jnp.where(is_pre, pre_ref[...], main_ref[...])
    small = jnp.where(is_pre, spre_ref[...], smain_ref[...])
    nqkv = 2 * GDN_QK + GDN_V

    lane = _iota2((rows, LANES), 1)
    gp = gp_ref[...]
    g_all = -jnp.exp(gp[0:1, :]) * _softplus(small + gp[1:2, :])
    g_all = jnp.where(lane < GDN_HEADS, g_all, 0.0)
    rrow = _iota2((rows, rows), 0)
    rcol = _iota2((rows, rows), 1)
    tril_blk = jnp.where(jnp.logical_and(rrow >= rcol, (rrow >> 6) == (rcol >> 6)), 1.0, 0.0).astype(F32)
    gc_all = _dot_hilo(tril_blk, g_all) * LOG2E
    beta_all = _sigmoid(small)

    pairs = [(j, h) for j in range(MIX_STEP_CHUNKS) for h in range(GDN_HEADS)]

    def rs(j):
        return slice(j * CHUNK, (j + 1) * CHUNK)

    def gather(arr, col0, width):
        return jnp.stack([arr[rs(j), col0 + h * width:col0 + (h + 1) * width] for j, h in pairs])

    q = gather(proj, 0, GDN_DK).astype(F32)
    k = gather(proj, GDN_QK, GDN_DK).astype(F32)
    v = gather(proj, 2 * GDN_QK, GDN_DV).astype(F32)
    beta = gather(beta_all, GDN_HEADS, 1)
    gcol = gather(gc_all, 0, 1)
    glast = jnp.stack([gc_all[(j + 1) * CHUNK - 1:(j + 1) * CHUNK, h:h + 1] for j, h in pairs])
    kb = k * beta

    ng = len(pairs)
    npair = ng // 2
    lane_p = _iota2((CHUNK, 2 * CHUNK), 1)
    row = _iota2((CHUNK, 2 * CHUNK), 0)
    col = lane_p & (CHUNK - 1)
    first = lane_p < CHUNK

    def interleave(even, odd):
        return jnp.stack([even, odd], axis=1).reshape((ng,) + even.shape[1:])

    def pad_even(t):
        return jnp.concatenate([t, jnp.zeros_like(t)], axis=0)

    def pad_odd(t):
        return jnp.concatenate([jnp.zeros_like(t), t], axis=0)

    gc_t = [jnp.concatenate([gc_all[rs(j), :], gc_all[rs(j), :]], axis=0).T for j in range(MIX_STEP_CHUNKS)]
    gcol_p = jnp.stack([jnp.where(first, gcol[2 * p], gcol[2 * p + 1]) for p in range(npair)])
    grow_p = jnp.stack([jnp.where(first[0:1], gc_t[pairs[2 * p][0]][pairs[2 * p][1]:pairs[2 * p][1] + 1, :],
                                  gc_t[pairs[2 * p + 1][0]][pairs[2 * p + 1][1]:pairs[2 * p + 1][1] + 1, :])
                        for p in range(npair)])
    decay = jnp.exp2(jnp.where(row >= col, gcol_p - grow_p, -jnp.inf))
    zk = jnp.zeros((CHUNK, GDN_DK), F32)
    lhs = jnp.stack([jnp.concatenate([jnp.concatenate([q[2 * p], kb[2 * p]], axis=0),
                                      jnp.concatenate([q[2 * p + 1], kb[2 * p + 1]], axis=0)], axis=1)
                     for p in range(npair)])
    k_diag = jnp.stack([jnp.concatenate([jnp.concatenate([k[2 * p], zk], axis=1),
                                         jnp.concatenate([zk, k[2 * p + 1]], axis=1)], axis=0)
                        for p in range(npair)])
    qk_kk = _bmm_nt(lhs, k_diag)
    yield
    qk = qk_kk[:, :CHUNK] * decay
    a_low = jnp.where(row > col, qk_kk[:, CHUNK:] * decay, 0.0)
    tinv = yield from _unit_lower_inverse(a_low, row, col)
    egc = jnp.exp2(gcol)
    rhs = jnp.concatenate([v * beta, kb * egc], axis=2)
    uw = interleave(_bmm(tinv, jnp.stack([pad_even(rhs[2 * p]) for p in range(npair)])),
                    _bmm(tinv, jnp.stack([pad_odd(rhs[2 * p + 1]) for p in range(npair)])))
    yield
    kd = k * jnp.exp2(glast - gcol)
    kd_t = jnp.stack([kd[g].T for g in range(ng)])
    kd_uw = _bmm(kd_t, uw)
    yield
    qk_uw = interleave(_bmm(qk, jnp.stack([pad_even(uw[2 * p]) for p in range(npair)])),
                       _bmm(qk, jnp.stack([pad_odd(uw[2 * p + 1]) for p in range(npair)])))
    q_eff = q * egc - qk_uw[:, :, GDN_DV:]
    eg_last = jnp.exp2(glast)
    yield

    nw = nw_ref[...]
    s = s_scr[...]
    for j in range(MIX_STEP_CHUNKS):
        gs = slice(j * GDN_HEADS, (j + 1) * GDN_HEADS)
        sb = s.astype(BF16)
        o = _bmm(q_eff[gs], sb) + qk_uw[gs, :, :GDN_DV]
        s = s * eg_last[gs] - _bmm(kd_uw[gs, :, GDN_DV:], sb) + kd_uw[gs, :, :GDN_DV]
        on = o * lax.rsqrt(jnp.mean(o * o, axis=-1, keepdims=True) + NORM_EPS) * nw
        for h in range(GDN_HEADS):
            zg = proj[rs(j), nqkv + h * GDN_DV:nqkv + (h + 1) * GDN_DV].astype(F32)
            o_ref[rs(j), h * GDN_DV:(h + 1) * GDN_DV] = (on[h] * zg).astype(o_ref.dtype)
        yield
    s_scr[...] = s


def _gla_level(q, k, b, m):
    g = q.shape[0]
    ref = jnp.concatenate(
        [jnp.broadcast_to(b[:, t + m - 1:t + m, :], (g, 2 * m, GLA_DK)) for t in range(0, CHUNK, 2 * m)], axis=1)
    p = _bmm_nt(q * jnp.exp2(jnp.minimum(b - ref, 0.0)), k * jnp.exp2(jnp.minimum(ref - b, 0.0)))
    row = _iota2((CHUNK, CHUNK), 0)
    col = _iota2((CHUNK, CHUNK), 1)
    blk = 2 * m
    shift = blk.bit_length() - 1
    keep = jnp.logical_and((row >> shift) == (col >> shift),
                           jnp.logical_and((row & (blk - 1)) >= m, (col & (blk - 1)) < m))
    return jnp.where(keep, p, 0.0)


def _gla_kernel(qp_ref, kp_ref, vp_ref, rp_ref, sp_ref,
                qm_ref, km_ref, vm_ref, rm_ref, sm_ref,
                w2_ref, gb_ref, nw_ref, sel_ref, o_ref, s_scr):
    c = pl.program_id(1)
    is_pre = c == 0
    rows = MIX_STEP_CHUNKS * CHUNK

    @pl.when(is_pre)
    def _():
        s_scr[...] = jnp.zeros_like(s_scr)

    q_all = jnp.where(is_pre, qp_ref[...], qm_ref[...]).astype(F32)
    k_all = jnp.where(is_pre, kp_ref[...], km_ref[...]).astype(F32)
    v_all = jnp.where(is_pre, vp_ref[...], vm_ref[...]).astype(F32)
    r_all = jnp.where(is_pre, rp_ref[...], rm_ref[...]).astype(F32)
    small = jnp.where(is_pre, sp_ref[...], sm_ref[...])

    gate = _dot(small, w2_ref[...]) + gb_ref[...]
    log_a = -_softplus(-gate) * (1.0 / GLA_GATE_NORMALIZER)
    rrow = _iota2((rows, rows), 0)
    rcol = _iota2((rows, rows), 1)
    tril_blk = jnp.where(jnp.logical_and(rrow >= rcol, (rrow >> 6) == (rcol >> 6)), 1.0, 0.0).astype(F32)
    b_all = _dot_hilo(tril_blk, log_a) * LOG2E

    pairs = [(j, h) for j in range(MIX_STEP_CHUNKS) for h in range(GLA_HEADS)]
    ng = len(pairs)

    def rs(j):
        return slice(j * CHUNK, (j + 1) * CHUNK)

    def gather(arr, width):
        return jnp.stack([arr[rs(j), h * width:(h + 1) * width] for j, h in pairs])

    q = gather(q_all, GLA_DK)
    k = gather(k_all, GLA_DK)
    b = gather(b_all, GLA_DK)
    v = gather(v_all, GLA_DV)

    yield
    scores = _gla_level(q, k, b, 32)
    yield
    scores = scores + _gla_level(q, k, b, 16)
    yield
    scores = scores + _gla_level(q, k, b, 8)
    yield

    diag = []
    for blk in range(CHUNK // 8):
        bs = b[:, blk * 8:(blk + 1) * 8, :]
        qs = q[:, blk * 8:(blk + 1) * 8, :]
        ks = k[:, blk * 8:(blk + 1) * 8, :]
        pj = [qs * (ks[:, j:j + 1, :] * jnp.exp2(jnp.minimum(bs - bs[:, j:j + 1, :], 0.0))) for j in range(8)]
        p = jnp.concatenate(pj, axis=2).astype(BF16).reshape(ng * 8, 8 * GLA_DK)
        d = jnp.dot(p, sel_ref[blk], preferred_element_type=F32)
        diag.append(d.reshape(ng, 8, CHUNK))
        yield
    row = _iota2((CHUNK, CHUNK), 0)
    col = _iota2((CHUNK, CHUNK), 1)
    scores = scores + jnp.where(row >= col, jnp.concatenate(diag, axis=1), 0.0)

    blast = b[:, CHUNK - 1:CHUNK, :]
    qe = q * jnp.exp2(b)
    kdec = k * jnp.exp2(blast - b)
    v_t = jnp.stack([v[g].T for g in range(ng)])
    kv = _bmm(v_t, kdec)
    o_intra = _bmm(scores, v)
    eb_last = jnp.exp2(blast)
    yield

    nw = nw_ref[...]
    st = s_scr[...]
    for j in range(MIX_STEP_CHUNKS):
        gs = slice(j * GLA_HEADS, (j + 1) * GLA_HEADS)
        o = _bmm_nt(qe[gs], st) + o_intra[gs]
        st = st * eb_last[gs] + kv[gs]
        on = o * lax.rsqrt(jnp.mean(o * o, axis=-1, keepdims=True) + NORM_EPS) * nw
        for h in range(GLA_HEADS):
            rg = r_all[rs(j), h * GLA_DV:(h + 1) * GLA_DV]
            o_ref[rs(j), h * GLA_DV:(h + 1) * GLA_DV] = (on[h] * rg).astype(o_ref.dtype)
        yield
    s_scr[...] = st


def _mixer_kernel(gpre_ref, gmain_ref, spre_ref, smain_ref, gp_ref, gnw_ref,
                  qp_ref, kp_ref, vp_ref, rp_ref, qm_ref, km_ref, vm_ref, rm_ref,
                  w2_ref, gb_ref, lnw_ref, sel_ref, o_ref, gdn_s, gla_s):
    streams = [
        _gdn_kernel(gpre_ref, gmain_ref, spre_ref, smain_ref, gp_ref, gnw_ref, o_ref.at[:, :GDN_V], gdn_s),
        _gla_kernel(qp_ref, kp_ref, vp_ref, rp_ref, spre_ref, qm_ref, km_ref, vm_ref, rm_ref, smain_ref,
                    w2_ref, gb_ref, lnw_ref, sel_ref, o_ref.at[:, GDN_V:], gla_s),
    ]
    while streams:
        for stream in list(streams):
            if next(stream, StopIteration) is StopIteration:
                streams.remove(stream)


def _mixer(proj_pre, proj_main, small_pre, small_main, gparams, gdn_nw, w2_pad, gate_b, gla_nw, sel,
           bsz, nstep):
    rows = proj_main.shape[0]
    step_rows = MIX_STEP_CHUNKS * CHUNK
    gdn_cols = 2 * GDN_QK + 2 * GDN_V
    q_blk = gdn_cols // GLA_QK
    v_blk = (gdn_cols + 2 * GLA_QK) // GLA_V

    def rmap(blk):
        return lambda b, c: (b * nstep + jnp.maximum(c - 1, 0), blk)

    def pmap(blk):
        return lambda b, c: (0, blk)

    def const(*shape):
        return pl.BlockSpec(shape, lambda b, c: (0,) * len(shape))

    return pl.pallas_call(
        _mixer_kernel,
        grid=(bsz, nstep + 1),
        in_specs=[
            pl.BlockSpec((step_rows, gdn_cols), pmap(0)),
            pl.BlockSpec((step_rows, gdn_cols), rmap(0)),
            pl.BlockSpec((step_rows, LANES), pmap(0)),
            pl.BlockSpec((step_rows, LANES), rmap(0)),
            const(8, LANES),
            const(1, GDN_DV),
            pl.BlockSpec((step_rows, GLA_QK), pmap(q_blk)),
            pl.BlockSpec((step_rows, GLA_QK), pmap(q_blk + 1)),
            pl.BlockSpec((step_rows, GLA_V), pmap(v_blk)),
            pl.BlockSpec((step_rows, GLA_V), pmap(v_blk + 1)),
            pl.BlockSpec((step_rows, GLA_QK), rmap(q_blk)),
            pl.BlockSpec((step_rows, GLA_QK), rmap(q_blk + 1)),
            pl.BlockSpec((step_rows, GLA_V), rmap(v_blk)),
            pl.BlockSpec((step_rows, GLA_V), rmap(v_blk + 1)),
            const(LANES, GLA_QK),
            const(1, GLA_QK),
            const(1, GLA_DV),
            const(CHUNK // 8, 8 * GLA_DK, CHUNK),
        ],
        out_specs=pl.BlockSpec((step_rows, GDN_V + GLA_V), rmap(0)),
        out_shape=jax.ShapeDtypeStruct((rows, GDN_V + GLA_V), BF16),
        scratch_shapes=[pltpu.VMEM((GDN_HEADS, GDN_DK, GDN_DV), F32),
                        pltpu.VMEM((GLA_HEADS, GLA_DV, GLA_DK), F32)],
        compiler_params=pltpu.CompilerParams(
            dimension_semantics=("arbitrary", "arbitrary"), vmem_limit_bytes=VMEM_LIMIT),
        name="mixer",
    )(proj_pre, proj_main, small_pre, small_main, gparams, gdn_nw,
      proj_pre, proj_pre, proj_pre, proj_pre, proj_main, proj_main, proj_main, proj_main,
      w2_pad, gate_b, gla_nw, sel)


def _outproj_kernel(x_ref, m_ref, w_ref, o_ref):
    o_ref[...] = x_ref[...] + jnp.dot(m_ref[...], w_ref[...], preferred_element_type=F32)


def _outproj(x2d, mixed, wo, tm):
    rows, d = x2d.shape
    width = mixed.shape[1]
    return pl.pallas_call(
        _outproj_kernel,
        grid=(rows // tm,),
        in_specs=[
            pl.BlockSpec((tm, d), lambda i: (i, 0)),
            pl.BlockSpec((tm, width), lambda i: (i, 0)),
            pl.BlockSpec((width, d), lambda i: (0, 0), pipeline_mode=pl.Buffered(1)),
        ],
        out_specs=pl.BlockSpec((tm, d), lambda i: (i, 0)),
        out_shape=jax.ShapeDtypeStruct((rows, d), F32),
        compiler_params=pltpu.CompilerParams(
            dimension_semantics=("arbitrary",), vmem_limit_bytes=VMEM_LIMIT),
        name="outproj",
    )(x2d, mixed, wo)


def _ffn_kernel(h_ref, nw_ref, wg_ref, wu_ref, wd_ref, fw_ref, o_ref, n_scr):
    f = pl.program_id(1)

    @pl.when(f == 0)
    def _():
        h = h_ref[...]
        ms = jnp.mean(h * h, axis=-1, keepdims=True)
        n_scr[...] = ((h * lax.rsqrt(ms + NORM_EPS)) * nw_ref[...]).astype(BF16)
        o_ref[...] = h

    n = n_scr[...]
    g = jnp.dot(n, wg_ref[...], preferred_element_type=F32)
    u = jnp.dot(n, wu_ref[...], preferred_element_type=F32)
    act = (_silu(g) * u).astype(BF16)
    o_ref[...] += jnp.dot(act, wd_ref[...], preferred_element_type=F32)

    @pl.when(f == pl.num_programs(1) - 1)
    def _():
        y = o_ref[...]
        ms = jnp.mean(y * y, axis=-1, keepdims=True)
        o_ref[...] = (y * lax.rsqrt(ms + NORM_EPS)) * fw_ref[...]


def _ffn(h2d, norm_w, w_gate, w_up, w_down, final_w, tm, tf):
    rows, d = h2d.shape
    dff = w_gate.shape[1]
    return pl.pallas_call(
        _ffn_kernel,
        grid=(rows // tm, dff // tf),
        in_specs=[
            pl.BlockSpec((tm, d), lambda i, f: (i, 0)),
            pl.BlockSpec((1, d), lambda i, f: (0, 0)),
            pl.BlockSpec((d, tf), lambda i, f: (0, f)),
            pl.BlockSpec((d, tf), lambda i, f: (0, f)),
            pl.BlockSpec((tf, d), lambda i, f: (f, 0)),
            pl.BlockSpec((1, d), lambda i, f: (0, 0)),
        ],
        out_specs=pl.BlockSpec((tm, d), lambda i, f: (i, 0)),
        out_shape=jax.ShapeDtypeStruct((rows, d), F32),
        scratch_shapes=[pltpu.VMEM((tm, d), BF16)],
        compiler_params=pltpu.CompilerParams(
            dimension_semantics=("arbitrary", "arbitrary"), vmem_limit_bytes=VMEM_LIMIT),
        name="ffn",
    )(h2d, norm_w, w_gate, w_up, w_down, final_w)


def _pick_tile(n, pref):
    t = min(n, pref)
    while n % t:
        t //= 2
    return t


def kernel(x, meta_tokens, attn_norm_w, w_in, gdn_conv_w, gdn_a_log, gdn_dt_bias, gdn_norm_w,
           gla_gate_w2, gla_gate_b, gla_norm_w, w_out, ffn_norm_w, w_gate, w_up, w_down, final_norm_w):
    bsz, seq, d = x.shape
    step_rows = MIX_STEP_CHUNKS * CHUNK
    assert seq % step_rows == 0 and attn_norm_w.shape[0] == 1
    nchunk = seq // CHUNK
    rows = bsz * seq
    x2d = x.reshape(rows, d)

    w_all = w_in[0].astype(BF16)
    off_small = 2 * GDN_QK + 2 * GDN_V
    off_gla = off_small + 2 * GDN_HEADS
    off_lr = off_gla + 2 * GLA_QK + 2 * GLA_V
    w_gla = w_all[:, off_gla:off_lr]
    w_small = jnp.concatenate(
        [w_all[:, off_small:off_gla], w_all[:, off_lr:],
         jnp.zeros((d, LANES - 2 * GDN_HEADS - GLA_GATE_RANK), BF16)], axis=1)
    assert off_small + w_gla.shape[1] == MAIN_COLS
    nw_attn = attn_norm_w[0].reshape(1, d)
    gparams = jnp.zeros((8, LANES), F32)
    gparams = gparams.at[0, :GDN_HEADS].set(gdn_a_log[0]).at[1, :GDN_HEADS].set(gdn_dt_bias[0])
    w2_pad = jnp.zeros((LANES, GLA_QK), F32).at[2 * GDN_HEADS:2 * GDN_HEADS + GLA_GATE_RANK].set(gla_gate_w2[0])

    h_pre = jnp.concatenate([jnp.zeros((step_rows - N_META, d), x.dtype), meta_tokens.astype(x.dtype)], axis=0)

    kk = jnp.arange(8 * GLA_DK) // GLA_DK
    sel = (kk[None, :, None] + 8 * jnp.arange(CHUNK // 8)[:, None, None]
           == jnp.arange(CHUNK)[None, None, :]).astype(BF16)

    conv_w = gdn_conv_w[0]
    proj_pre, small_pre, carry = _inproj(h_pre, nw_attn, w_all, w_gla, w_small, conv_w,
                                         jnp.zeros((8, conv_w.shape[1]), F32), step_rows, 256, 1)
    tm_in = _pick_tile(seq, 512)
    proj_main, small_main, _ = _inproj(x2d, nw_attn, w_all, w_gla, w_small, conv_w, carry, tm_in, 256,
                                       seq // tm_in)

    nstep = seq // step_rows
    mixed = _mixer(proj_pre, proj_main, small_pre, small_main, gparams, gdn_norm_w[0].reshape(1, GDN_DV),
                   w2_pad.astype(BF16), gla_gate_b[0].reshape(1, GLA_QK), gla_norm_w[0].reshape(1, GLA_DV),
                   sel, bsz, nstep)

    h2d = _outproj(x2d, mixed, w_out[0].astype(BF16), _pick_tile(rows, 512))
    out = _ffn(h2d, ffn_norm_w[0].reshape(1, d), w_gate[0].astype(BF16), w_up[0].astype(BF16),
               w_down[0].astype(BF16), final_norm_w.reshape(1, d), _pick_tile(rows, 1024), 512)
    return out.reshape(bsz, seq, d)
```

```python
import functools

import jax
import jax.numpy as jnp
from jax import lax
from jax.experimental import pallas as pl
from jax.experimental.pallas import tpu as pltpu

F32 = jnp.float32
BF16 = jnp.bfloat16

N_META = 16
CONV_K = 4
GDN_HEADS = 8
GDN_DK = 128
GDN_DV = 128
GLA_HEADS = 4
GLA_DK = 128
GLA_DV = 256
GLA_GATE_RANK = 16
GLA_GATE_NORMALIZER = 16.0
GDN_QK = GDN_HEADS * GDN_DK
GDN_V = GDN_HEADS * GDN_DV
GLA_QK = GLA_HEADS * GLA_DK
GLA_V = GLA_HEADS * GLA_DV
NORM_EPS = 1e-6

CHUNK = 64
CHUNK_SHIFT = CHUNK.bit_length() - 1
MIX_STEP_CHUNKS = 4
LANES = 128
SUBLANES = 8
MAIN_COLS = 2 * GDN_QK + 2 * GDN_V + 2 * GLA_QK + 2 * GLA_V
VMEM_LIMIT = 56 * 1024 * 1024
LOG2E = 1.4426950408889634


def _dot(a, b):
    return jnp.dot(a.astype(BF16), b.astype(BF16), preferred_element_type=F32)


def _dot_hilo(a_exact, b):
    bh = b.astype(BF16)
    bl = (b - bh.astype(F32)).astype(BF16)
    ab = a_exact.astype(BF16)
    return (jnp.dot(ab, bh, preferred_element_type=F32)
            + jnp.dot(ab, bl, preferred_element_type=F32))


def _bmm(a, b):
    return jnp.einsum("gij,gjk->gik", a.astype(BF16), b.astype(BF16), preferred_element_type=F32)


def _bmm_nt(a, b):
    return jnp.einsum("gik,gjk->gij", a.astype(BF16), b.astype(BF16), preferred_element_type=F32)


def _sigmoid(x):
    return 1.0 / (1.0 + jnp.exp2(x * (-LOG2E)))


def _silu(x):
    return x * _sigmoid(x)


def _softplus(x):
    return jnp.maximum(x, 0.0) + jnp.log(1.0 + jnp.exp(-jnp.abs(x)))


def _iota2(shape, dim):
    return lax.broadcasted_iota(jnp.int32, shape, dim)


def _l2norm_heads(y, scale):
    outs = []
    for h in range(y.shape[1] // LANES):
        yh = y[:, h * LANES:(h + 1) * LANES]
        outs.append(yh * (lax.rsqrt(jnp.sum(yh * yh, axis=-1, keepdims=True) + NORM_EPS) * scale))
    return jnp.concatenate(outs, axis=1)


def _inproj_kernel(x_ref, nw_ref, wa_ref, wb_ref, ws_ref, cw_ref, carry_ref, o_ref, os_ref, tail_ref, tail_scr,
                   acc_scr, *, tn, steps_per_seq):
    i = pl.program_id(0)
    tm = x_ref.shape[0]

    @pl.when(i % steps_per_seq == 0)
    def _():
        tail_scr[...] = carry_ref[...]

    x = x_ref[...]
    ms = jnp.mean(x * x, axis=-1, keepdims=True)
    nb = ((x * lax.rsqrt(ms + NORM_EPS)) * nw_ref[...]).astype(BF16)
    os_ref[...] = jnp.dot(nb, ws_ref[...], preferred_element_type=F32)
    nqkv = 2 * GDN_QK + GDN_V
    gla_q0 = nqkv + GDN_V
    gla_r0 = gla_q0 + 2 * GLA_QK + GLA_V
    na = wa_ref.shape[1]
    ntile = (na + wb_ref.shape[1]) // tn

    hist = SUBLANES

    def matmul_tile(j):
        cs = slice(j * tn, (j + 1) * tn)
        w_tile = wa_ref[:, cs] if j * tn < na else wb_ref[:, j * tn - na:(j + 1) * tn - na]
        acc_scr[hist:hist + tm, :] = jnp.dot(nb, w_tile, preferred_element_type=F32)
        if j * tn < nqkv:
            acc_scr[0:hist, :] = tail_scr[:, cs]
            tail_scr[:, cs] = acc_scr[tm:tm + hist, :]

    def epilogue_tile(j):
        c0 = j * tn
        cs = slice(c0, c0 + tn)
        acc = acc_scr[hist:hist + tm, :]
        if c0 < nqkv:
            y = acc * cw_ref[CONV_K - 1:CONV_K, cs]
            for t in range(1, CONV_K):
                y = y + acc_scr[hist - t:hist - t + tm, :] * cw_ref[CONV_K - 1 - t:CONV_K - t, cs]
            y = _silu(y)
            if c0 < GDN_QK:
                y = _l2norm_heads(y, GDN_DK ** -0.5)
            elif c0 < 2 * GDN_QK:
                y = _l2norm_heads(y, 1.0)
        elif c0 < gla_q0 or c0 >= gla_r0:
            y = _silu(acc)
        elif c0 < gla_q0 + GLA_QK:
            y = acc * (GLA_DK ** -0.5)
        else:
            y = acc
        o_ref[:, cs] = y.astype(o_ref.dtype)

    for j in range(ntile):
        matmul_tile(j)
        epilogue_tile(j)
    tail_ref[...] = tail_scr[...]


def _inproj(x2d, norm_w, w_all, w_gla, w_small, conv_w, carry, tm, tn, steps_per_seq):
    rows, d = x2d.shape
    gdn_cols = 2 * GDN_QK + 2 * GDN_V
    ncol = gdn_cols + w_gla.shape[1]
    nqkv = 2 * GDN_QK + GDN_V
    assert GDN_QK % tn == 0 and GDN_V % tn == 0 and GLA_QK % tn == 0 and tn % LANES == 0
    resident = pl.Buffered(1)
    return pl.pallas_call(
        functools.partial(_inproj_kernel, tn=tn, steps_per_seq=steps_per_seq),
        grid=(rows // tm,),
        in_specs=[
            pl.BlockSpec((tm, d), lambda i: (i, 0)),
            pl.BlockSpec((1, d), lambda i: (0, 0), pipeline_mode=resident),
            pl.BlockSpec((d, gdn_cols), lambda i: (0, 0), pipeline_mode=resident),
            pl.BlockSpec((d, ncol - gdn_cols), lambda i: (0, 0), pipeline_mode=resident),
            pl.BlockSpec((d, LANES), lambda i: (0, 0), pipeline_mode=resident),
            pl.BlockSpec((CONV_K, nqkv), lambda i: (0, 0), pipeline_mode=resident),
            pl.BlockSpec((SUBLANES, nqkv), lambda i: (0, 0), pipeline_mode=resident),
        ],
        out_specs=[
            pl.BlockSpec((tm, ncol), lambda i: (i, 0)),
            pl.BlockSpec((tm, LANES), lambda i: (i, 0)),
            pl.BlockSpec((SUBLANES, nqkv), lambda i: (0, 0)),
        ],
        out_shape=[
            jax.ShapeDtypeStruct((rows, ncol), BF16),
            jax.ShapeDtypeStruct((rows, LANES), F32),
            jax.ShapeDtypeStruct((SUBLANES, nqkv), F32),
        ],
        scratch_shapes=[pltpu.VMEM((SUBLANES, nqkv), F32), pltpu.VMEM((SUBLANES + tm, tn), F32)],
        compiler_params=pltpu.CompilerParams(
            dimension_semantics=("arbitrary",), vmem_limit_bytes=VMEM_LIMIT),
        name="inproj",
    )(x2d, norm_w, w_all, w_gla, w_small, conv_w, carry)


def _pair_matmul(x, p):
    pb = p.astype(BF16)
    lane = _iota2((CHUNK, 2 * CHUNK), 1)
    zero = jnp.zeros_like(pb)
    blockdiag = jnp.concatenate([jnp.where(lane < CHUNK, pb, zero), jnp.where(lane >= CHUNK, pb, zero)], axis=1)
    return _bmm(x, blockdiag)


def _unit_lower_inverse(a, row, col):
    same16 = (row >> 4) == (col >> 4)
    same32 = (row >> 5) == (col >> 5)
    eye = jnp.where(row == col, 1.0, 0.0).astype(F32)
    a0 = jnp.where(same16, a, 0.0)
    x = eye - a0
    p = _pair_matmul(a0, a0)
    yield
    x = x + _pair_matmul(x, p)
    p = _pair_matmul(p, p)
    yield
    x = x + _pair_matmul(x, p)
    p = _pair_matmul(p, p)
    yield
    x = x + _pair_matmul(x, p)
    yield
    l1 = jnp.where(jnp.logical_and(same32, jnp.logical_not(same16)), a, 0.0)
    t = _pair_matmul(x, l1)
    yield
    x = x - _pair_matmul(t, x)
    yield
    l2 = jnp.where(same32, 0.0, a)
    t = _pair_matmul(x, l2)
    yield
    x = x - _pair_matmul(t, x)
    return x


def _gdn_kernel(proj_ref, small_ref, gp_ref, nw_ref, s0_ref, o_ref, sout_ref, s_scr):
    rows = MIX_STEP_CHUNKS * CHUNK

    @pl.when(pl.program_id(1) == 0)
    def _():
        s_scr[...] = s0_ref[...]

    proj = proj_ref[...]
    small = small_ref[...]
    nqkv = 2 * GDN_QK + GDN_V

    lane = _iota2((rows, LANES), 1)
    gp = gp_ref[...]
    g_all = -jnp.exp(gp[0:1, :]) * _softplus(small + gp[1:2, :])
    g_all = jnp.where(lane < GDN_HEADS, g_all, 0.0)
    rrow = _iota2((rows, rows), 0)
    rcol = _iota2((rows, rows), 1)
    tril_blk = jnp.where(jnp.logical_and(rrow >= rcol, (rrow >> CHUNK_SHIFT) == (rcol >> CHUNK_SHIFT)),
                         1.0, 0.0).astype(F32)
    gc_all = _dot_hilo(tril_blk, g_all) * LOG2E
    beta_all = _sigmoid(small)

    pairs = [(j, h) for j in range(MIX_STEP_CHUNKS) for h in range(GDN_HEADS)]

    def rs(j):
        return slice(j * CHUNK, (j + 1) * CHUNK)

    def gather(arr, col0, width):
        return jnp.stack([arr[rs(j), col0 + h * width:col0 + (h + 1) * width] for j, h in pairs])

    q = gather(proj, 0, GDN_DK).astype(F32)
    k = gather(proj, GDN_QK, GDN_DK).astype(F32)
    v = gather(proj, 2 * GDN_QK, GDN_DV).astype(F32)
    beta = gather(beta_all, GDN_HEADS, 1)
    gcol = gather(gc_all, 0, 1)
    glast = jnp.stack([gc_all[(j + 1) * CHUNK - 1:(j + 1) * CHUNK, h:h + 1] for j, h in pairs])
    kb = k * beta

    ng = len(pairs)
    npair = ng // 2
    lane_p = _iota2((CHUNK, 2 * CHUNK), 1)
    row = _iota2((CHUNK, 2 * CHUNK), 0)
    col = lane_p & (CHUNK - 1)
    first = lane_p < CHUNK

    def interleave(even, odd):
        return jnp.stack([even, odd], axis=1).reshape((ng,) + even.shape[1:])

    def pad_even(t):
        return jnp.concatenate([t, jnp.zeros_like(t)], axis=0)

    def pad_odd(t):
        return jnp.concatenate([jnp.zeros_like(t), t], axis=0)

    gc_t = [jnp.concatenate([gc_all[rs(j), :], gc_all[rs(j), :]], axis=0).T for j in range(MIX_STEP_CHUNKS)]
    gcol_p = jnp.stack([jnp.where(first, gcol[2 * p], gcol[2 * p + 1]) for p in range(npair)])
    grow_p = jnp.stack([jnp.where(first[0:1], gc_t[pairs[2 * p][0]][pairs[2 * p][1]:pairs[2 * p][1] + 1, :],
                                  gc_t[pairs[2 * p + 1][0]][pairs[2 * p + 1][1]:pairs[2 * p + 1][1] + 1, :])
                        for p in range(npair)])
    decay = jnp.exp2(jnp.where(row >= col, gcol_p - grow_p, -jnp.inf))
    zk = jnp.zeros((CHUNK, GDN_DK), F32)
    lhs = jnp.stack([jnp.concatenate([jnp.concatenate([q[2 * p], kb[2 * p]], axis=0),
                                      jnp.concatenate([q[2 * p + 1], kb[2 * p + 1]], axis=0)], axis=1)
                     for p in range(npair)])
    k_diag = jnp.stack([jnp.concatenate([jnp.concatenate([k[2 * p], zk], axis=1),
                                         jnp.concatenate([zk, k[2 * p + 1]], axis=1)], axis=0)
                        for p in range(npair)])
    qk_kk = _bmm_nt(lhs, k_diag)
    yield
    qk = qk_kk[:, :CHUNK] * decay
    a_low = jnp.where(row > col, qk_kk[:, CHUNK:] * decay, 0.0)
    tinv = yield from _unit_lower_inverse(a_low, row, col)
    egc = jnp.exp2(gcol)
    rhs = jnp.concatenate([v * beta, kb * egc], axis=2)
    uw = interleave(_bmm(tinv, jnp.stack([pad_even(rhs[2 * p]) for p in range(npair)])),
                    _bmm(tinv, jnp.stack([pad_odd(rhs[2 * p + 1]) for p in range(npair)])))
    yield
    kd = k * jnp.exp2(glast - gcol)
    kd_t = jnp.stack([kd[g].T for g in range(ng)])
    kd_uw = _bmm(kd_t, uw)
    yield
    qk_uw = interleave(_bmm(qk, jnp.stack([pad_even(uw[2 * p]) for p in range(npair)])),
                       _bmm(qk, jnp.stack([pad_odd(uw[2 * p + 1]) for p in range(npair)])))
    q_eff = q * egc - qk_uw[:, :, GDN_DV:]
    eg_last = jnp.exp2(glast)
    yield

    nw = nw_ref[...]
    s = s_scr[...]
    for j in range(MIX_STEP_CHUNKS):
        gs = slice(j * GDN_HEADS, (j + 1) * GDN_HEADS)
        sb = s.astype(BF16)
        o = _bmm(q_eff[gs], sb) + qk_uw[gs, :, :GDN_DV]
        s = s * eg_last[gs] - _bmm(kd_uw[gs, :, GDN_DV:], sb) + kd_uw[gs, :, :GDN_DV]
        on = o * lax.rsqrt(jnp.mean(o * o, axis=-1, keepdims=True) + NORM_EPS) * nw
        for h in range(GDN_HEADS):
            zg = proj[rs(j), nqkv + h * GDN_DV:nqkv + (h + 1) * GDN_DV].astype(F32)
            o_ref[rs(j), h * GDN_DV:(h + 1) * GDN_DV] = (on[h] * zg).astype(o_ref.dtype)
        yield
    s_scr[...] = s
    sout_ref[...] = s


def _gla_level(q, k, b, m):
    g = q.shape[0]
    ref = jnp.concatenate(
        [jnp.broadcast_to(b[:, t + m - 1:t + m, :], (g, 2 * m, GLA_DK)) for t in range(0, CHUNK, 2 * m)], axis=1)
    p = _bmm_nt(q * jnp.exp2(jnp.minimum(b - ref, 0.0)), k * jnp.exp2(jnp.minimum(ref - b, 0.0)))
    row = _iota2((CHUNK, CHUNK), 0)
    col = _iota2((CHUNK, CHUNK), 1)
    blk = 2 * m
    shift = blk.bit_length() - 1
    keep = jnp.logical_and((row >> shift) == (col >> shift),
                           jnp.logical_and((row & (blk - 1)) >= m, (col & (blk - 1)) < m))
    return jnp.where(keep, p, 0.0)


def _gla_kernel(q_ref, k_ref, v_ref, r_ref, small_ref, w2_ref, gb_ref, nw_ref, sel_ref, s0_ref,
                o_ref, sout_ref, s_scr):
    rows = MIX_STEP_CHUNKS * CHUNK

    @pl.when(pl.program_id(1) == 0)
    def _():
        s_scr[...] = s0_ref[...]

    q_all = q_ref[...].astype(F32)
    k_all = k_ref[...].astype(F32)
    v_all = v_ref[...].astype(F32)
    r_all = r_ref[...].astype(F32)
    small = small_ref[...]

    gate = _dot(small, w2_ref[...]) + gb_ref[...]
    log_a = -_softplus(-gate) * (1.0 / GLA_GATE_NORMALIZER)
    rrow = _iota2((rows, rows), 0)
    rcol = _iota2((rows, rows), 1)
    tril_blk = jnp.where(jnp.logical_and(rrow >= rcol, (rrow >> CHUNK_SHIFT) == (rcol >> CHUNK_SHIFT)),
                         1.0, 0.0).astype(F32)
    b_all = _dot_hilo(tril_blk, log_a) * LOG2E

    pairs = [(j, h) for j in range(MIX_STEP_CHUNKS) for h in range(GLA_HEADS)]
    ng = len(pairs)

    def rs(j):
        return slice(j * CHUNK, (j + 1) * CHUNK)

    def gather(arr, width):
        return jnp.stack([arr[rs(j), h * width:(h + 1) * width] for j, h in pairs])

    q = gather(q_all, GLA_DK)
    k = gather(k_all, GLA_DK)
    b = gather(b_all, GLA_DK)
    v = gather(v_all, GLA_DV)

    yield
    scores = _gla_level(q, k, b, 32)
    yield
    scores = scores + _gla_level(q, k, b, 16)
    yield
    scores = scores + _gla_level(q, k, b, SUBLANES)
    yield

    diag = []
    nb8 = SUBLANES
    for blk in range(CHUNK // nb8):
        bs = b[:, blk * nb8:(blk + 1) * nb8, :]
        qs = q[:, blk * nb8:(blk + 1) * nb8, :]
        ks = k[:, blk * nb8:(blk + 1) * nb8, :]
        pj = [qs * (ks[:, j:j + 1, :] * jnp.exp2(jnp.minimum(bs - bs[:, j:j + 1, :], 0.0))) for j in range(nb8)]
        p = jnp.concatenate(pj, axis=2).astype(BF16).reshape(ng * nb8, nb8 * GLA_DK)
        d = jnp.dot(p, sel_ref[blk], preferred_element_type=F32)
        diag.append(d.reshape(ng, nb8, CHUNK))
        yield
    row = _iota2((CHUNK, CHUNK), 0)
    col = _iota2((CHUNK, CHUNK), 1)
    scores = scores + jnp.where(row >= col, jnp.concatenate(diag, axis=1), 0.0)

    blast = b[:, CHUNK - 1:CHUNK, :]
    qe = q * jnp.exp2(b)
    kdec = k * jnp.exp2(blast - b)
    v_t = jnp.stack([v[g].T for g in range(ng)])
    kv = _bmm(v_t, kdec)
    o_intra = _bmm(scores, v)
    eb_last = jnp.exp2(blast)
    yield

    nw = nw_ref[...]
    st = s_scr[...]
    for j in range(MIX_STEP_CHUNKS):
        gs = slice(j * GLA_HEADS, (j + 1) * GLA_HEADS)
        o = _bmm_nt(qe[gs], st) + o_intra[gs]
        st = st * eb_last[gs] + kv[gs]
        on = o * lax.rsqrt(jnp.mean(o * o, axis=-1, keepdims=True) + NORM_EPS) * nw
        for h in range(GLA_HEADS):
            rg = r_all[rs(j), h * GLA_DV:(h + 1) * GLA_DV]
            o_ref[rs(j), h * GLA_DV:(h + 1) * GLA_DV] = (on[h] * rg).astype(o_ref.dtype)
        yield
    s_scr[...] = st
    sout_ref[...] = st


def _mixer_kernel(gproj_ref, small_ref, gp_ref, gnw_ref, q_ref, k_ref, v_ref, r_ref,
                  w2_ref, gb_ref, lnw_ref, sel_ref, gs0_ref, ls0_ref,
                  o_ref, gsout_ref, lsout_ref, gdn_s, gla_s):
    streams = [
        _gdn_kernel(gproj_ref, small_ref, gp_ref, gnw_ref, gs0_ref, o_ref.at[:, :GDN_V], gsout_ref, gdn_s),
        _gla_kernel(q_ref, k_ref, v_ref, r_ref, small_ref, w2_ref, gb_ref, lnw_ref, sel_ref, ls0_ref,
                    o_ref.at[:, GDN_V:], lsout_ref, gla_s),
    ]
    while streams:
        for stream in list(streams):
            if next(stream, StopIteration) is StopIteration:
                streams.remove(stream)


def _mixer(proj, small, gparams, gdn_nw, w2_pad, gate_b, gla_nw, sel, gdn_s0, gla_s0, bsz, nstep):
    rows = proj.shape[0]
    step_rows = MIX_STEP_CHUNKS * CHUNK
    gdn_cols = 2 * GDN_QK + 2 * GDN_V
    q_blk = gdn_cols // GLA_QK
    v_blk = (gdn_cols + 2 * GLA_QK) // GLA_V

    def rmap(blk):
        return lambda b, c: (b * nstep + c, blk)

    def const(*shape):
        return pl.BlockSpec(shape, lambda b, c: (0,) * len(shape))

    return pl.pallas_call(
        _mixer_kernel,
        grid=(bsz, nstep),
        in_specs=[
            pl.BlockSpec((step_rows, gdn_cols), rmap(0)),
            pl.BlockSpec((step_rows, LANES), rmap(0)),
            const(SUBLANES, LANES),
            const(1, GDN_DV),
            pl.BlockSpec((step_rows, GLA_QK), rmap(q_blk)),
            pl.BlockSpec((step_rows, GLA_QK), rmap(q_blk + 1)),
            pl.BlockSpec((step_rows, GLA_V), rmap(v_blk)),
            pl.BlockSpec((step_rows, GLA_V), rmap(v_blk + 1)),
            const(LANES, GLA_QK),
            const(1, GLA_QK),
            const(1, GLA_DV),
            const(CHUNK // SUBLANES, SUBLANES * GLA_DK, CHUNK),
            const(GDN_HEADS, GDN_DK, GDN_DV),
            const(GLA_HEADS, GLA_DV, GLA_DK),
        ],
        out_specs=[
            pl.BlockSpec((step_rows, GDN_V + GLA_V), rmap(0)),
            const(GDN_HEADS, GDN_DK, GDN_DV),
            const(GLA_HEADS, GLA_DV, GLA_DK),
        ],
        out_shape=[
            jax.ShapeDtypeStruct((rows, GDN_V + GLA_V), BF16),
            jax.ShapeDtypeStruct((GDN_HEADS, GDN_DK, GDN_DV), F32),
            jax.ShapeDtypeStruct((GLA_HEADS, GLA_DV, GLA_DK), F32),
        ],
        scratch_shapes=[pltpu.VMEM((GDN_HEADS, GDN_DK, GDN_DV), F32),
                        pltpu.VMEM((GLA_HEADS, GLA_DV, GLA_DK), F32)],
        compiler_params=pltpu.CompilerParams(
            dimension_semantics=("arbitrary", "arbitrary"), vmem_limit_bytes=VMEM_LIMIT),
        name="mixer",
    )(proj, small, gparams, gdn_nw, proj, proj, proj, proj, w2_pad, gate_b, gla_nw, sel, gdn_s0, gla_s0)


def _outproj_kernel(x_ref, m_ref, w_ref, o_ref):
    o_ref[...] = x_ref[...] + jnp.dot(m_ref[...], w_ref[...], preferred_element_type=F32)


def _outproj(x2d, mixed, wo, tm):
    rows, d = x2d.shape
    width = mixed.shape[1]
    return pl.pallas_call(
        _outproj_kernel,
        grid=(rows // tm,),
        in_specs=[
            pl.BlockSpec((tm, d), lambda i: (i, 0)),
            pl.BlockSpec((tm, width), lambda i: (i, 0)),
            pl.BlockSpec((width, d), lambda i: (0, 0), pipeline_mode=pl.Buffered(1)),
        ],
        out_specs=pl.BlockSpec((tm, d), lambda i: (i, 0)),
        out_shape=jax.ShapeDtypeStruct((rows, d), F32),
        compiler_params=pltpu.CompilerParams(
            dimension_semantics=("arbitrary",), vmem_limit_bytes=VMEM_LIMIT),
        name="outproj",
    )(x2d, mixed, wo)


def _ffn_kernel(h_ref, nw_ref, wg_ref, wu_ref, wd_ref, fw_ref, o_ref, n_scr):
    f = pl.program_id(1)

    @pl.when(f == 0)
    def _():
        h = h_ref[...]
        ms = jnp.mean(h * h, axis=-1, keepdims=True)
        n_scr[...] = ((h * lax.rsqrt(ms + NORM_EPS)) * nw_ref[...]).astype(BF16)
        o_ref[...] = h

    n = n_scr[...]
    g = jnp.dot(n, wg_ref[...], preferred_element_type=F32)
    u = jnp.dot(n, wu_ref[...], preferred_element_type=F32)
    act = (_silu(g) * u).astype(BF16)
    o_ref[...] += jnp.dot(act, wd_ref[...], preferred_element_type=F32)

    @pl.when(f == pl.num_programs(1) - 1)
    def _():
        y = o_ref[...]
        ms = jnp.mean(y * y, axis=-1, keepdims=True)
        o_ref[...] = (y * lax.rsqrt(ms + NORM_EPS)) * fw_ref[...]


def _ffn(h2d, norm_w, w_gate, w_up, w_down, final_w, tm, tf):
    rows, d = h2d.shape
    dff = w_gate.shape[1]
    return pl.pallas_call(
        _ffn_kernel,
        grid=(rows // tm, dff // tf),
        in_specs=[
            pl.BlockSpec((tm, d), lambda i, f: (i, 0)),
            pl.BlockSpec((1, d), lambda i, f: (0, 0)),
            pl.BlockSpec((d, tf), lambda i, f: (0, f)),
            pl.BlockSpec((d, tf), lambda i, f: (0, f)),
            pl.BlockSpec((tf, d), lambda i, f: (f, 0)),
            pl.BlockSpec((1, d), lambda i, f: (0, 0)),
        ],
        out_specs=pl.BlockSpec((tm, d), lambda i, f: (i, 0)),
        out_shape=jax.ShapeDtypeStruct((rows, d), F32),
        scratch_shapes=[pltpu.VMEM((tm, d), BF16)],
        compiler_params=pltpu.CompilerParams(
            dimension_semantics=("arbitrary", "arbitrary"), vmem_limit_bytes=VMEM_LIMIT),
        name="ffn",
    )(h2d, norm_w, w_gate, w_up, w_down, final_w)


def _pick_tile(n, pref):
    t = min(n, pref)
    while n % t:
        t //= 2
    return t


def kernel(x, meta_tokens, attn_norm_w, w_in, gdn_conv_w, gdn_a_log, gdn_dt_bias, gdn_norm_w,
           gla_gate_w2, gla_gate_b, gla_norm_w, w_out, ffn_norm_w, w_gate, w_up, w_down, final_norm_w):
    bsz, seq, d = x.shape
    step_rows = MIX_STEP_CHUNKS * CHUNK
    assert seq % step_rows == 0 and attn_norm_w.shape[0] == 1
    rows = bsz * seq
    x2d = x.reshape(rows, d)

    w_all = w_in[0].astype(BF16)
    off_small = 2 * GDN_QK + 2 * GDN_V
    off_gla = off_small + 2 * GDN_HEADS
    off_lr = off_gla + 2 * GLA_QK + 2 * GLA_V
    w_gla = w_all[:, off_gla:off_lr]
    w_small = jnp.concatenate(
        [w_all[:, off_small:off_gla], w_all[:, off_lr:],
         jnp.zeros((d, LANES - 2 * GDN_HEADS - GLA_GATE_RANK), BF16)], axis=1)
    assert off_small + w_gla.shape[1] == MAIN_COLS
    nw_attn = attn_norm_w[0].reshape(1, d)
    gparams = jnp.zeros((SUBLANES, LANES), F32)
    gparams = gparams.at[0, :GDN_HEADS].set(gdn_a_log[0]).at[1, :GDN_HEADS].set(gdn_dt_bias[0])
    w2_pad = jnp.zeros((LANES, GLA_QK), F32).at[2 * GDN_HEADS:2 * GDN_HEADS + GLA_GATE_RANK].set(gla_gate_w2[0])

    h_pre = jnp.concatenate([jnp.zeros((step_rows - N_META, d), x.dtype), meta_tokens.astype(x.dtype)], axis=0)

    kk = jnp.arange(SUBLANES * GLA_DK) // GLA_DK
    sel = (kk[None, :, None] + SUBLANES * jnp.arange(CHUNK // SUBLANES)[:, None, None]
           == jnp.arange(CHUNK)[None, None, :]).astype(BF16)

    conv_w = gdn_conv_w[0]
    mixer_params = (gparams, gdn_norm_w[0].reshape(1, GDN_DV), w2_pad.astype(BF16),
                    gla_gate_b[0].reshape(1, GLA_QK), gla_norm_w[0].reshape(1, GLA_DV), sel)
    proj_pre, small_pre, carry = _inproj(h_pre, nw_attn, w_all, w_gla, w_small, conv_w,
                                         jnp.zeros((SUBLANES, conv_w.shape[1]), F32), step_rows, 256, 1)
    _, gdn_s0, gla_s0 = _mixer(proj_pre, small_pre, *mixer_params,
                               jnp.zeros((GDN_HEADS, GDN_DK, GDN_DV), F32),
                               jnp.zeros((GLA_HEADS, GLA_DV, GLA_DK), F32), 1, 1)

    tm_in = _pick_tile(seq, 512)
    proj_main, small_main, _ = _inproj(x2d, nw_attn, w_all, w_gla, w_small, conv_w, carry, tm_in, 256,
                                       seq // tm_in)
    mixed, _, _ = _mixer(proj_main, small_main, *mixer_params, gdn_s0, gla_s0, bsz, seq // step_rows)

    h2d = _outproj(x2d, mixed, w_out[0].astype(BF16), _pick_tile(rows, 1024))
    out = _ffn(h2d, ffn_norm_w[0].reshape(1, d), w_gate[0].astype(BF16), w_up[0].astype(BF16),
               w_down[0].astype(BF16), final_norm_w.reshape(1, d), _pick_tile(rows, 1024), 512)
    return out.reshape(bsz, seq, d)
```

```python
import functools

import jax
import jax.numpy as jnp
from jax import lax
from jax.experimental import pallas as pl
from jax.experimental.pallas import tpu as pltpu

F32 = jnp.float32
BF16 = jnp.bfloat16

N_META = 16
CONV_K = 4
GDN_HEADS = 8
GDN_DK = 128
GDN_DV = 128
GLA_HEADS = 4
GLA_DK = 128
GLA_DV = 256
GLA_GATE_RANK = 16
GLA_GATE_NORMALIZER = 16.0
GDN_QK = GDN_HEADS * GDN_DK
GDN_V = GDN_HEADS * GDN_DV
GLA_QK = GLA_HEADS * GLA_DK
GLA_V = GLA_HEADS * GLA_DV
NORM_EPS = 1e-6

CHUNK = 64
CHUNK_SHIFT = CHUNK.bit_length() - 1
MIX_STEP_CHUNKS = 4
LANES = 128
SUBLANES = 8
MAIN_COLS = 2 * GDN_QK + 2 * GDN_V + 2 * GLA_QK + 2 * GLA_V
VMEM_LIMIT = 56 * 1024 * 1024
LOG2E = 1.4426950408889634


def _dot(a, b):
    return jnp.dot(a.astype(BF16), b.astype(BF16), preferred_element_type=F32)


def _dot_hilo(a_exact, b):
    bh = b.astype(BF16)
    bl = (b - bh.astype(F32)).astype(BF16)
    ab = a_exact.astype(BF16)
    return (jnp.dot(ab, bh, preferred_element_type=F32)
            + jnp.dot(ab, bl, preferred_element_type=F32))


def _bmm(a, b):
    return jnp.einsum("gij,gjk->gik", a.astype(BF16), b.astype(BF16), preferred_element_type=F32)


def _bmm_nt(a, b):
    return jnp.einsum("gik,gjk->gij", a.astype(BF16), b.astype(BF16), preferred_element_type=F32)


def _sigmoid(x):
    return 1.0 / (1.0 + jnp.exp2(x * (-LOG2E)))


def _silu(x):
    return x * _sigmoid(x)


def _softplus(x):
    return jnp.maximum(x, 0.0) + jnp.log(1.0 + jnp.exp(-jnp.abs(x)))


def _iota2(shape, dim):
    return lax.broadcasted_iota(jnp.int32, shape, dim)


def _l2norm_heads(y, scale):
    outs = []
    for h in range(y.shape[1] // LANES):
        yh = y[:, h * LANES:(h + 1) * LANES]
        outs.append(yh * (lax.rsqrt(jnp.sum(yh * yh, axis=-1, keepdims=True) + NORM_EPS) * scale))
    return jnp.concatenate(outs, axis=1)


def _inproj_kernel(x_ref, nw_ref, wa_ref, wb_ref, ws_ref, cw_ref, carry_ref, o_ref, os_ref, tail_ref, tail_scr,
                   acc_scr, *, tn, steps_per_seq):
    i = pl.program_id(0)
    tm = x_ref.shape[0]

    @pl.when(i % steps_per_seq == 0)
    def _():
        tail_scr[...] = carry_ref[...]

    x = x_ref[...]
    ms = jnp.mean(x * x, axis=-1, keepdims=True)
    nb = ((x * lax.rsqrt(ms + NORM_EPS)) * nw_ref[...]).astype(BF16)
    os_ref[...] = jnp.dot(nb, ws_ref[...], preferred_element_type=F32)
    nqkv = 2 * GDN_QK + GDN_V
    gla_q0 = nqkv + GDN_V
    gla_r0 = gla_q0 + 2 * GLA_QK + GLA_V
    na = wa_ref.shape[1]
    ntile = (na + wb_ref.shape[1]) // tn

    hist = SUBLANES

    def matmul_tile(j):
        cs = slice(j * tn, (j + 1) * tn)
        w_tile = wa_ref[:, cs] if j * tn < na else wb_ref[:, j * tn - na:(j + 1) * tn - na]
        acc_scr[hist:hist + tm, :] = jnp.dot(nb, w_tile, preferred_element_type=F32)
        if j * tn < nqkv:
            acc_scr[0:hist, :] = tail_scr[:, cs]
            tail_scr[:, cs] = acc_scr[tm:tm + hist, :]

    def epilogue_tile(j):
        c0 = j * tn
        cs = slice(c0, c0 + tn)
        acc = acc_scr[hist:hist + tm, :]
        if c0 < nqkv:
            y = acc * cw_ref[CONV_K - 1:CONV_K, cs]
            for t in range(1, CONV_K):
                y = y + acc_scr[hist - t:hist - t + tm, :] * cw_ref[CONV_K - 1 - t:CONV_K - t, cs]
            y = _silu(y)
            if c0 < GDN_QK:
                y = _l2norm_heads(y, GDN_DK ** -0.5)
            elif c0 < 2 * GDN_QK:
                y = _l2norm_heads(y, 1.0)
        elif c0 < gla_q0 or c0 >= gla_r0:
            y = _silu(acc)
        elif c0 < gla_q0 + GLA_QK:
            y = acc * (GLA_DK ** -0.5)
        else:
            y = acc
        o_ref[:, cs] = y.astype(o_ref.dtype)

    for j in range(ntile):
        matmul_tile(j)
        epilogue_tile(j)
    tail_ref[...] = tail_scr[...]


def _inproj(x2d, norm_w, w_all, w_gla, w_small, conv_w, carry, tm, tn, steps_per_seq):
    rows, d = x2d.shape
    gdn_cols = 2 * GDN_QK + 2 * GDN_V
    ncol = gdn_cols + w_gla.shape[1]
    nqkv = 2 * GDN_QK + GDN_V
    assert GDN_QK % tn == 0 and GDN_V % tn == 0 and GLA_QK % tn == 0 and tn % LANES == 0
    resident = pl.Buffered(1)
    return pl.pallas_call(
        functools.partial(_inproj_kernel, tn=tn, steps_per_seq=steps_per_seq),
        grid=(rows // tm,),
        in_specs=[
            pl.BlockSpec((tm, d), lambda i: (i, 0)),
            pl.BlockSpec((1, d), lambda i: (0, 0), pipeline_mode=resident),
            pl.BlockSpec((d, gdn_cols), lambda i: (0, 0), pipeline_mode=resident),
            pl.BlockSpec((d, ncol - gdn_cols), lambda i: (0, 0), pipeline_mode=resident),
            pl.BlockSpec((d, LANES), lambda i: (0, 0), pipeline_mode=resident),
            pl.BlockSpec((CONV_K, nqkv), lambda i: (0, 0), pipeline_mode=resident),
            pl.BlockSpec((SUBLANES, nqkv), lambda i: (0, 0), pipeline_mode=resident),
        ],
        out_specs=[
            pl.BlockSpec((tm, ncol), lambda i: (i, 0)),
            pl.BlockSpec((tm, LANES), lambda i: (i, 0)),
            pl.BlockSpec((SUBLANES, nqkv), lambda i: (0, 0)),
        ],
        out_shape=[
            jax.ShapeDtypeStruct((rows, ncol), BF16),
            jax.ShapeDtypeStruct((rows, LANES), F32),
            jax.ShapeDtypeStruct((SUBLANES, nqkv), F32),
        ],
        scratch_shapes=[pltpu.VMEM((SUBLANES, nqkv), F32), pltpu.VMEM((SUBLANES + tm, tn), F32)],
        compiler_params=pltpu.CompilerParams(
            dimension_semantics=("arbitrary",), vmem_limit_bytes=VMEM_LIMIT),
        name="inproj",
    )(x2d, norm_w, w_all, w_gla, w_small, conv_w, carry)


def _pair_matmul(x, p):
    pb = p.astype(BF16)
    lane = _iota2((CHUNK, 2 * CHUNK), 1)
    zero = jnp.zeros_like(pb)
    blockdiag = jnp.concatenate([jnp.where(lane < CHUNK, pb, zero), jnp.where(lane >= CHUNK, pb, zero)], axis=1)
    return _bmm(x, blockdiag)


def _unit_lower_inverse(a, row, col):
    same16 = (row >> 4) == (col >> 4)
    same32 = (row >> 5) == (col >> 5)
    eye = jnp.where(row == col, 1.0, 0.0).astype(F32)
    a0 = jnp.where(same16, a, 0.0)
    x = eye - a0
    p = _pair_matmul(a0, a0)
    yield
    x = x + _pair_matmul(x, p)
    p = _pair_matmul(p, p)
    yield
    x = x + _pair_matmul(x, p)
    p = _pair_matmul(p, p)
    yield
    x = x + _pair_matmul(x, p)
    yield
    l1 = jnp.where(jnp.logical_and(same32, jnp.logical_not(same16)), a, 0.0)
    t = _pair_matmul(x, l1)
    yield
    x = x - _pair_matmul(t, x)
    yield
    l2 = jnp.where(same32, 0.0, a)
    t = _pair_matmul(x, l2)
    yield
    x = x - _pair_matmul(t, x)
    return x


def _gdn_kernel(proj_ref, small_ref, gp_ref, nw_ref, s0_ref, o_ref, sout_ref, s_scr):
    rows = MIX_STEP_CHUNKS * CHUNK

    @pl.when(pl.program_id(1) == 0)
    def _():
        s_scr[...] = s0_ref[...]

    proj = proj_ref[...]
    small = small_ref[...]
    nqkv = 2 * GDN_QK + GDN_V

    lane = _iota2((rows, LANES), 1)
    gp = gp_ref[...]
    g_all = -jnp.exp(gp[0:1, :]) * _softplus(small + gp[1:2, :])
    g_all = jnp.where(lane < GDN_HEADS, g_all, 0.0)
    rrow = _iota2((rows, rows), 0)
    rcol = _iota2((rows, rows), 1)
    tril_blk = jnp.where(jnp.logical_and(rrow >= rcol, (rrow >> CHUNK_SHIFT) == (rcol >> CHUNK_SHIFT)),
                         1.0, 0.0).astype(F32)
    gc_all = _dot_hilo(tril_blk, g_all) * LOG2E
    beta_all = _sigmoid(small)

    pairs = [(j, h) for j in range(MIX_STEP_CHUNKS) for h in range(GDN_HEADS)]

    def rs(j):
        return slice(j * CHUNK, (j + 1) * CHUNK)

    def gather(arr, col0, width):
        return jnp.stack([arr[rs(j), col0 + h * width:col0 + (h + 1) * width] for j, h in pairs])

    q = gather(proj, 0, GDN_DK).astype(F32)
    k = gather(proj, GDN_QK, GDN_DK).astype(F32)
    v = gather(proj, 2 * GDN_QK, GDN_DV).astype(F32)
    beta = jnp.broadcast_to(gather(beta_all, GDN_HEADS, 1), k.shape)
    gcol = jnp.broadcast_to(gather(gc_all, 0, 1), k.shape)
    glast = jnp.stack([gc_all[(j + 1) * CHUNK - 1:(j + 1) * CHUNK, h:h + 1] for j, h in pairs])
    kb = k * beta

    ng = len(pairs)
    npair = ng // 2
    lane_p = _iota2((CHUNK, 2 * CHUNK), 1)
    row = _iota2((CHUNK, 2 * CHUNK), 0)
    col = lane_p & (CHUNK - 1)
    first = lane_p < CHUNK

    def interleave(even, odd):
        return jnp.stack([even, odd], axis=1).reshape((ng,) + even.shape[1:])

    def pad_even(t):
        t = t.astype(BF16)
        return jnp.concatenate([t, jnp.zeros_like(t)], axis=0)

    def pad_odd(t):
        t = t.astype(BF16)
        return jnp.concatenate([jnp.zeros_like(t), t], axis=0)

    gc_t = [jnp.concatenate([gc_all[rs(j), :], gc_all[rs(j), :]], axis=0).T for j in range(MIX_STEP_CHUNKS)]
    gcol_p = jnp.stack([jnp.where(first, gcol[2 * p], gcol[2 * p + 1]) for p in range(npair)])
    grow_p = jnp.stack([jnp.where(first[0:1], gc_t[pairs[2 * p][0]][pairs[2 * p][1]:pairs[2 * p][1] + 1, :],
                                  gc_t[pairs[2 * p + 1][0]][pairs[2 * p + 1][1]:pairs[2 * p + 1][1] + 1, :])
                        for p in range(npair)])
    decay = jnp.exp2(jnp.where(row >= col, gcol_p - grow_p, -jnp.inf))
    zk = jnp.zeros((CHUNK, GDN_DK), BF16)
    q16, k16, kb16 = q.astype(BF16), k.astype(BF16), kb.astype(BF16)
    lhs = jnp.stack([jnp.concatenate([jnp.concatenate([q16[2 * p], kb16[2 * p]], axis=0),
                                      jnp.concatenate([q16[2 * p + 1], kb16[2 * p + 1]], axis=0)], axis=1)
                     for p in range(npair)])
    k_diag = jnp.stack([jnp.concatenate([jnp.concatenate([k16[2 * p], zk], axis=1),
                                         jnp.concatenate([zk, k16[2 * p + 1]], axis=1)], axis=0)
                        for p in range(npair)])
    qk_kk = _bmm_nt(lhs, k_diag)
    yield
    qk = qk_kk[:, :CHUNK] * decay
    a_low = jnp.where(row > col, qk_kk[:, CHUNK:] * decay, 0.0)
    tinv = yield from _unit_lower_inverse(a_low, row, col)
    egc = jnp.exp2(gcol)
    rhs = jnp.concatenate([v * beta, kb * egc], axis=2)
    uw = interleave(_bmm(tinv, jnp.stack([pad_even(rhs[2 * p]) for p in range(npair)])),
                    _bmm(tinv, jnp.stack([pad_odd(rhs[2 * p + 1]) for p in range(npair)])))
    yield
    kd = k * jnp.exp2(glast - gcol)
    kd_t = jnp.stack([kd[g].T for g in range(ng)])
    kd_uw = _bmm(kd_t, uw)
    yield
    qk_uw = interleave(_bmm(qk, jnp.stack([pad_even(uw[2 * p]) for p in range(npair)])),
                       _bmm(qk, jnp.stack([pad_odd(uw[2 * p + 1]) for p in range(npair)])))
    q_eff = q * egc - qk_uw[:, :, GDN_DV:]
    eg_last = jnp.exp2(glast)
    yield

    nw = nw_ref[...]
    s = s_scr[...]
    for j in range(MIX_STEP_CHUNKS):
        gs = slice(j * GDN_HEADS, (j + 1) * GDN_HEADS)
        sb = s.astype(BF16)
        o = _bmm(q_eff[gs], sb) + qk_uw[gs, :, :GDN_DV]
        s = s * eg_last[gs] - _bmm(kd_uw[gs, :, GDN_DV:], sb) + kd_uw[gs, :, :GDN_DV]
        on = o * lax.rsqrt(jnp.mean(o * o, axis=-1, keepdims=True) + NORM_EPS) * nw
        for h in range(GDN_HEADS):
            zg = proj[rs(j), nqkv + h * GDN_DV:nqkv + (h + 1) * GDN_DV].astype(F32)
            o_ref[rs(j), h * GDN_DV:(h + 1) * GDN_DV] = (on[h] * zg).astype(o_ref.dtype)
        yield
    s_scr[...] = s
    sout_ref[...] = s


def _gla_level(q, k, b, m):
    g = q.shape[0]
    ref = jnp.concatenate(
        [jnp.broadcast_to(b[:, t + m - 1:t + m, :], (g, 2 * m, GLA_DK)) for t in range(0, CHUNK, 2 * m)], axis=1)
    p = _bmm_nt(q * jnp.exp2(jnp.minimum(b - ref, 0.0)), k * jnp.exp2(jnp.minimum(ref - b, 0.0)))
    row = _iota2((CHUNK, CHUNK), 0)
    col = _iota2((CHUNK, CHUNK), 1)
    blk = 2 * m
    shift = blk.bit_length() - 1
    keep = jnp.logical_and((row >> shift) == (col >> shift),
                           jnp.logical_and((row & (blk - 1)) >= m, (col & (blk - 1)) < m))
    return jnp.where(keep, p, 0.0)


def _gla_kernel(q_ref, k_ref, v_ref, r_ref, small_ref, w2_ref, gb_ref, nw_ref, sel_ref, s0_ref,
                o_ref, sout_ref, s_scr):
    rows = MIX_STEP_CHUNKS * CHUNK

    @pl.when(pl.program_id(1) == 0)
    def _():
        s_scr[...] = s0_ref[...]

    q_all = q_ref[...].astype(F32)
    k_all = k_ref[...].astype(F32)
    v_all = v_ref[...].astype(F32)
    r_all = r_ref[...].astype(F32)
    small = small_ref[...]

    gate = _dot(small, w2_ref[...]) + gb_ref[...]
    log_a = -_softplus(-gate) * (1.0 / GLA_GATE_NORMALIZER)
    rrow = _iota2((rows, rows), 0)
    rcol = _iota2((rows, rows), 1)
    tril_blk = jnp.where(jnp.logical_and(rrow >= rcol, (rrow >> CHUNK_SHIFT) == (rcol >> CHUNK_SHIFT)),
                         1.0, 0.0).astype(F32)
    b_all = _dot_hilo(tril_blk, log_a) * LOG2E

    pairs = [(j, h) for j in range(MIX_STEP_CHUNKS) for h in range(GLA_HEADS)]
    ng = len(pairs)

    def rs(j):
        return slice(j * CHUNK, (j + 1) * CHUNK)

    def gather(arr, width):
        return jnp.stack([arr[rs(j), h * width:(h + 1) * width] for j, h in pairs])

    q = gather(q_all, GLA_DK)
    k = gather(k_all, GLA_DK)
    b = gather(b_all, GLA_DK)
    v = gather(v_all, GLA_DV)

    yield
    scores = _gla_level(q, k, b, 32)
    yield
    scores = scores + _gla_level(q, k, b, 16)
    yield
    scores = scores + _gla_level(q, k, b, SUBLANES)
    yield

    diag = []
    nb8 = SUBLANES
    for blk in range(CHUNK // nb8):
        bs = b[:, blk * nb8:(blk + 1) * nb8, :]
        qs = q[:, blk * nb8:(blk + 1) * nb8, :]
        ks = k[:, blk * nb8:(blk + 1) * nb8, :]
        pj = [qs * (ks[:, j:j + 1, :] * jnp.exp2(jnp.minimum(bs - bs[:, j:j + 1, :], 0.0))) for j in range(nb8)]
        p = jnp.concatenate(pj, axis=2).astype(BF16).reshape(ng * nb8, nb8 * GLA_DK)
        d = jnp.dot(p, sel_ref[blk], preferred_element_type=F32)
        diag.append(d.reshape(ng, nb8, CHUNK))
        yield
    row = _iota2((CHUNK, CHUNK), 0)
    col = _iota2((CHUNK, CHUNK), 1)
    scores = scores + jnp.where(row >= col, jnp.concatenate(diag, axis=1), 0.0)

    blast = b[:, CHUNK - 1:CHUNK, :]
    qe = q * jnp.exp2(b)
    kdec = k * jnp.exp2(blast - b)
    v_t = jnp.stack([v[g].T for g in range(ng)])
    kv = _bmm(v_t, kdec)
    o_intra = _bmm(scores, v)
    eb_last = jnp.exp2(blast)
    yield

    nw = nw_ref[...]
    st = s_scr[...]
    for j in range(MIX_STEP_CHUNKS):
        gs = slice(j * GLA_HEADS, (j + 1) * GLA_HEADS)
        o = _bmm_nt(qe[gs], st) + o_intra[gs]
        st = st * eb_last[gs] + kv[gs]
        on = o * lax.rsqrt(jnp.mean(o * o, axis=-1, keepdims=True) + NORM_EPS) * nw
        for h in range(GLA_HEADS):
            rg = r_all[rs(j), h * GLA_DV:(h + 1) * GLA_DV]
            o_ref[rs(j), h * GLA_DV:(h + 1) * GLA_DV] = (on[h] * rg).astype(o_ref.dtype)
        yield
    s_scr[...] = st
    sout_ref[...] = st


def _mixer_kernel(gproj_ref, small_ref, gp_ref, gnw_ref, q_ref, k_ref, v_ref, r_ref,
                  w2_ref, gb_ref, lnw_ref, sel_ref, gs0_ref, ls0_ref,
                  o_ref, gsout_ref, lsout_ref, gdn_s, gla_s):
    streams = [
        _gdn_kernel(gproj_ref, small_ref, gp_ref, gnw_ref, gs0_ref, o_ref.at[:, :GDN_V], gsout_ref, gdn_s),
        _gla_kernel(q_ref, k_ref, v_ref, r_ref, small_ref, w2_ref, gb_ref, lnw_ref, sel_ref, ls0_ref,
                    o_ref.at[:, GDN_V:], lsout_ref, gla_s),
    ]
    while streams:
        for stream in list(streams):
            if next(stream, StopIteration) is StopIteration:
                streams.remove(stream)


def _mixer(proj, small, gparams, gdn_nw, w2_pad, gate_b, gla_nw, sel, gdn_s0, gla_s0, bsz, nstep):
    rows = proj.shape[0]
    step_rows = MIX_STEP_CHUNKS * CHUNK
    gdn_cols = 2 * GDN_QK + 2 * GDN_V
    q_blk = gdn_cols // GLA_QK
    v_blk = (gdn_cols + 2 * GLA_QK) // GLA_V

    def rmap(blk):
        return lambda b, c: (b * nstep + c, blk)

    def const(*shape):
        return pl.BlockSpec(shape, lambda b, c: (0,) * len(shape))

    return pl.pallas_call(
        _mixer_kernel,
        grid=(bsz, nstep),
        in_specs=[
            pl.BlockSpec((step_rows, gdn_cols), rmap(0)),
            pl.BlockSpec((step_rows, LANES), rmap(0)),
            const(SUBLANES, LANES),
            const(1, GDN_DV),
            pl.BlockSpec((step_rows, GLA_QK), rmap(q_blk)),
            pl.BlockSpec((step_rows, GLA_QK), rmap(q_blk + 1)),
            pl.BlockSpec((step_rows, GLA_V), rmap(v_blk)),
            pl.BlockSpec((step_rows, GLA_V), rmap(v_blk + 1)),
            const(LANES, GLA_QK),
            const(1, GLA_QK),
            const(1, GLA_DV),
            const(CHUNK // SUBLANES, SUBLANES * GLA_DK, CHUNK),
            const(GDN_HEADS, GDN_DK, GDN_DV),
            const(GLA_HEADS, GLA_DV, GLA_DK),
        ],
        out_specs=[
            pl.BlockSpec((step_rows, GDN_V + GLA_V), rmap(0)),
            const(GDN_HEADS, GDN_DK, GDN_DV),
            const(GLA_HEADS, GLA_DV, GLA_DK),
        ],
        out_shape=[
            jax.ShapeDtypeStruct((rows, GDN_V + GLA_V), BF16),
            jax.ShapeDtypeStruct((GDN_HEADS, GDN_DK, GDN_DV), F32),
            jax.ShapeDtypeStruct((GLA_HEADS, GLA_DV, GLA_DK), F32),
        ],
        scratch_shapes=[pltpu.VMEM((GDN_HEADS, GDN_DK, GDN_DV), F32),
                        pltpu.VMEM((GLA_HEADS, GLA_DV, GLA_DK), F32)],
        compiler_params=pltpu.CompilerParams(
            dimension_semantics=("arbitrary", "arbitrary"), vmem_limit_bytes=VMEM_LIMIT),
        name="mixer",
    )(proj, small, gparams, gdn_nw, proj, proj, proj, proj, w2_pad, gate_b, gla_nw, sel, gdn_s0, gla_s0)


def _outproj_kernel(x_ref, m_ref, w_ref, o_ref):
    o_ref[...] = x_ref[...] + jnp.dot(m_ref[...], w_ref[...], preferred_element_type=F32)


def _outproj(x2d, mixed, wo, tm):
    rows, d = x2d.shape
    width = mixed.shape[1]
    return pl.pallas_call(
        _outproj_kernel,
        grid=(rows // tm,),
        in_specs=[
            pl.BlockSpec((tm, d), lambda i: (i, 0)),
            pl.BlockSpec((tm, width), lambda i: (i, 0)),
            pl.BlockSpec((width, d), lambda i: (0, 0), pipeline_mode=pl.Buffered(1)),
        ],
        out_specs=pl.BlockSpec((tm, d), lambda i: (i, 0)),
        out_shape=jax.ShapeDtypeStruct((rows, d), F32),
        compiler_params=pltpu.CompilerParams(
            dimension_semantics=("arbitrary",), vmem_limit_bytes=VMEM_LIMIT),
        name="outproj",
    )(x2d, mixed, wo)


def _ffn_kernel(h_ref, nw_ref, wg_ref, wu_ref, wd_ref, fw_ref, o_ref, n_scr):
    f = pl.program_id(1)

    @pl.when(f == 0)
    def _():
        h = h_ref[...]
        ms = jnp.mean(h * h, axis=-1, keepdims=True)
        n_scr[...] = ((h * lax.rsqrt(ms + NORM_EPS)) * nw_ref[...]).astype(BF16)
        o_ref[...] = h

    n = n_scr[...]
    g = jnp.dot(n, wg_ref[...], preferred_element_type=F32)
    u = jnp.dot(n, wu_ref[...], preferred_element_type=F32)
    act = (_silu(g) * u).astype(BF16)
    o_ref[...] += jnp.dot(act, wd_ref[...], preferred_element_type=F32)

    @pl.when(f == pl.num_programs(1) - 1)
    def _():
        y = o_ref[...]
        ms = jnp.mean(y * y, axis=-1, keepdims=True)
        o_ref[...] = (y * lax.rsqrt(ms + NORM_EPS)) * fw_ref[...]


def _ffn(h2d, norm_w, w_gate, w_up, w_down, final_w, tm, tf):
    rows, d = h2d.shape
    dff = w_gate.shape[1]
    return pl.pallas_call(
        _ffn_kernel,
        grid=(rows // tm, dff // tf),
        in_specs=[
            pl.BlockSpec((tm, d), lambda i, f: (i, 0)),
            pl.BlockSpec((1, d), lambda i, f: (0, 0)),
            pl.BlockSpec((d, tf), lambda i, f: (0, f)),
            pl.BlockSpec((d, tf), lambda i, f: (0, f)),
            pl.BlockSpec((tf, d), lambda i, f: (f, 0)),
            pl.BlockSpec((1, d), lambda i, f: (0, 0)),
        ],
        out_specs=pl.BlockSpec((tm, d), lambda i, f: (i, 0)),
        out_shape=jax.ShapeDtypeStruct((rows, d), F32),
        scratch_shapes=[pltpu.VMEM((tm, d), BF16)],
        compiler_params=pltpu.CompilerParams(
            dimension_semantics=("arbitrary", "arbitrary"), vmem_limit_bytes=VMEM_LIMIT),
        name="ffn",
    )(h2d, norm_w, w_gate, w_up, w_down, final_w)


def _pick_tile(n, pref):
    t = min(n, pref)
    while n % t:
        t //= 2
    return t


def kernel(x, meta_tokens, attn_norm_w, w_in, gdn_conv_w, gdn_a_log, gdn_dt_bias, gdn_norm_w,
           gla_gate_w2, gla_gate_b, gla_norm_w, w_out, ffn_norm_w, w_gate, w_up, w_down, final_norm_w):
    bsz, seq, d = x.shape
    step_rows = MIX_STEP_CHUNKS * CHUNK
    assert seq % step_rows == 0 and attn_norm_w.shape[0] == 1
    rows = bsz * seq
    x2d = x.reshape(rows, d)

    w_all = w_in[0].astype(BF16)
    off_small = 2 * GDN_QK + 2 * GDN_V
    off_gla = off_small + 2 * GDN_HEADS
    off_lr = off_gla + 2 * GLA_QK + 2 * GLA_V
    w_gla = w_all[:, off_gla:off_lr]
    w_small = jnp.concatenate(
        [w_all[:, off_small:off_gla], w_all[:, off_lr:],
         jnp.zeros((d, LANES - 2 * GDN_HEADS - GLA_GATE_RANK), BF16)], axis=1)
    assert off_small + w_gla.shape[1] == MAIN_COLS
    nw_attn = attn_norm_w[0].reshape(1, d)
    gparams = jnp.zeros((SUBLANES, LANES), F32)
    gparams = gparams.at[0, :GDN_HEADS].set(gdn_a_log[0]).at[1, :GDN_HEADS].set(gdn_dt_bias[0])
    w2_pad = jnp.zeros((LANES, GLA_QK), F32).at[2 * GDN_HEADS:2 * GDN_HEADS + GLA_GATE_RANK].set(gla_gate_w2[0])

    h_pre = jnp.concatenate([jnp.zeros((step_rows - N_META, d), x.dtype), meta_tokens.astype(x.dtype)], axis=0)

    kk = jnp.arange(SUBLANES * GLA_DK) // GLA_DK
    sel = (kk[None, :, None] + SUBLANES * jnp.arange(CHUNK // SUBLANES)[:, None, None]
           == jnp.arange(CHUNK)[None, None, :]).astype(BF16)

    conv_w = gdn_conv_w[0]
    mixer_params = (gparams, gdn_norm_w[0].reshape(1, GDN_DV), w2_pad.astype(BF16),
                    gla_gate_b[0].reshape(1, GLA_QK), gla_norm_w[0].reshape(1, GLA_DV), sel)
    proj_pre, small_pre, carry = _inproj(h_pre, nw_attn, w_all, w_gla, w_small, conv_w,
                                         jnp.zeros((SUBLANES, conv_w.shape[1]), F32), step_rows, 256, 1)
    _, gdn_s0, gla_s0 = _mixer(proj_pre, small_pre, *mixer_params,
                               jnp.zeros((GDN_HEADS, GDN_DK, GDN_DV), F32),
                               jnp.zeros((GLA_HEADS, GLA_DV, GLA_DK), F32), 1, 1)

    tm_in = _pick_tile(seq, 512)
    proj_main, small_main, _ = _inproj(x2d, nw_attn, w_all, w_gla, w_small, conv_w, carry, tm_in, 256,
                                       seq // tm_in)
    mixed, _, _ = _mixer(proj_main, small_main, *mixer_params, gdn_s0, gla_s0, bsz, seq // step_rows)

    h2d = _outproj(x2d, mixed, w_out[0].astype(BF16), _pick_tile(rows, 1024))
    out = _ffn(h2d, ffn_norm_w[0].reshape(1, d), w_gate[0].astype(BF16), w_up[0].astype(BF16),
               w_down[0].astype(BF16), final_norm_w.reshape(1, d), _pick_tile(rows, 1024), 512)
    return out.reshape(bsz, seq, d)
```

```python
import functools

import jax
import jax.numpy as jnp
from jax import lax
from jax.experimental import pallas as pl
from jax.experimental.pallas import tpu as pltpu

F32 = jnp.float32
BF16 = jnp.bfloat16

N_META = 16
CONV_K = 4
GDN_HEADS = 8
GDN_DK = 128
GDN_DV = 128
GLA_HEADS = 4
GLA_DK = 128
GLA_DV = 256
GLA_GATE_RANK = 16
GLA_GATE_NORMALIZER = 16.0
GDN_QK = GDN_HEADS * GDN_DK
GDN_V = GDN_HEADS * GDN_DV
GLA_QK = GLA_HEADS * GLA_DK
GLA_V = GLA_HEADS * GLA_DV
NORM_EPS = 1e-6

CHUNK = 64
CHUNK_SHIFT = CHUNK.bit_length() - 1
MIX_STEP_CHUNKS = 4
LANES = 128
SUBLANES = 8
MAIN_COLS = 2 * GDN_QK + 2 * GDN_V + 2 * GLA_QK + 2 * GLA_V
VMEM_LIMIT = 56 * 1024 * 1024
LOG2E = 1.4426950408889634


def _dot(a, b):
    return jnp.dot(a.astype(BF16), b.astype(BF16), preferred_element_type=F32)


def _dot_hilo(a_exact, b):
    bh = b.astype(BF16)
    bl = (b - bh.astype(F32)).astype(BF16)
    ab = a_exact.astype(BF16)
    return (jnp.dot(ab, bh, preferred_element_type=F32)
            + jnp.dot(ab, bl, preferred_element_type=F32))


def _bmm(a, b):
    return jnp.einsum("gij,gjk->gik", a.astype(BF16), b.astype(BF16), preferred_element_type=F32)


def _bmm_nt(a, b):
    return jnp.einsum("gik,gjk->gij", a.astype(BF16), b.astype(BF16), preferred_element_type=F32)


def _sigmoid(x):
    return 1.0 / (1.0 + jnp.exp2(x * (-LOG2E)))


def _silu(x):
    return x * _sigmoid(x)


def _softplus(x):
    return jnp.maximum(x, 0.0) + jnp.log(1.0 + jnp.exp(-jnp.abs(x)))


def _iota2(shape, dim):
    return lax.broadcasted_iota(jnp.int32, shape, dim)


def _l2norm_heads(y, scale):
    outs = []
    for h in range(y.shape[1] // LANES):
        yh = y[:, h * LANES:(h + 1) * LANES]
        outs.append(yh * (lax.rsqrt(jnp.sum(yh * yh, axis=-1, keepdims=True) + NORM_EPS) * scale))
    return jnp.concatenate(outs, axis=1)


def _inproj_kernel(x_ref, nw_ref, wa_ref, wb_ref, ws_ref, cw_ref, carry_ref, o_ref, os_ref, tail_ref, tail_scr,
                   acc_scr, *, tn, steps_per_seq):
    i = pl.program_id(0)
    tm = x_ref.shape[0]

    @pl.when(i % steps_per_seq == 0)
    def _():
        tail_scr[...] = carry_ref[...]

    x = x_ref[...]
    ms = jnp.mean(x * x, axis=-1, keepdims=True)
    nb = ((x * lax.rsqrt(ms + NORM_EPS)) * nw_ref[...]).astype(BF16)
    os_ref[...] = jnp.dot(nb, ws_ref[...], preferred_element_type=F32)
    nqkv = 2 * GDN_QK + GDN_V
    gla_q0 = nqkv + GDN_V
    gla_r0 = gla_q0 + 2 * GLA_QK + GLA_V
    na = wa_ref.shape[1]
    ntile = (na + wb_ref.shape[1]) // tn

    hist = SUBLANES

    def matmul_tile(j):
        cs = slice(j * tn, (j + 1) * tn)
        w_tile = wa_ref[:, cs] if j * tn < na else wb_ref[:, j * tn - na:(j + 1) * tn - na]
        acc_scr[hist:hist + tm, :] = jnp.dot(nb, w_tile, preferred_element_type=F32)
        if j * tn < nqkv:
            acc_scr[0:hist, :] = tail_scr[:, cs]
            tail_scr[:, cs] = acc_scr[tm:tm + hist, :]

    def epilogue_tile(j):
        c0 = j * tn
        cs = slice(c0, c0 + tn)
        acc = acc_scr[hist:hist + tm, :]
        if c0 < nqkv:
            y = acc * cw_ref[CONV_K - 1:CONV_K, cs]
            for t in range(1, CONV_K):
                y = y + acc_scr[hist - t:hist - t + tm, :] * cw_ref[CONV_K - 1 - t:CONV_K - t, cs]
            y = _silu(y)
            if c0 < GDN_QK:
                y = _l2norm_heads(y, GDN_DK ** -0.5)
            elif c0 < 2 * GDN_QK:
                y = _l2norm_heads(y, 1.0)
        elif c0 < gla_q0 or c0 >= gla_r0:
            y = _silu(acc)
        elif c0 < gla_q0 + GLA_QK:
            y = acc * (GLA_DK ** -0.5)
        else:
            y = acc
        o_ref[:, cs] = y.astype(o_ref.dtype)

    for j in range(ntile):
        matmul_tile(j)
        epilogue_tile(j)
    tail_ref[...] = tail_scr[...]


def _inproj(x2d, norm_w, w_all, w_gla, w_small, conv_w, carry, tm, tn, steps_per_seq):
    rows, d = x2d.shape
    gdn_cols = 2 * GDN_QK + 2 * GDN_V
    ncol = gdn_cols + w_gla.shape[1]
    nqkv = 2 * GDN_QK + GDN_V
    assert GDN_QK % tn == 0 and GDN_V % tn == 0 and GLA_QK % tn == 0 and tn % LANES == 0
    resident = pl.Buffered(1)
    return pl.pallas_call(
        functools.partial(_inproj_kernel, tn=tn, steps_per_seq=steps_per_seq),
        grid=(rows // tm,),
        in_specs=[
            pl.BlockSpec((tm, d), lambda i: (i, 0)),
            pl.BlockSpec((1, d), lambda i: (0, 0), pipeline_mode=resident),
            pl.BlockSpec((d, gdn_cols), lambda i: (0, 0), pipeline_mode=resident),
            pl.BlockSpec((d, ncol - gdn_cols), lambda i: (0, 0), pipeline_mode=resident),
            pl.BlockSpec((d, LANES), lambda i: (0, 0), pipeline_mode=resident),
            pl.BlockSpec((CONV_K, nqkv), lambda i: (0, 0), pipeline_mode=resident),
            pl.BlockSpec((SUBLANES, nqkv), lambda i: (0, 0), pipeline_mode=resident),
        ],
        out_specs=[
            pl.BlockSpec((tm, ncol), lambda i: (i, 0)),
            pl.BlockSpec((tm, LANES), lambda i: (i, 0)),
            pl.BlockSpec((SUBLANES, nqkv), lambda i: (0, 0)),
        ],
        out_shape=[
            jax.ShapeDtypeStruct((rows, ncol), BF16),
            jax.ShapeDtypeStruct((rows, LANES), F32),
            jax.ShapeDtypeStruct((SUBLANES, nqkv), F32),
        ],
        scratch_shapes=[pltpu.VMEM((SUBLANES, nqkv), F32), pltpu.VMEM((SUBLANES + tm, tn), F32)],
        compiler_params=pltpu.CompilerParams(
            dimension_semantics=("arbitrary",), vmem_limit_bytes=VMEM_LIMIT),
        name="inproj",
    )(x2d, norm_w, w_all, w_gla, w_small, conv_w, carry)


def _pair_matmul(x, p):
    pb = p.astype(BF16)
    lane = _iota2((CHUNK, 2 * CHUNK), 1)
    zero = jnp.zeros_like(pb)
    blockdiag = jnp.concatenate([jnp.where(lane < CHUNK, pb, zero), jnp.where(lane >= CHUNK, pb, zero)], axis=1)
    return _bmm(x, blockdiag)


def _unit_lower_inverse(a, row, col):
    same16 = (row >> 4) == (col >> 4)
    same32 = (row >> 5) == (col >> 5)
    eye = jnp.where(row == col, 1.0, 0.0).astype(F32)
    a0 = jnp.where(same16, a, 0.0)
    x = eye - a0
    p = _pair_matmul(a0, a0)
    yield
    x = x + _pair_matmul(x, p)
    p = _pair_matmul(p, p)
    yield
    x = x + _pair_matmul(x, p)
    p = _pair_matmul(p, p)
    yield
    x = x + _pair_matmul(x, p)
    yield
    l1 = jnp.where(jnp.logical_and(same32, jnp.logical_not(same16)), a, 0.0)
    t = _pair_matmul(x, l1)
    yield
    x = x - _pair_matmul(t, x)
    yield
    l2 = jnp.where(same32, 0.0, a)
    t = _pair_matmul(x, l2)
    yield
    x = x - _pair_matmul(t, x)
    return x


def _gdn_kernel(proj_ref, small_ref, gp_ref, nw_ref, s0_ref, o_ref, sout_ref, s_scr):
    rows = MIX_STEP_CHUNKS * CHUNK

    @pl.when(pl.program_id(1) == 0)
    def _():
        s_scr[...] = s0_ref[...]

    proj = proj_ref[...]
    small = small_ref[...]
    nqkv = 2 * GDN_QK + GDN_V

    lane = _iota2((rows, LANES), 1)
    gp = gp_ref[...]
    g_all = -jnp.exp(gp[0:1, :]) * _softplus(small + gp[1:2, :])
    g_all = jnp.where(lane < GDN_HEADS, g_all, 0.0)
    rrow = _iota2((rows, rows), 0)
    rcol = _iota2((rows, rows), 1)
    tril_blk = jnp.where(jnp.logical_and(rrow >= rcol, (rrow >> CHUNK_SHIFT) == (rcol >> CHUNK_SHIFT)),
                         1.0, 0.0).astype(F32)
    gc_all = _dot_hilo(tril_blk, g_all) * LOG2E
    beta_all = _sigmoid(small)

    pairs = [(j, h) for j in range(MIX_STEP_CHUNKS) for h in range(GDN_HEADS)]

    def rs(j):
        return slice(j * CHUNK, (j + 1) * CHUNK)

    def gather(arr, col0, width):
        return jnp.stack([arr[rs(j), col0 + h * width:col0 + (h + 1) * width] for j, h in pairs])

    q = gather(proj, 0, GDN_DK).astype(F32)
    k = gather(proj, GDN_QK, GDN_DK).astype(F32)
    v = gather(proj, 2 * GDN_QK, GDN_DV).astype(F32)
    beta = jnp.broadcast_to(gather(beta_all, GDN_HEADS, 1), k.shape)
    gcol = jnp.broadcast_to(gather(gc_all, 0, 1), k.shape)
    glast = jnp.stack([gc_all[(j + 1) * CHUNK - 1:(j + 1) * CHUNK, h:h + 1] for j, h in pairs])
    kb = k * beta

    ng = len(pairs)
    npair = ng // 2
    lane_p = _iota2((CHUNK, 2 * CHUNK), 1)
    row = _iota2((CHUNK, 2 * CHUNK), 0)
    col = lane_p & (CHUNK - 1)
    first = lane_p < CHUNK

    def interleave(even, odd):
        return jnp.stack([even, odd], axis=1).reshape((ng,) + even.shape[1:])

    def pad_even(t):
        t = t.astype(BF16)
        return jnp.concatenate([t, jnp.zeros_like(t)], axis=0)

    def pad_odd(t):
        t = t.astype(BF16)
        return jnp.concatenate([jnp.zeros_like(t), t], axis=0)

    gc_t = [jnp.concatenate([gc_all[rs(j), :], gc_all[rs(j), :]], axis=0).T for j in range(MIX_STEP_CHUNKS)]
    gcol_p = jnp.stack([jnp.where(first, gcol[2 * p], gcol[2 * p + 1]) for p in range(npair)])
    grow_p = jnp.stack([jnp.where(first[0:1], gc_t[pairs[2 * p][0]][pairs[2 * p][1]:pairs[2 * p][1] + 1, :],
                                  gc_t[pairs[2 * p + 1][0]][pairs[2 * p + 1][1]:pairs[2 * p + 1][1] + 1, :])
                        for p in range(npair)])
    decay = jnp.exp2(jnp.where(row >= col, gcol_p - grow_p, -jnp.inf))
    zk = jnp.zeros((CHUNK, GDN_DK), BF16)
    q16, k16, kb16 = q.astype(BF16), k.astype(BF16), kb.astype(BF16)
    lhs = jnp.stack([jnp.concatenate([jnp.concatenate([q16[2 * p], kb16[2 * p]], axis=0),
                                      jnp.concatenate([q16[2 * p + 1], kb16[2 * p + 1]], axis=0)], axis=1)
                     for p in range(npair)])
    k_diag = jnp.stack([jnp.concatenate([jnp.concatenate([k16[2 * p], zk], axis=1),
                                         jnp.concatenate([zk, k16[2 * p + 1]], axis=1)], axis=0)
                        for p in range(npair)])
    qk_kk = _bmm_nt(lhs, k_diag)
    yield
    qk = qk_kk[:, :CHUNK] * decay
    a_low = jnp.where(row > col, qk_kk[:, CHUNK:] * decay, 0.0)
    tinv = yield from _unit_lower_inverse(a_low, row, col)
    egc = jnp.exp2(gcol)
    rhs = jnp.concatenate([v * beta, kb * egc], axis=2)
    uw = interleave(_bmm(tinv, jnp.stack([pad_even(rhs[2 * p]) for p in range(npair)])),
                    _bmm(tinv, jnp.stack([pad_odd(rhs[2 * p + 1]) for p in range(npair)])))
    yield
    kd = k * jnp.exp2(glast - gcol)
    kd_t = jnp.stack([kd[g].T for g in range(ng)])
    kd_uw = _bmm(kd_t, uw)
    yield
    qk_uw = interleave(_bmm(qk, jnp.stack([pad_even(uw[2 * p]) for p in range(npair)])),
                       _bmm(qk, jnp.stack([pad_odd(uw[2 * p + 1]) for p in range(npair)])))
    q_eff = q * egc - qk_uw[:, :, GDN_DV:]
    eg_last = jnp.exp2(glast)
    yield

    nw = nw_ref[...]
    s = s_scr[...]
    for j in range(MIX_STEP_CHUNKS):
        gs = slice(j * GDN_HEADS, (j + 1) * GDN_HEADS)
        sb = s.astype(BF16)
        o = _bmm(q_eff[gs], sb) + qk_uw[gs, :, :GDN_DV]
        s = s * eg_last[gs] - _bmm(kd_uw[gs, :, GDN_DV:], sb) + kd_uw[gs, :, :GDN_DV]
        on = o * lax.rsqrt(jnp.mean(o * o, axis=-1, keepdims=True) + NORM_EPS) * nw
        for h in range(GDN_HEADS):
            zg = proj[rs(j), nqkv + h * GDN_DV:nqkv + (h + 1) * GDN_DV].astype(F32)
            o_ref[rs(j), h * GDN_DV:(h + 1) * GDN_DV] = (on[h] * zg).astype(o_ref.dtype)
        yield
    s_scr[...] = s
    sout_ref[...] = s


def _gla_level(q, k, b, m):
    g = q.shape[0]
    ref = jnp.concatenate(
        [jnp.broadcast_to(b[:, t + m - 1:t + m, :], (g, 2 * m, GLA_DK)) for t in range(0, CHUNK, 2 * m)], axis=1)
    p = _bmm_nt(q * jnp.exp2(jnp.minimum(b - ref, 0.0)), k * jnp.exp2(jnp.minimum(ref - b, 0.0)))
    row = _iota2((CHUNK, CHUNK), 0)
    col = _iota2((CHUNK, CHUNK), 1)
    blk = 2 * m
    shift = blk.bit_length() - 1
    keep = jnp.logical_and((row >> shift) == (col >> shift),
                           jnp.logical_and((row & (blk - 1)) >= m, (col & (blk - 1)) < m))
    return jnp.where(keep, p, 0.0)


def _gla_kernel(q_ref, k_ref, v_ref, r_ref, small_ref, w2_ref, gb_ref, nw_ref, sel_ref, s0_ref,
                o_ref, sout_ref, s_scr):
    rows = MIX_STEP_CHUNKS * CHUNK

    @pl.when(pl.program_id(1) == 0)
    def _():
        s_scr[...] = s0_ref[...]

    q_all = q_ref[...].astype(F32)
    k_all = k_ref[...].astype(F32)
    v_all = v_ref[...].astype(F32)
    r_all = r_ref[...].astype(F32)
    small = small_ref[...]

    gate = _dot(small, w2_ref[...]) + gb_ref[...]
    log_a = -_softplus(-gate) * (1.0 / GLA_GATE_NORMALIZER)
    rrow = _iota2((rows, rows), 0)
    rcol = _iota2((rows, rows), 1)
    tril_blk = jnp.where(jnp.logical_and(rrow >= rcol, (rrow >> CHUNK_SHIFT) == (rcol >> CHUNK_SHIFT)),
                         1.0, 0.0).astype(F32)
    b_all = _dot_hilo(tril_blk, log_a) * LOG2E

    pairs = [(j, h) for j in range(MIX_STEP_CHUNKS) for h in range(GLA_HEADS)]
    ng = len(pairs)

    def rs(j):
        return slice(j * CHUNK, (j + 1) * CHUNK)

    def gather(arr, width):
        return jnp.stack([arr[rs(j), h * width:(h + 1) * width] for j, h in pairs])

    q = gather(q_all, GLA_DK)
    k = gather(k_all, GLA_DK)
    b = gather(b_all, GLA_DK)
    v = gather(v_all, GLA_DV)

    yield
    scores = _gla_level(q, k, b, 32)
    yield
    scores = scores + _gla_level(q, k, b, 16)
    yield
    scores = scores + _gla_level(q, k, b, SUBLANES)
    yield

    diag = []
    nb8 = SUBLANES
    for blk in range(CHUNK // nb8):
        bs = b[:, blk * nb8:(blk + 1) * nb8, :]
        qs = q[:, blk * nb8:(blk + 1) * nb8, :]
        ks = k[:, blk * nb8:(blk + 1) * nb8, :]
        pj = [qs * (ks[:, j:j + 1, :] * jnp.exp2(jnp.minimum(bs - bs[:, j:j + 1, :], 0.0))) for j in range(nb8)]
        p = jnp.concatenate(pj, axis=2).astype(BF16).reshape(ng * nb8, nb8 * GLA_DK)
        d = jnp.dot(p, sel_ref[blk], preferred_element_type=F32)
        diag.append(d.reshape(ng, nb8, CHUNK))
        yield
    row = _iota2((CHUNK, CHUNK), 0)
    col = _iota2((CHUNK, CHUNK), 1)
    scores = scores + jnp.where(row >= col, jnp.concatenate(diag, axis=1), 0.0)

    blast = b[:, CHUNK - 1:CHUNK, :]
    qe = q * jnp.exp2(b)
    kdec = k * jnp.exp2(blast - b)
    v_t = jnp.stack([v[g].T for g in range(ng)])
    kv = _bmm(v_t, kdec)
    o_intra = _bmm(scores, v)
    eb_last = jnp.exp2(blast)
    yield

    nw = nw_ref[...]
    st = s_scr[...]
    for j in range(MIX_STEP_CHUNKS):
        gs = slice(j * GLA_HEADS, (j + 1) * GLA_HEADS)
        o = _bmm_nt(qe[gs], st) + o_intra[gs]
        st = st * eb_last[gs] + kv[gs]
        on = o * lax.rsqrt(jnp.mean(o * o, axis=-1, keepdims=True) + NORM_EPS) * nw
        for h in range(GLA_HEADS):
            rg = r_all[rs(j), h * GLA_DV:(h + 1) * GLA_DV]
            o_ref[rs(j), h * GLA_DV:(h + 1) * GLA_DV] = (on[h] * rg).astype(o_ref.dtype)
        yield
    s_scr[...] = st
    sout_ref[...] = st


def _mixer_kernel(proj_ref, small_ref, gp_ref, gnw_ref, w2_ref, gb_ref, lnw_ref, sel_ref, gs0_ref, ls0_ref,
                  o_ref, gsout_ref, lsout_ref, gdn_s, gla_s):
    gdn_cols = 2 * GDN_QK + 2 * GDN_V
    q0, k0, v0, r0 = gdn_cols, gdn_cols + GLA_QK, gdn_cols + 2 * GLA_QK, gdn_cols + 2 * GLA_QK + GLA_V
    streams = [
        _gdn_kernel(proj_ref.at[:, :gdn_cols], small_ref, gp_ref, gnw_ref, gs0_ref,
                    o_ref.at[:, :GDN_V], gsout_ref, gdn_s),
        _gla_kernel(proj_ref.at[:, q0:k0], proj_ref.at[:, k0:v0], proj_ref.at[:, v0:r0],
                    proj_ref.at[:, r0:r0 + GLA_V], small_ref, w2_ref, gb_ref, lnw_ref, sel_ref, ls0_ref,
                    o_ref.at[:, GDN_V:], lsout_ref, gla_s),
    ]
    while streams:
        for stream in list(streams):
            if next(stream, StopIteration) is StopIteration:
                streams.remove(stream)


def _mixer(proj, small, gparams, gdn_nw, w2_pad, gate_b, gla_nw, sel, gdn_s0, gla_s0, bsz, nstep):
    rows, ncol = proj.shape
    step_rows = MIX_STEP_CHUNKS * CHUNK

    def rmap(b, c):
        return (b * nstep + c, 0)

    def const(*shape):
        return pl.BlockSpec(shape, lambda b, c: (0,) * len(shape))

    return pl.pallas_call(
        _mixer_kernel,
        grid=(bsz, nstep),
        in_specs=[
            pl.BlockSpec((step_rows, ncol), rmap),
            pl.BlockSpec((step_rows, LANES), rmap),
            const(SUBLANES, LANES),
            const(1, GDN_DV),
            const(LANES, GLA_QK),
            const(1, GLA_QK),
            const(1, GLA_DV),
            const(CHUNK // SUBLANES, SUBLANES * GLA_DK, CHUNK),
            const(GDN_HEADS, GDN_DK, GDN_DV),
            const(GLA_HEADS, GLA_DV, GLA_DK),
        ],
        out_specs=[
            pl.BlockSpec((step_rows, GDN_V + GLA_V), rmap),
            const(GDN_HEADS, GDN_DK, GDN_DV),
            const(GLA_HEADS, GLA_DV, GLA_DK),
        ],
        out_shape=[
            jax.ShapeDtypeStruct((rows, GDN_V + GLA_V), BF16),
            jax.ShapeDtypeStruct((GDN_HEADS, GDN_DK, GDN_DV), F32),
            jax.ShapeDtypeStruct((GLA_HEADS, GLA_DV, GLA_DK), F32),
        ],
        scratch_shapes=[pltpu.VMEM((GDN_HEADS, GDN_DK, GDN_DV), F32),
                        pltpu.VMEM((GLA_HEADS, GLA_DV, GLA_DK), F32)],
        compiler_params=pltpu.CompilerParams(
            dimension_semantics=("arbitrary", "arbitrary"), vmem_limit_bytes=VMEM_LIMIT),
        name="mixer",
    )(proj, small, gparams, gdn_nw, w2_pad, gate_b, gla_nw, sel, gdn_s0, gla_s0)


def _outproj_kernel(x_ref, m_ref, w_ref, o_ref):
    o_ref[...] = x_ref[...] + jnp.dot(m_ref[...], w_ref[...], preferred_element_type=F32)


def _outproj(x2d, mixed, wo, tm):
    rows, d = x2d.shape
    width = mixed.shape[1]
    return pl.pallas_call(
        _outproj_kernel,
        grid=(rows // tm,),
        in_specs=[
            pl.BlockSpec((tm, d), lambda i: (i, 0)),
            pl.BlockSpec((tm, width), lambda i: (i, 0)),
            pl.BlockSpec((width, d), lambda i: (0, 0), pipeline_mode=pl.Buffered(1)),
        ],
        out_specs=pl.BlockSpec((tm, d), lambda i: (i, 0)),
        out_shape=jax.ShapeDtypeStruct((rows, d), F32),
        compiler_params=pltpu.CompilerParams(
            dimension_semantics=("arbitrary",), vmem_limit_bytes=VMEM_LIMIT),
        name="outproj",
    )(x2d, mixed, wo)


def _ffn_kernel(h_ref, nw_ref, wg_ref, wu_ref, wd_ref, fw_ref, o_ref, n_scr):
    f = pl.program_id(1)

    @pl.when(f == 0)
    def _():
        h = h_ref[...]
        ms = jnp.mean(h * h, axis=-1, keepdims=True)
        n_scr[...] = ((h * lax.rsqrt(ms + NORM_EPS)) * nw_ref[...]).astype(BF16)
        o_ref[...] = h

    n = n_scr[...]
    g = jnp.dot(n, wg_ref[...], preferred_element_type=F32)
    u = jnp.dot(n, wu_ref[...], preferred_element_type=F32)
    act = (_silu(g) * u).astype(BF16)
    o_ref[...] += jnp.dot(act, wd_ref[...], preferred_element_type=F32)

    @pl.when(f == pl.num_programs(1) - 1)
    def _():
        y = o_ref[...]
        ms = jnp.mean(y * y, axis=-1, keepdims=True)
        o_ref[...] = (y * lax.rsqrt(ms + NORM_EPS)) * fw_ref[...]


def _ffn(h2d, norm_w, w_gate, w_up, w_down, final_w, tm, tf):
    rows, d = h2d.shape
    dff = w_gate.shape[1]
    return pl.pallas_call(
        _ffn_kernel,
        grid=(rows // tm, dff // tf),
        in_specs=[
            pl.BlockSpec((tm, d), lambda i, f: (i, 0)),
            pl.BlockSpec((1, d), lambda i, f: (0, 0)),
            pl.BlockSpec((d, tf), lambda i, f: (0, f)),
            pl.BlockSpec((d, tf), lambda i, f: (0, f)),
            pl.BlockSpec((tf, d), lambda i, f: (f, 0)),
            pl.BlockSpec((1, d), lambda i, f: (0, 0)),
        ],
        out_specs=pl.BlockSpec((tm, d), lambda i, f: (i, 0)),
        out_shape=jax.ShapeDtypeStruct((rows, d), F32),
        scratch_shapes=[pltpu.VMEM((tm, d), BF16)],
        compiler_params=pltpu.CompilerParams(
            dimension_semantics=("arbitrary", "arbitrary"), vmem_limit_bytes=VMEM_LIMIT),
        name="ffn",
    )(h2d, norm_w, w_gate, w_up, w_down, final_w)


def _pick_tile(n, pref):
    t = min(n, pref)
    while n % t:
        t //= 2
    return t


def kernel(x, meta_tokens, attn_norm_w, w_in, gdn_conv_w, gdn_a_log, gdn_dt_bias, gdn_norm_w,
           gla_gate_w2, gla_gate_b, gla_norm_w, w_out, ffn_norm_w, w_gate, w_up, w_down, final_norm_w):
    bsz, seq, d = x.shape
    step_rows = MIX_STEP_CHUNKS * CHUNK
    assert seq % step_rows == 0 and attn_norm_w.shape[0] == 1
    rows = bsz * seq
    x2d = x.reshape(rows, d)

    w_all = w_in[0].astype(BF16)
    off_small = 2 * GDN_QK + 2 * GDN_V
    off_gla = off_small + 2 * GDN_HEADS
    off_lr = off_gla + 2 * GLA_QK + 2 * GLA_V
    w_gla = w_all[:, off_gla:off_lr]
    w_small = jnp.concatenate(
        [w_all[:, off_small:off_gla], w_all[:, off_lr:],
         jnp.zeros((d, LANES - 2 * GDN_HEADS - GLA_GATE_RANK), BF16)], axis=1)
    assert off_small + w_gla.shape[1] == MAIN_COLS
    nw_attn = attn_norm_w[0].reshape(1, d)
    gparams = jnp.zeros((SUBLANES, LANES), F32)
    gparams = gparams.at[0, :GDN_HEADS].set(gdn_a_log[0]).at[1, :GDN_HEADS].set(gdn_dt_bias[0])
    w2_pad = jnp.zeros((LANES, GLA_QK), F32).at[2 * GDN_HEADS:2 * GDN_HEADS + GLA_GATE_RANK].set(gla_gate_w2[0])

    h_pre = jnp.concatenate([jnp.zeros((step_rows - N_META, d), x.dtype), meta_tokens.astype(x.dtype)], axis=0)

    kk = jnp.arange(SUBLANES * GLA_DK) // GLA_DK
    sel = (kk[None, :, None] + SUBLANES * jnp.arange(CHUNK // SUBLANES)[:, None, None]
           == jnp.arange(CHUNK)[None, None, :]).astype(BF16)

    conv_w = gdn_conv_w[0]
    mixer_params = (gparams, gdn_norm_w[0].reshape(1, GDN_DV), w2_pad.astype(BF16),
                    gla_gate_b[0].reshape(1, GLA_QK), gla_norm_w[0].reshape(1, GLA_DV), sel)
    proj_pre, small_pre, carry = _inproj(h_pre, nw_attn, w_all, w_gla, w_small, conv_w,
                                         jnp.zeros((SUBLANES, conv_w.shape[1]), F32), step_rows, 256, 1)
    _, gdn_s0, gla_s0 = _mixer(proj_pre, small_pre, *mixer_params,
                               jnp.zeros((GDN_HEADS, GDN_DK, GDN_DV), F32),
                               jnp.zeros((GLA_HEADS, GLA_DV, GLA_DK), F32), 1, 1)

    tm_in = _pick_tile(seq, 512)
    proj_main, small_main, _ = _inproj(x2d, nw_attn, w_all, w_gla, w_small, conv_w, carry, tm_in, 256,
                                       seq // tm_in)
    mixed, _, _ = _mixer(proj_main, small_main, *mixer_params, gdn_s0, gla_s0, bsz, seq // step_rows)

    h2d = _outproj(x2d, mixed, w_out[0].astype(BF16), _pick_tile(rows, 1024))
    out = _ffn(h2d, ffn_norm_w[0].reshape(1, d), w_gate[0].astype(BF16), w_up[0].astype(BF16),
               w_down[0].astype(BF16), final_norm_w.reshape(1, d), _pick_tile(rows, 1024), 512)
    return out.reshape(bsz, seq, d)
```

```python
import functools

import jax
import jax.numpy as jnp
from jax import lax
from jax.experimental import pallas as pl
from jax.experimental.pallas import tpu as pltpu

F32 = jnp.float32
BF16 = jnp.bfloat16

N_META = 16
CONV_K = 4
GDN_HEADS = 8
GDN_DK = 128
GDN_DV = 128
GLA_HEADS = 4
GLA_DK = 128
GLA_DV = 256
GLA_GATE_RANK = 16
GLA_GATE_NORMALIZER = 16.0
GDN_QK = GDN_HEADS * GDN_DK
GDN_V = GDN_HEADS * GDN_DV
GLA_QK = GLA_HEADS * GLA_DK
GLA_V = GLA_HEADS * GLA_DV
NORM_EPS = 1e-6

CHUNK = 64
CHUNK_SHIFT = CHUNK.bit_length() - 1
MIX_STEP_CHUNKS = 4
LANES = 128
SUBLANES = 8
MAIN_COLS = 2 * GDN_QK + 2 * GDN_V + 2 * GLA_QK + 2 * GLA_V
VMEM_LIMIT = 56 * 1024 * 1024
LOG2E = 1.4426950408889634


def _dot(a, b):
    return jnp.dot(a.astype(BF16), b.astype(BF16), preferred_element_type=F32)


def _dot_hilo(a_exact, b):
    bh = b.astype(BF16)
    bl = (b - bh.astype(F32)).astype(BF16)
    ab = a_exact.astype(BF16)
    return (jnp.dot(ab, bh, preferred_element_type=F32)
            + jnp.dot(ab, bl, preferred_element_type=F32))


def _bmm(a, b):
    return jnp.einsum("gij,gjk->gik", a.astype(BF16), b.astype(BF16), preferred_element_type=F32)


def _bmm_nt(a, b):
    return jnp.einsum("gik,gjk->gij", a.astype(BF16), b.astype(BF16), preferred_element_type=F32)


def _sigmoid(x):
    return 1.0 / (1.0 + jnp.exp2(x * (-LOG2E)))


def _silu(x):
    return x * _sigmoid(x)


def _softplus(x):
    return jnp.maximum(x, 0.0) + jnp.log(1.0 + jnp.exp(-jnp.abs(x)))


def _iota2(shape, dim):
    return lax.broadcasted_iota(jnp.int32, shape, dim)


def _l2norm_heads(y, scale):
    outs = []
    for h in range(y.shape[1] // LANES):
        yh = y[:, h * LANES:(h + 1) * LANES]
        outs.append(yh * (lax.rsqrt(jnp.sum(yh * yh, axis=-1, keepdims=True) + NORM_EPS) * scale))
    return jnp.concatenate(outs, axis=1)


def _inproj_kernel(x_ref, nw_ref, wa_ref, wb_ref, ws_ref, cw_ref, carry_ref, o_ref, os_ref, tail_ref, tail_scr,
                   acc_scr, *, tn, steps_per_seq):
    i = pl.program_id(0)
    tm = x_ref.shape[0]

    @pl.when(i % steps_per_seq == 0)
    def _():
        tail_scr[...] = carry_ref[...]

    x = x_ref[...]
    ms = jnp.mean(x * x, axis=-1, keepdims=True)
    nb = ((x * lax.rsqrt(ms + NORM_EPS)) * nw_ref[...]).astype(BF16)
    os_ref[...] = jnp.dot(nb, ws_ref[...], preferred_element_type=F32)
    nqkv = 2 * GDN_QK + GDN_V
    gla_q0 = nqkv + GDN_V
    gla_r0 = gla_q0 + 2 * GLA_QK + GLA_V
    na = wa_ref.shape[1]
    ntile = (na + wb_ref.shape[1]) // tn

    hist = SUBLANES

    def matmul_tile(j):
        cs = slice(j * tn, (j + 1) * tn)
        w_tile = wa_ref[:, cs] if j * tn < na else wb_ref[:, j * tn - na:(j + 1) * tn - na]
        acc_scr[hist:hist + tm, :] = jnp.dot(nb, w_tile, preferred_element_type=F32)
        if j * tn < nqkv:
            acc_scr[0:hist, :] = tail_scr[:, cs]
            tail_scr[:, cs] = acc_scr[tm:tm + hist, :]

    def epilogue_tile(j):
        c0 = j * tn
        cs = slice(c0, c0 + tn)
        acc = acc_scr[hist:hist + tm, :]
        if c0 < nqkv:
            y = acc * cw_ref[CONV_K - 1:CONV_K, cs]
            for t in range(1, CONV_K):
                y = y + acc_scr[hist - t:hist - t + tm, :] * cw_ref[CONV_K - 1 - t:CONV_K - t, cs]
            y = _silu(y)
            if c0 < GDN_QK:
                y = _l2norm_heads(y, GDN_DK ** -0.5)
            elif c0 < 2 * GDN_QK:
                y = _l2norm_heads(y, 1.0)
        elif c0 < gla_q0 or c0 >= gla_r0:
            y = _silu(acc)
        elif c0 < gla_q0 + GLA_QK:
            y = acc * (GLA_DK ** -0.5)
        else:
            y = acc
        o_ref[:, cs] = y.astype(o_ref.dtype)

    for j in range(ntile):
        matmul_tile(j)
        epilogue_tile(j)
    tail_ref[...] = tail_scr[...]


def _inproj(x2d, norm_w, w_gdn, w_gla, w_small, conv_w, carry, tm, tn, steps_per_seq):
    rows, d = x2d.shape
    gdn_cols = 2 * GDN_QK + 2 * GDN_V
    ncol = gdn_cols + w_gla.shape[1]
    nqkv = 2 * GDN_QK + GDN_V
    assert GDN_QK % tn == 0 and GDN_V % tn == 0 and GLA_QK % tn == 0 and tn % LANES == 0
    resident = pl.Buffered(1)
    return pl.pallas_call(
        functools.partial(_inproj_kernel, tn=tn, steps_per_seq=steps_per_seq),
        grid=(rows // tm,),
        in_specs=[
            pl.BlockSpec((tm, d), lambda i: (i, 0)),
            pl.BlockSpec((1, d), lambda i: (0, 0), pipeline_mode=resident),
            pl.BlockSpec((d, gdn_cols), lambda i: (0, 0), pipeline_mode=resident),
            pl.BlockSpec((d, ncol - gdn_cols), lambda i: (0, 0), pipeline_mode=resident),
            pl.BlockSpec((d, LANES), lambda i: (0, 0), pipeline_mode=resident),
            pl.BlockSpec((CONV_K, nqkv), lambda i: (0, 0), pipeline_mode=resident),
            pl.BlockSpec((SUBLANES, nqkv), lambda i: (0, 0), pipeline_mode=resident),
        ],
        out_specs=[
            pl.BlockSpec((tm, ncol), lambda i: (i, 0)),
            pl.BlockSpec((tm, LANES), lambda i: (i, 0)),
            pl.BlockSpec((SUBLANES, nqkv), lambda i: (0, 0)),
        ],
        out_shape=[
            jax.ShapeDtypeStruct((rows, ncol), BF16),
            jax.ShapeDtypeStruct((rows, LANES), F32),
            jax.ShapeDtypeStruct((SUBLANES, nqkv), F32),
        ],
        scratch_shapes=[pltpu.VMEM((SUBLANES, nqkv), F32), pltpu.VMEM((SUBLANES + tm, tn), F32)],
        compiler_params=pltpu.CompilerParams(
            dimension_semantics=("arbitrary",), vmem_limit_bytes=VMEM_LIMIT),
        name="inproj",
    )(x2d, norm_w, w_gdn, w_gla, w_small, conv_w, carry)


def _pair_matmul(x, p):
    pb = p.astype(BF16)
    lane = _iota2((CHUNK, 2 * CHUNK), 1)
    zero = jnp.zeros_like(pb)
    blockdiag = jnp.concatenate([jnp.where(lane < CHUNK, pb, zero), jnp.where(lane >= CHUNK, pb, zero)], axis=1)
    return _bmm(x, blockdiag)


def _unit_lower_inverse(a, row, col):
    same16 = (row >> 4) == (col >> 4)
    same32 = (row >> 5) == (col >> 5)
    eye = jnp.where(row == col, 1.0, 0.0).astype(F32)
    a0 = jnp.where(same16, a, 0.0)
    x = eye - a0
    p = _pair_matmul(a0, a0)
    yield
    x = x + _pair_matmul(x, p)
    p = _pair_matmul(p, p)
    yield
    x = x + _pair_matmul(x, p)
    p = _pair_matmul(p, p)
    yield
    x = x + _pair_matmul(x, p)
    yield
    l1 = jnp.where(jnp.logical_and(same32, jnp.logical_not(same16)), a, 0.0)
    t = _pair_matmul(x, l1)
    yield
    x = x - _pair_matmul(t, x)
    yield
    l2 = jnp.where(same32, 0.0, a)
    t = _pair_matmul(x, l2)
    yield
    x = x - _pair_matmul(t, x)
    return x


def _gdn_kernel(proj_ref, small_ref, gp_ref, nw_ref, s0_ref, o_ref, sout_ref, s_scr):
    rows = MIX_STEP_CHUNKS * CHUNK

    @pl.when(pl.program_id(1) == 0)
    def _():
        s_scr[...] = s0_ref[...]

    proj = proj_ref[...]
    small = small_ref[...]
    nqkv = 2 * GDN_QK + GDN_V

    lane = _iota2((rows, LANES), 1)
    gp = gp_ref[...]
    g_all = -jnp.exp(gp[0:1, :]) * _softplus(small + gp[1:2, :])
    g_all = jnp.where(lane < GDN_HEADS, g_all, 0.0)
    rrow = _iota2((rows, rows), 0)
    rcol = _iota2((rows, rows), 1)
    tril_blk = jnp.where(jnp.logical_and(rrow >= rcol, (rrow >> CHUNK_SHIFT) == (rcol >> CHUNK_SHIFT)),
                         1.0, 0.0).astype(F32)
    gc_all = _dot_hilo(tril_blk, g_all) * LOG2E
    beta_all = _sigmoid(small)

    pairs = [(j, h) for j in range(MIX_STEP_CHUNKS) for h in range(GDN_HEADS)]

    def rs(j):
        return slice(j * CHUNK, (j + 1) * CHUNK)

    def gather(arr, col0, width):
        return jnp.stack([arr[rs(j), col0 + h * width:col0 + (h + 1) * width] for j, h in pairs])

    q = gather(proj, 0, GDN_DK).astype(F32)
    k = gather(proj, GDN_QK, GDN_DK).astype(F32)
    v = gather(proj, 2 * GDN_QK, GDN_DV).astype(F32)
    beta = jnp.broadcast_to(gather(beta_all, GDN_HEADS, 1), k.shape)
    gcol = jnp.broadcast_to(gather(gc_all, 0, 1), k.shape)
    glast = jnp.stack([gc_all[(j + 1) * CHUNK - 1:(j + 1) * CHUNK, h:h + 1] for j, h in pairs])
    kb = k * beta

    ng = len(pairs)
    npair = ng // 2
    lane_p = _iota2((CHUNK, 2 * CHUNK), 1)
    row = _iota2((CHUNK, 2 * CHUNK), 0)
    col = lane_p & (CHUNK - 1)
    first = lane_p < CHUNK

    def interleave(even, odd):
        return jnp.stack([even, odd], axis=1).reshape((ng,) + even.shape[1:])

    def pad_even(t):
        t = t.astype(BF16)
        return jnp.concatenate([t, jnp.zeros_like(t)], axis=0)

    def pad_odd(t):
        t = t.astype(BF16)
        return jnp.concatenate([jnp.zeros_like(t), t], axis=0)

    gc_t = [jnp.concatenate([gc_all[rs(j), :], gc_all[rs(j), :]], axis=0).T for j in range(MIX_STEP_CHUNKS)]
    gcol_p = jnp.stack([jnp.where(first, gcol[2 * p], gcol[2 * p + 1]) for p in range(npair)])
    grow_p = jnp.stack([jnp.where(first[0:1], gc_t[pairs[2 * p][0]][pairs[2 * p][1]:pairs[2 * p][1] + 1, :],
                                  gc_t[pairs[2 * p + 1][0]][pairs[2 * p + 1][1]:pairs[2 * p + 1][1] + 1, :])
                        for p in range(npair)])
    decay = jnp.exp2(jnp.where(row >= col, gcol_p - grow_p, -jnp.inf))
    zk = jnp.zeros((CHUNK, GDN_DK), BF16)
    q16, k16, kb16 = q.astype(BF16), k.astype(BF16), kb.astype(BF16)
    lhs = jnp.stack([jnp.concatenate([jnp.concatenate([q16[2 * p], kb16[2 * p]], axis=0),
                                      jnp.concatenate([q16[2 * p + 1], kb16[2 * p + 1]], axis=0)], axis=1)
                     for p in range(npair)])
    k_diag = jnp.stack([jnp.concatenate([jnp.concatenate([k16[2 * p], zk], axis=1),
                                         jnp.concatenate([zk, k16[2 * p + 1]], axis=1)], axis=0)
                        for p in range(npair)])
    qk_kk = _bmm_nt(lhs, k_diag)
    yield
    qk = qk_kk[:, :CHUNK] * decay
    a_low = jnp.where(row > col, qk_kk[:, CHUNK:] * decay, 0.0)
    tinv = yield from _unit_lower_inverse(a_low, row, col)
    egc = jnp.exp2(gcol)
    rhs = jnp.concatenate([v * beta, kb * egc], axis=2)
    uw = interleave(_bmm(tinv, jnp.stack([pad_even(rhs[2 * p]) for p in range(npair)])),
                    _bmm(tinv, jnp.stack([pad_odd(rhs[2 * p + 1]) for p in range(npair)])))
    yield
    kd = k * jnp.exp2(glast - gcol)
    kd_t = jnp.stack([kd[g].T for g in range(ng)])
    kd_uw = _bmm(kd_t, uw)
    yield
    qk_uw = interleave(_bmm(qk, jnp.stack([pad_even(uw[2 * p]) for p in range(npair)])),
                       _bmm(qk, jnp.stack([pad_odd(uw[2 * p + 1]) for p in range(npair)])))
    q_eff = q * egc - qk_uw[:, :, GDN_DV:]
    eg_last = jnp.exp2(glast)
    yield

    nw = nw_ref[...]
    s = s_scr[...]
    for j in range(MIX_STEP_CHUNKS):
        gs = slice(j * GDN_HEADS, (j + 1) * GDN_HEADS)
        sb = s.astype(BF16)
        o = _bmm(q_eff[gs], sb) + qk_uw[gs, :, :GDN_DV]
        s = s * eg_last[gs] - _bmm(kd_uw[gs, :, GDN_DV:], sb) + kd_uw[gs, :, :GDN_DV]
        on = o * lax.rsqrt(jnp.mean(o * o, axis=-1, keepdims=True) + NORM_EPS) * nw
        for h in range(GDN_HEADS):
            zg = proj[rs(j), nqkv + h * GDN_DV:nqkv + (h + 1) * GDN_DV].astype(F32)
            o_ref[rs(j), h * GDN_DV:(h + 1) * GDN_DV] = (on[h] * zg).astype(o_ref.dtype)
        yield
    s_scr[...] = s
    sout_ref[...] = s


def _gla_level(q, k, b, m):
    g = q.shape[0]
    ref = jnp.concatenate(
        [jnp.broadcast_to(b[:, t + m - 1:t + m, :], (g, 2 * m, GLA_DK)) for t in range(0, CHUNK, 2 * m)], axis=1)
    p = _bmm_nt(q * jnp.exp2(jnp.minimum(b - ref, 0.0)), k * jnp.exp2(jnp.minimum(ref - b, 0.0)))
    row = _iota2((CHUNK, CHUNK), 0)
    col = _iota2((CHUNK, CHUNK), 1)
    blk = 2 * m
    shift = blk.bit_length() - 1
    keep = jnp.logical_and((row >> shift) == (col >> shift),
                           jnp.logical_and((row & (blk - 1)) >= m, (col & (blk - 1)) < m))
    return jnp.where(keep, p, 0.0)


def _gla_kernel(q_ref, k_ref, v_ref, r_ref, small_ref, w2_ref, gb_ref, nw_ref, sel_ref, s0_ref,
                o_ref, sout_ref, s_scr):
    rows = MIX_STEP_CHUNKS * CHUNK

    @pl.when(pl.program_id(1) == 0)
    def _():
        s_scr[...] = s0_ref[...]

    q_all = q_ref[...].astype(F32)
    k_all = k_ref[...].astype(F32)
    v_all = v_ref[...].astype(F32)
    r_all = r_ref[...].astype(F32)
    small = small_ref[...]

    gate = _dot(small, w2_ref[...]) + gb_ref[...]
    log_a = -_softplus(-gate) * (1.0 / GLA_GATE_NORMALIZER)
    rrow = _iota2((rows, rows), 0)
    rcol = _iota2((rows, rows), 1)
    tril_blk = jnp.where(jnp.logical_and(rrow >= rcol, (rrow >> CHUNK_SHIFT) == (rcol >> CHUNK_SHIFT)),
                         1.0, 0.0).astype(F32)
    b_all = _dot_hilo(tril_blk, log_a) * LOG2E

    pairs = [(j, h) for j in range(MIX_STEP_CHUNKS) for h in range(GLA_HEADS)]
    ng = len(pairs)

    def rs(j):
        return slice(j * CHUNK, (j + 1) * CHUNK)

    def gather(arr, width):
        return jnp.stack([arr[rs(j), h * width:(h + 1) * width] for j, h in pairs])

    q = gather(q_all, GLA_DK)
    k = gather(k_all, GLA_DK)
    b = gather(b_all, GLA_DK)
    v = gather(v_all, GLA_DV)

    yield
    scores = _gla_level(q, k, b, 32)
    yield
    scores = scores + _gla_level(q, k, b, 16)
    yield
    scores = scores + _gla_level(q, k, b, SUBLANES)
    yield

    diag = []
    nb8 = SUBLANES
    for blk in range(CHUNK // nb8):
        bs = b[:, blk * nb8:(blk + 1) * nb8, :]
        qs = q[:, blk * nb8:(blk + 1) * nb8, :]
        ks = k[:, blk * nb8:(blk + 1) * nb8, :]
        pj = [qs * (ks[:, j:j + 1, :] * jnp.exp2(jnp.minimum(bs - bs[:, j:j + 1, :], 0.0))) for j in range(nb8)]
        p = jnp.concatenate(pj, axis=2).astype(BF16).reshape(ng * nb8, nb8 * GLA_DK)
        d = jnp.dot(p, sel_ref[blk], preferred_element_type=F32)
        diag.append(d.reshape(ng, nb8, CHUNK))
        yield
    row = _iota2((CHUNK, CHUNK), 0)
    col = _iota2((CHUNK, CHUNK), 1)
    scores = scores + jnp.where(row >= col, jnp.concatenate(diag, axis=1), 0.0)

    blast = b[:, CHUNK - 1:CHUNK, :]
    qe = q * jnp.exp2(b)
    kdec = k * jnp.exp2(blast - b)
    v_t = jnp.stack([v[g].T for g in range(ng)])
    kv = _bmm(v_t, kdec)
    o_intra = _bmm(scores, v)
    eb_last = jnp.exp2(blast)
    yield

    nw = nw_ref[...]
    st = s_scr[...]
    for j in range(MIX_STEP_CHUNKS):
        gs = slice(j * GLA_HEADS, (j + 1) * GLA_HEADS)
        o = _bmm_nt(qe[gs], st) + o_intra[gs]
        st = st * eb_last[gs] + kv[gs]
        on = o * lax.rsqrt(jnp.mean(o * o, axis=-1, keepdims=True) + NORM_EPS) * nw
        for h in range(GLA_HEADS):
            rg = r_all[rs(j), h * GLA_DV:(h + 1) * GLA_DV]
            o_ref[rs(j), h * GLA_DV:(h + 1) * GLA_DV] = (on[h] * rg).astype(o_ref.dtype)
        yield
    s_scr[...] = st
    sout_ref[...] = st


def _mixer_kernel(proj_ref, small_ref, gp_ref, gnw_ref, w2_ref, gb_ref, lnw_ref, sel_ref, gs0_ref, ls0_ref,
                  o_ref, gsout_ref, lsout_ref, gdn_s, gla_s):
    gdn_cols = 2 * GDN_QK + 2 * GDN_V
    q0, k0, v0, r0 = gdn_cols, gdn_cols + GLA_QK, gdn_cols + 2 * GLA_QK, gdn_cols + 2 * GLA_QK + GLA_V
    streams = [
        _gdn_kernel(proj_ref.at[:, :gdn_cols], small_ref, gp_ref, gnw_ref, gs0_ref,
                    o_ref.at[:, :GDN_V], gsout_ref, gdn_s),
        _gla_kernel(proj_ref.at[:, q0:k0], proj_ref.at[:, k0:v0], proj_ref.at[:, v0:r0],
                    proj_ref.at[:, r0:r0 + GLA_V], small_ref, w2_ref, gb_ref, lnw_ref, sel_ref, ls0_ref,
                    o_ref.at[:, GDN_V:], lsout_ref, gla_s),
    ]
    while streams:
        for stream in list(streams):
            if next(stream, StopIteration) is StopIteration:
                streams.remove(stream)


def _mixer(proj, small, gparams, gdn_nw, w2_pad, gate_b, gla_nw, sel, gdn_s0, gla_s0, bsz, nstep):
    rows, ncol = proj.shape
    step_rows = MIX_STEP_CHUNKS * CHUNK

    def rmap(b, c):
        return (b * nstep + c, 0)

    def const(*shape):
        return pl.BlockSpec(shape, lambda b, c: (0,) * len(shape))

    return pl.pallas_call(
        _mixer_kernel,
        grid=(bsz, nstep),
        in_specs=[
            pl.BlockSpec((step_rows, ncol), rmap),
            pl.BlockSpec((step_rows, LANES), rmap),
            const(SUBLANES, LANES),
            const(1, GDN_DV),
            const(LANES, GLA_QK),
            const(1, GLA_QK),
            const(1, GLA_DV),
            const(CHUNK // SUBLANES, SUBLANES * GLA_DK, CHUNK),
            const(GDN_HEADS, GDN_DK, GDN_DV),
            const(GLA_HEADS, GLA_DV, GLA_DK),
        ],
        out_specs=[
            pl.BlockSpec((step_rows, GDN_V + GLA_V), rmap),
            const(GDN_HEADS, GDN_DK, GDN_DV),
            const(GLA_HEADS, GLA_DV, GLA_DK),
        ],
        out_shape=[
            jax.ShapeDtypeStruct((rows, GDN_V + GLA_V), BF16),
            jax.ShapeDtypeStruct((GDN_HEADS, GDN_DK, GDN_DV), F32),
            jax.ShapeDtypeStruct((GLA_HEADS, GLA_DV, GLA_DK), F32),
        ],
        scratch_shapes=[pltpu.VMEM((GDN_HEADS, GDN_DK, GDN_DV), F32),
                        pltpu.VMEM((GLA_HEADS, GLA_DV, GLA_DK), F32)],
        compiler_params=pltpu.CompilerParams(
            dimension_semantics=("arbitrary", "arbitrary"), vmem_limit_bytes=VMEM_LIMIT),
        name="mixer",
    )(proj, small, gparams, gdn_nw, w2_pad, gate_b, gla_nw, sel, gdn_s0, gla_s0)


def _outproj_kernel(x_ref, m_ref, w_ref, o_ref):
    o_ref[...] = x_ref[...] + jnp.dot(m_ref[...], w_ref[...], preferred_element_type=F32)


def _outproj(x2d, mixed, wo, tm):
    rows, d = x2d.shape
    width = mixed.shape[1]
    return pl.pallas_call(
        _outproj_kernel,
        grid=(rows // tm,),
        in_specs=[
            pl.BlockSpec((tm, d), lambda i: (i, 0)),
            pl.BlockSpec((tm, width), lambda i: (i, 0)),
            pl.BlockSpec((width, d), lambda i: (0, 0), pipeline_mode=pl.Buffered(1)),
        ],
        out_specs=pl.BlockSpec((tm, d), lambda i: (i, 0)),
        out_shape=jax.ShapeDtypeStruct((rows, d), F32),
        compiler_params=pltpu.CompilerParams(
            dimension_semantics=("arbitrary",), vmem_limit_bytes=VMEM_LIMIT),
        name="outproj",
    )(x2d, mixed, wo)


def _ffn_kernel(h_ref, nw_ref, wg_ref, wu_ref, wd_ref, fw_ref, o_ref, n_scr):
    f = pl.program_id(1)

    @pl.when(f == 0)
    def _():
        h = h_ref[...]
        ms = jnp.mean(h * h, axis=-1, keepdims=True)
        n_scr[...] = ((h * lax.rsqrt(ms + NORM_EPS)) * nw_ref[...]).astype(BF16)
        o_ref[...] = h

    n = n_scr[...]
    g = jnp.dot(n, wg_ref[...], preferred_element_type=F32)
    u = jnp.dot(n, wu_ref[...], preferred_element_type=F32)
    act = (_silu(g) * u).astype(BF16)
    o_ref[...] += jnp.dot(act, wd_ref[...], preferred_element_type=F32)

    @pl.when(f == pl.num_programs(1) - 1)
    def _():
        y = o_ref[...]
        ms = jnp.mean(y * y, axis=-1, keepdims=True)
        o_ref[...] = (y * lax.rsqrt(ms + NORM_EPS)) * fw_ref[...]


def _ffn(h2d, norm_w, w_gate, w_up, w_down, final_w, tm, tf):
    rows, d = h2d.shape
    dff = w_gate.shape[1]
    return pl.pallas_call(
        _ffn_kernel,
        grid=(rows // tm, dff // tf),
        in_specs=[
            pl.BlockSpec((tm, d), lambda i, f: (i, 0)),
            pl.BlockSpec((1, d), lambda i, f: (0, 0)),
            pl.BlockSpec((d, tf), lambda i, f: (0, f)),
            pl.BlockSpec((d, tf), lambda i, f: (0, f)),
            pl.BlockSpec((tf, d), lambda i, f: (f, 0)),
            pl.BlockSpec((1, d), lambda i, f: (0, 0)),
        ],
        out_specs=pl.BlockSpec((tm, d), lambda i, f: (i, 0)),
        out_shape=jax.ShapeDtypeStruct((rows, d), F32),
        scratch_shapes=[pltpu.VMEM((tm, d), BF16)],
        compiler_params=pltpu.CompilerParams(
            dimension_semantics=("arbitrary", "arbitrary"), vmem_limit_bytes=VMEM_LIMIT),
        name="ffn",
    )(h2d, norm_w, w_gate, w_up, w_down, final_w)


def _pick_tile(n, pref):
    t = min(n, pref)
    while n % t:
        t //= 2
    return t


def kernel(x, meta_tokens, attn_norm_w, w_in, gdn_conv_w, gdn_a_log, gdn_dt_bias, gdn_norm_w,
           gla_gate_w2, gla_gate_b, gla_norm_w, w_out, ffn_norm_w, w_gate, w_up, w_down, final_norm_w):
    bsz, seq, d = x.shape
    step_rows = MIX_STEP_CHUNKS * CHUNK
    assert seq % step_rows == 0 and attn_norm_w.shape[0] == 1
    rows = bsz * seq
    x2d = x.reshape(rows, d)

    wi = w_in[0]
    off_small = 2 * GDN_QK + 2 * GDN_V
    off_gla = off_small + 2 * GDN_HEADS
    off_lr = off_gla + 2 * GLA_QK + 2 * GLA_V
    w_gdn = wi[:, :off_small].astype(BF16)
    w_gla = wi[:, off_gla:off_lr].astype(BF16)
    w_small = jnp.concatenate(
        [wi[:, off_small:off_gla], wi[:, off_lr:],
         jnp.zeros((d, LANES - 2 * GDN_HEADS - GLA_GATE_RANK), F32)], axis=1).astype(BF16)
    assert off_small + w_gla.shape[1] == MAIN_COLS
    nw_attn = attn_norm_w[0].reshape(1, d)
    gparams = jnp.zeros((SUBLANES, LANES), F32)
    gparams = gparams.at[0, :GDN_HEADS].set(gdn_a_log[0]).at[1, :GDN_HEADS].set(gdn_dt_bias[0])
    w2_pad = jnp.zeros((LANES, GLA_QK), F32).at[2 * GDN_HEADS:2 * GDN_HEADS + GLA_GATE_RANK].set(gla_gate_w2[0])

    h_pre = jnp.concatenate([jnp.zeros((step_rows - N_META, d), x.dtype), meta_tokens.astype(x.dtype)], axis=0)

    kk = jnp.arange(SUBLANES * GLA_DK) // GLA_DK
    sel = (kk[None, :, None] + SUBLANES * jnp.arange(CHUNK // SUBLANES)[:, None, None]
           == jnp.arange(CHUNK)[None, None, :]).astype(BF16)

    conv_w = gdn_conv_w[0]
    mixer_params = (gparams, gdn_norm_w[0].reshape(1, GDN_DV), w2_pad.astype(BF16),
                    gla_gate_b[0].reshape(1, GLA_QK), gla_norm_w[0].reshape(1, GLA_DV), sel)
    proj_pre, small_pre, carry = _inproj(h_pre, nw_attn, w_gdn, w_gla, w_small, conv_w,
                                         jnp.zeros((SUBLANES, conv_w.shape[1]), F32), step_rows, 256, 1)
    _, gdn_s0, gla_s0 = _mixer(proj_pre, small_pre, *mixer_params,
                               jnp.zeros((GDN_HEADS, GDN_DK, GDN_DV), F32),
                               jnp.zeros((GLA_HEADS, GLA_DV, GLA_DK), F32), 1, 1)

    tm_in = _pick_tile(seq, 512)
    proj_main, small_main, _ = _inproj(x2d, nw_attn, w_gdn, w_gla, w_small, conv_w, carry, tm_in, 256,
                                       seq // tm_in)
    mixed, _, _ = _mixer(proj_main, small_main, *mixer_params, gdn_s0, gla_s0, bsz, seq // step_rows)

    h2d = _outproj(x2d, mixed, w_out[0].astype(BF16), _pick_tile(rows, 1024))
    out = _ffn(h2d, ffn_norm_w[0].reshape(1, d), w_gate[0].astype(BF16), w_up[0].astype(BF16),
               w_down[0].astype(BF16), final_norm_w.reshape(1, d), _pick_tile(rows, 1024), 512)
    return out.reshape(bsz, seq, d)
```

```python
import functools

import jax
import jax.numpy as jnp
from jax import lax
from jax.experimental import pallas as pl
from jax.experimental.pallas import tpu as pltpu

F32 = jnp.float32
BF16 = jnp.bfloat16

N_META = 16
CONV_K = 4
GDN_HEADS = 8
GDN_DK = 128
GDN_DV = 128
GLA_HEADS = 4
GLA_DK = 128
GLA_DV = 256
GLA_GATE_RANK = 16
GLA_GATE_NORMALIZER = 16.0
GDN_QK = GDN_HEADS * GDN_DK
GDN_V = GDN_HEADS * GDN_DV
GLA_QK = GLA_HEADS * GLA_DK
GLA_V = GLA_HEADS * GLA_DV
NORM_EPS = 1e-6

CHUNK = 64
CHUNK_SHIFT = CHUNK.bit_length() - 1
MIX_STEP_CHUNKS = 8
LANES = 128
SUBLANES = 8
MAIN_COLS = 2 * GDN_QK + 2 * GDN_V + 2 * GLA_QK + 2 * GLA_V
VMEM_LIMIT = 56 * 1024 * 1024
LOG2E = 1.4426950408889634


def _dot(a, b):
    return jnp.dot(a.astype(BF16), b.astype(BF16), preferred_element_type=F32)


def _dot_hilo(a_exact, b):
    bh = b.astype(BF16)
    bl = (b - bh.astype(F32)).astype(BF16)
    ab = a_exact.astype(BF16)
    return (jnp.dot(ab, bh, preferred_element_type=F32)
            + jnp.dot(ab, bl, preferred_element_type=F32))


def _bmm(a, b):
    return jnp.einsum("gij,gjk->gik", a.astype(BF16), b.astype(BF16), preferred_element_type=F32)


def _bmm_nt(a, b):
    return jnp.einsum("gik,gjk->gij", a.astype(BF16), b.astype(BF16), preferred_element_type=F32)


def _sigmoid(x):
    return 1.0 / (1.0 + jnp.exp2(x * (-LOG2E)))


def _silu(x):
    return x * _sigmoid(x)


def _softplus(x):
    return jnp.maximum(x, 0.0) + jnp.log(1.0 + jnp.exp(-jnp.abs(x)))


def _iota2(shape, dim):
    return lax.broadcasted_iota(jnp.int32, shape, dim)


def _l2norm_heads(y, scale):
    outs = []
    for h in range(y.shape[1] // LANES):
        yh = y[:, h * LANES:(h + 1) * LANES]
        outs.append(yh * (lax.rsqrt(jnp.sum(yh * yh, axis=-1, keepdims=True) + NORM_EPS) * scale))
    return jnp.concatenate(outs, axis=1)


def _inproj_kernel(x_ref, nw_ref, wa_ref, wb_ref, ws_ref, cw_ref, carry_ref, o_ref, os_ref, tail_ref, tail_scr,
                   acc_scr, *, tn, steps_per_seq):
    i = pl.program_id(0)
    tm = x_ref.shape[0]

    @pl.when(i % steps_per_seq == 0)
    def _():
        tail_scr[...] = carry_ref[...]

    x = x_ref[...]
    ms = jnp.mean(x * x, axis=-1, keepdims=True)
    nb = ((x * lax.rsqrt(ms + NORM_EPS)) * nw_ref[...]).astype(BF16)
    os_ref[...] = jnp.dot(nb, ws_ref[...], preferred_element_type=F32)
    nqkv = 2 * GDN_QK + GDN_V
    gla_q0 = nqkv + GDN_V
    gla_r0 = gla_q0 + 2 * GLA_QK + GLA_V
    na = wa_ref.shape[1]
    ntile = (na + wb_ref.shape[1]) // tn

    hist = SUBLANES

    def matmul_tile(j):
        cs = slice(j * tn, (j + 1) * tn)
        w_tile = wa_ref[:, cs] if j * tn < na else wb_ref[:, j * tn - na:(j + 1) * tn - na]
        acc_scr[hist:hist + tm, :] = jnp.dot(nb, w_tile, preferred_element_type=F32)
        if j * tn < nqkv:
            acc_scr[0:hist, :] = tail_scr[:, cs]
            tail_scr[:, cs] = acc_scr[tm:tm + hist, :]

    def epilogue_tile(j):
        c0 = j * tn
        cs = slice(c0, c0 + tn)
        acc = acc_scr[hist:hist + tm, :]
        if c0 < nqkv:
            y = acc * cw_ref[CONV_K - 1:CONV_K, cs]
            for t in range(1, CONV_K):
                y = y + acc_scr[hist - t:hist - t + tm, :] * cw_ref[CONV_K - 1 - t:CONV_K - t, cs]
            y = _silu(y)
            if c0 < GDN_QK:
                y = _l2norm_heads(y, GDN_DK ** -0.5)
            elif c0 < 2 * GDN_QK:
                y = _l2norm_heads(y, 1.0)
        elif c0 < gla_q0 or c0 >= gla_r0:
            y = _silu(acc)
        elif c0 < gla_q0 + GLA_QK:
            y = acc * (GLA_DK ** -0.5)
        else:
            y = acc
        o_ref[:, cs] = y.astype(o_ref.dtype)

    for j in range(ntile):
        matmul_tile(j)
        epilogue_tile(j)
    tail_ref[...] = tail_scr[...]


def _inproj(x2d, norm_w, w_gdn, w_gla, w_small, conv_w, carry, tm, tn, steps_per_seq):
    rows, d = x2d.shape
    gdn_cols = 2 * GDN_QK + 2 * GDN_V
    ncol = gdn_cols + w_gla.shape[1]
    nqkv = 2 * GDN_QK + GDN_V
    assert GDN_QK % tn == 0 and GDN_V % tn == 0 and GLA_QK % tn == 0 and tn % LANES == 0
    resident = pl.Buffered(1)
    return pl.pallas_call(
        functools.partial(_inproj_kernel, tn=tn, steps_per_seq=steps_per_seq),
        grid=(rows // tm,),
        in_specs=[
            pl.BlockSpec((tm, d), lambda i: (i, 0)),
            pl.BlockSpec((1, d), lambda i: (0, 0), pipeline_mode=resident),
            pl.BlockSpec((d, gdn_cols), lambda i: (0, 0), pipeline_mode=resident),
            pl.BlockSpec((d, ncol - gdn_cols), lambda i: (0, 0), pipeline_mode=resident),
            pl.BlockSpec((d, LANES), lambda i: (0, 0), pipeline_mode=resident),
            pl.BlockSpec((CONV_K, nqkv), lambda i: (0, 0), pipeline_mode=resident),
            pl.BlockSpec((SUBLANES, nqkv), lambda i: (0, 0), pipeline_mode=resident),
        ],
        out_specs=[
            pl.BlockSpec((tm, ncol), lambda i: (i, 0)),
            pl.BlockSpec((tm, LANES), lambda i: (i, 0)),
            pl.BlockSpec((SUBLANES, nqkv), lambda i: (0, 0)),
        ],
        out_shape=[
            jax.ShapeDtypeStruct((rows, ncol), BF16),
            jax.ShapeDtypeStruct((rows, LANES), F32),
            jax.ShapeDtypeStruct((SUBLANES, nqkv), F32),
        ],
        scratch_shapes=[pltpu.VMEM((SUBLANES, nqkv), F32), pltpu.VMEM((SUBLANES + tm, tn), F32)],
        compiler_params=pltpu.CompilerParams(
            dimension_semantics=("arbitrary",), vmem_limit_bytes=VMEM_LIMIT),
        name="inproj",
    )(x2d, norm_w, w_gdn, w_gla, w_small, conv_w, carry)


def _pair_matmul(x, p):
    pb = p.astype(BF16)
    lane = _iota2((CHUNK, 2 * CHUNK), 1)
    zero = jnp.zeros_like(pb)
    blockdiag = jnp.concatenate([jnp.where(lane < CHUNK, pb, zero), jnp.where(lane >= CHUNK, pb, zero)], axis=1)
    return _bmm(x, blockdiag)


def _unit_lower_inverse(a, row, col):
    same16 = (row >> 4) == (col >> 4)
    same32 = (row >> 5) == (col >> 5)
    eye = jnp.where(row == col, 1.0, 0.0).astype(F32)
    a0 = jnp.where(same16, a, 0.0)
    x = eye - a0
    p = _pair_matmul(a0, a0)
    yield
    x = x + _pair_matmul(x, p)
    p = _pair_matmul(p, p)
    yield
    x = x + _pair_matmul(x, p)
    p = _pair_matmul(p, p)
    yield
    x = x + _pair_matmul(x, p)
    yield
    l1 = jnp.where(jnp.logical_and(same32, jnp.logical_not(same16)), a, 0.0)
    t = _pair_matmul(x, l1)
    yield
    x = x - _pair_matmul(t, x)
    yield
    l2 = jnp.where(same32, 0.0, a)
    t = _pair_matmul(x, l2)
    yield
    x = x - _pair_matmul(t, x)
    return x


def _gdn_kernel(proj_ref, small_ref, gp_ref, nw_ref, s0_ref, o_ref, sout_ref, s_scr):
    rows = MIX_STEP_CHUNKS * CHUNK

    @pl.when(pl.program_id(1) == 0)
    def _():
        s_scr[...] = s0_ref[...]

    proj = proj_ref[...]
    small = small_ref[...]
    nqkv = 2 * GDN_QK + GDN_V

    lane = _iota2((rows, LANES), 1)
    gp = gp_ref[...]
    g_all = -jnp.exp(gp[0:1, :]) * _softplus(small + gp[1:2, :])
    g_all = jnp.where(lane < GDN_HEADS, g_all, 0.0)
    rrow = _iota2((rows, rows), 0)
    rcol = _iota2((rows, rows), 1)
    tril_blk = jnp.where(jnp.logical_and(rrow >= rcol, (rrow >> CHUNK_SHIFT) == (rcol >> CHUNK_SHIFT)),
                         1.0, 0.0).astype(F32)
    gc_all = _dot_hilo(tril_blk, g_all) * LOG2E
    beta_all = _sigmoid(small)

    pairs = [(j, h) for j in range(MIX_STEP_CHUNKS) for h in range(GDN_HEADS)]

    def rs(j):
        return slice(j * CHUNK, (j + 1) * CHUNK)

    def gather(arr, col0, width):
        return jnp.stack([arr[rs(j), col0 + h * width:col0 + (h + 1) * width] for j, h in pairs])

    q = gather(proj, 0, GDN_DK).astype(F32)
    k = gather(proj, GDN_QK, GDN_DK).astype(F32)
    v = gather(proj, 2 * GDN_QK, GDN_DV).astype(F32)
    beta = jnp.broadcast_to(gather(beta_all, GDN_HEADS, 1), k.shape)
    gcol = jnp.broadcast_to(gather(gc_all, 0, 1), k.shape)
    glast = jnp.stack([gc_all[(j + 1) * CHUNK - 1:(j + 1) * CHUNK, h:h + 1] for j, h in pairs])
    kb = k * beta

    ng = len(pairs)
    npair = ng // 2
    lane_p = _iota2((CHUNK, 2 * CHUNK), 1)
    row = _iota2((CHUNK, 2 * CHUNK), 0)
    col = lane_p & (CHUNK - 1)
    first = lane_p < CHUNK

    def interleave(even, odd):
        return jnp.stack([even, odd], axis=1).reshape((ng,) + even.shape[1:])

    def pad_even(t):
        t = t.astype(BF16)
        return jnp.concatenate([t, jnp.zeros_like(t)], axis=0)

    def pad_odd(t):
        t = t.astype(BF16)
        return jnp.concatenate([jnp.zeros_like(t), t], axis=0)

    gc_t = [jnp.concatenate([gc_all[rs(j), :], gc_all[rs(j), :]], axis=0).T for j in range(MIX_STEP_CHUNKS)]
    gcol_p = jnp.stack([jnp.where(first, gcol[2 * p], gcol[2 * p + 1]) for p in range(npair)])
    grow_p = jnp.stack([jnp.where(first[0:1], gc_t[pairs[2 * p][0]][pairs[2 * p][1]:pairs[2 * p][1] + 1, :],
                                  gc_t[pairs[2 * p + 1][0]][pairs[2 * p + 1][1]:pairs[2 * p + 1][1] + 1, :])
                        for p in range(npair)])
    decay = jnp.exp2(jnp.where(row >= col, gcol_p - grow_p, -jnp.inf))
    zk = jnp.zeros((CHUNK, GDN_DK), BF16)
    q16, k16, kb16 = q.astype(BF16), k.astype(BF16), kb.astype(BF16)
    lhs = jnp.stack([jnp.concatenate([jnp.concatenate([q16[2 * p], kb16[2 * p]], axis=0),
                                      jnp.concatenate([q16[2 * p + 1], kb16[2 * p + 1]], axis=0)], axis=1)
                     for p in range(npair)])
    k_diag = jnp.stack([jnp.concatenate([jnp.concatenate([k16[2 * p], zk], axis=1),
                                         jnp.concatenate([zk, k16[2 * p + 1]], axis=1)], axis=0)
                        for p in range(npair)])
    qk_kk = _bmm_nt(lhs, k_diag)
    yield
    qk = qk_kk[:, :CHUNK] * decay
    a_low = jnp.where(row > col, qk_kk[:, CHUNK:] * decay, 0.0)
    tinv = yield from _unit_lower_inverse(a_low, row, col)
    egc = jnp.exp2(gcol)
    rhs = jnp.concatenate([v * beta, kb * egc], axis=2)
    uw = interleave(_bmm(tinv, jnp.stack([pad_even(rhs[2 * p]) for p in range(npair)])),
                    _bmm(tinv, jnp.stack([pad_odd(rhs[2 * p + 1]) for p in range(npair)])))
    yield
    kd = k * jnp.exp2(glast - gcol)
    kd_t = jnp.stack([kd[g].T for g in range(ng)])
    kd_uw = _bmm(kd_t, uw)
    yield
    qk_uw = interleave(_bmm(qk, jnp.stack([pad_even(uw[2 * p]) for p in range(npair)])),
                       _bmm(qk, jnp.stack([pad_odd(uw[2 * p + 1]) for p in range(npair)])))
    q_eff = q * egc - qk_uw[:, :, GDN_DV:]
    eg_last = jnp.exp2(glast)
    yield

    nw = nw_ref[...]
    s = s_scr[...]
    for j in range(MIX_STEP_CHUNKS):
        gs = slice(j * GDN_HEADS, (j + 1) * GDN_HEADS)
        sb = s.astype(BF16)
        o = _bmm(q_eff[gs], sb) + qk_uw[gs, :, :GDN_DV]
        s = s * eg_last[gs] - _bmm(kd_uw[gs, :, GDN_DV:], sb) + kd_uw[gs, :, :GDN_DV]
        on = o * lax.rsqrt(jnp.mean(o * o, axis=-1, keepdims=True) + NORM_EPS) * nw
        for h in range(GDN_HEADS):
            zg = proj[rs(j), nqkv + h * GDN_DV:nqkv + (h + 1) * GDN_DV].astype(F32)
            o_ref[rs(j), h * GDN_DV:(h + 1) * GDN_DV] = (on[h] * zg).astype(o_ref.dtype)
        yield
    s_scr[...] = s
    sout_ref[...] = s


def _gla_level(q, k, b, m):
    g = q.shape[0]
    ref = jnp.concatenate(
        [jnp.broadcast_to(b[:, t + m - 1:t + m, :], (g, 2 * m, GLA_DK)) for t in range(0, CHUNK, 2 * m)], axis=1)
    p = _bmm_nt(q * jnp.exp2(jnp.minimum(b - ref, 0.0)), k * jnp.exp2(jnp.minimum(ref - b, 0.0)))
    row = _iota2((CHUNK, CHUNK), 0)
    col = _iota2((CHUNK, CHUNK), 1)
    blk = 2 * m
    shift = blk.bit_length() - 1
    keep = jnp.logical_and((row >> shift) == (col >> shift),
                           jnp.logical_and((row & (blk - 1)) >= m, (col & (blk - 1)) < m))
    return jnp.where(keep, p, 0.0)


def _gla_kernel(q_ref, k_ref, v_ref, r_ref, small_ref, w2_ref, gb_ref, nw_ref, sel_ref, s0_ref,
                o_ref, sout_ref, s_scr):
    rows = MIX_STEP_CHUNKS * CHUNK

    @pl.when(pl.program_id(1) == 0)
    def _():
        s_scr[...] = s0_ref[...]

    q_all = q_ref[...].astype(F32)
    k_all = k_ref[...].astype(F32)
    v_all = v_ref[...].astype(F32)
    r_all = r_ref[...].astype(F32)
    small = small_ref[...]

    gate = _dot(small, w2_ref[...]) + gb_ref[...]
    log_a = -_softplus(-gate) * (1.0 / GLA_GATE_NORMALIZER)
    rrow = _iota2((rows, rows), 0)
    rcol = _iota2((rows, rows), 1)
    tril_blk = jnp.where(jnp.logical_and(rrow >= rcol, (rrow >> CHUNK_SHIFT) == (rcol >> CHUNK_SHIFT)),
                         1.0, 0.0).astype(F32)
    b_all = _dot_hilo(tril_blk, log_a) * LOG2E

    pairs = [(j, h) for j in range(MIX_STEP_CHUNKS) for h in range(GLA_HEADS)]
    ng = len(pairs)

    def rs(j):
        return slice(j * CHUNK, (j + 1) * CHUNK)

    def gather(arr, width):
        return jnp.stack([arr[rs(j), h * width:(h + 1) * width] for j, h in pairs])

    q = gather(q_all, GLA_DK)
    k = gather(k_all, GLA_DK)
    b = gather(b_all, GLA_DK)
    v = gather(v_all, GLA_DV)

    yield
    scores = _gla_level(q, k, b, 32)
    yield
    scores = scores + _gla_level(q, k, b, 16)
    yield
    scores = scores + _gla_level(q, k, b, SUBLANES)
    yield

    diag = []
    nb8 = SUBLANES
    for blk in range(CHUNK // nb8):
        bs = b[:, blk * nb8:(blk + 1) * nb8, :]
        qs = q[:, blk * nb8:(blk + 1) * nb8, :]
        ks = k[:, blk * nb8:(blk + 1) * nb8, :]
        pj = [qs * (ks[:, j:j + 1, :] * jnp.exp2(jnp.minimum(bs - bs[:, j:j + 1, :], 0.0))) for j in range(nb8)]
        p = jnp.concatenate(pj, axis=2).astype(BF16).reshape(ng * nb8, nb8 * GLA_DK)
        d = jnp.dot(p, sel_ref[blk], preferred_element_type=F32)
        diag.append(d.reshape(ng, nb8, CHUNK))
        yield
    row = _iota2((CHUNK, CHUNK), 0)
    col = _iota2((CHUNK, CHUNK), 1)
    scores = scores + jnp.where(row >= col, jnp.concatenate(diag, axis=1), 0.0)

    blast = b[:, CHUNK - 1:CHUNK, :]
    qe = q * jnp.exp2(b)
    kdec = k * jnp.exp2(blast - b)
    v_t = jnp.stack([v[g].T for g in range(ng)])
    kv = _bmm(v_t, kdec)
    o_intra = _bmm(scores, v)
    eb_last = jnp.exp2(blast)
    yield

    nw = nw_ref[...]
    st = s_scr[...]
    for j in range(MIX_STEP_CHUNKS):
        gs = slice(j * GLA_HEADS, (j + 1) * GLA_HEADS)
        o = _bmm_nt(qe[gs], st) + o_intra[gs]
        st = st * eb_last[gs] + kv[gs]
        on = o * lax.rsqrt(jnp.mean(o * o, axis=-1, keepdims=True) + NORM_EPS) * nw
        for h in range(GLA_HEADS):
            rg = r_all[rs(j), h * GLA_DV:(h + 1) * GLA_DV]
            o_ref[rs(j), h * GLA_DV:(h + 1) * GLA_DV] = (on[h] * rg).astype(o_ref.dtype)
        yield
    s_scr[...] = st
    sout_ref[...] = st


def _mixer_kernel(proj_ref, small_ref, gp_ref, gnw_ref, w2_ref, gb_ref, lnw_ref, sel_ref, gs0_ref, ls0_ref,
                  o_ref, gsout_ref, lsout_ref, gdn_s, gla_s):
    gdn_cols = 2 * GDN_QK + 2 * GDN_V
    q0, k0, v0, r0 = gdn_cols, gdn_cols + GLA_QK, gdn_cols + 2 * GLA_QK, gdn_cols + 2 * GLA_QK + GLA_V
    streams = [
        _gdn_kernel(proj_ref.at[:, :gdn_cols], small_ref, gp_ref, gnw_ref, gs0_ref,
                    o_ref.at[:, :GDN_V], gsout_ref, gdn_s),
        _gla_kernel(proj_ref.at[:, q0:k0], proj_ref.at[:, k0:v0], proj_ref.at[:, v0:r0],
                    proj_ref.at[:, r0:r0 + GLA_V], small_ref, w2_ref, gb_ref, lnw_ref, sel_ref, ls0_ref,
                    o_ref.at[:, GDN_V:], lsout_ref, gla_s),
    ]
    while streams:
        for stream in list(streams):
            if next(stream, StopIteration) is StopIteration:
                streams.remove(stream)


def _mixer(proj, small, gparams, gdn_nw, w2_pad, gate_b, gla_nw, sel, gdn_s0, gla_s0, bsz, nstep):
    rows, ncol = proj.shape
    step_rows = MIX_STEP_CHUNKS * CHUNK

    def rmap(b, c):
        return (b * nstep + c, 0)

    def const(*shape):
        return pl.BlockSpec(shape, lambda b, c: (0,) * len(shape))

    return pl.pallas_call(
        _mixer_kernel,
        grid=(bsz, nstep),
        in_specs=[
            pl.BlockSpec((step_rows, ncol), rmap),
            pl.BlockSpec((step_rows, LANES), rmap),
            const(SUBLANES, LANES),
            const(1, GDN_DV),
            const(LANES, GLA_QK),
            const(1, GLA_QK),
            const(1, GLA_DV),
            const(CHUNK // SUBLANES, SUBLANES * GLA_DK, CHUNK),
            const(GDN_HEADS, GDN_DK, GDN_DV),
            const(GLA_HEADS, GLA_DV, GLA_DK),
        ],
        out_specs=[
            pl.BlockSpec((step_rows, GDN_V + GLA_V), rmap),
            const(GDN_HEADS, GDN_DK, GDN_DV),
            const(GLA_HEADS, GLA_DV, GLA_DK),
        ],
        out_shape=[
            jax.ShapeDtypeStruct((rows, GDN_V + GLA_V), BF16),
            jax.ShapeDtypeStruct((GDN_HEADS, GDN_DK, GDN_DV), F32),
            jax.ShapeDtypeStruct((GLA_HEADS, GLA_DV, GLA_DK), F32),
        ],
        scratch_shapes=[pltpu.VMEM((GDN_HEADS, GDN_DK, GDN_DV), F32),
                        pltpu.VMEM((GLA_HEADS, GLA_DV, GLA_DK), F32)],
        compiler_params=pltpu.CompilerParams(
            dimension_semantics=("arbitrary", "arbitrary"), vmem_limit_bytes=VMEM_LIMIT),
        name="mixer",
    )(proj, small, gparams, gdn_nw, w2_pad, gate_b, gla_nw, sel, gdn_s0, gla_s0)


def _outproj_kernel(x_ref, m_ref, w_ref, o_ref):
    o_ref[...] = x_ref[...] + jnp.dot(m_ref[...], w_ref[...], preferred_element_type=F32)


def _outproj(x2d, mixed, wo, tm):
    rows, d = x2d.shape
    width = mixed.shape[1]
    return pl.pallas_call(
        _outproj_kernel,
        grid=(rows // tm,),
        in_specs=[
            pl.BlockSpec((tm, d), lambda i: (i, 0)),
            pl.BlockSpec((tm, width), lambda i: (i, 0)),
            pl.BlockSpec((width, d), lambda i: (0, 0), pipeline_mode=pl.Buffered(1)),
        ],
        out_specs=pl.BlockSpec((tm, d), lambda i: (i, 0)),
        out_shape=jax.ShapeDtypeStruct((rows, d), F32),
        compiler_params=pltpu.CompilerParams(
            dimension_semantics=("arbitrary",), vmem_limit_bytes=VMEM_LIMIT),
        name="outproj",
    )(x2d, mixed, wo)


def _ffn_kernel(h_ref, nw_ref, wg_ref, wu_ref, wd_ref, fw_ref, o_ref, n_scr):
    f = pl.program_id(1)

    @pl.when(f == 0)
    def _():
        h = h_ref[...]
        ms = jnp.mean(h * h, axis=-1, keepdims=True)
        n_scr[...] = ((h * lax.rsqrt(ms + NORM_EPS)) * nw_ref[...]).astype(BF16)
        o_ref[...] = h

    n = n_scr[...]
    g = jnp.dot(n, wg_ref[...], preferred_element_type=F32)
    u = jnp.dot(n, wu_ref[...], preferred_element_type=F32)
    act = (_silu(g) * u).astype(BF16)
    o_ref[...] += jnp.dot(act, wd_ref[...], preferred_element_type=F32)

    @pl.when(f == pl.num_programs(1) - 1)
    def _():
        y = o_ref[...]
        ms = jnp.mean(y * y, axis=-1, keepdims=True)
        o_ref[...] = (y * lax.rsqrt(ms + NORM_EPS)) * fw_ref[...]


def _ffn(h2d, norm_w, w_gate, w_up, w_down, final_w, tm, tf):
    rows, d = h2d.shape
    dff = w_gate.shape[1]
    return pl.pallas_call(
        _ffn_kernel,
        grid=(rows // tm, dff // tf),
        in_specs=[
            pl.BlockSpec((tm, d), lambda i, f: (i, 0)),
            pl.BlockSpec((1, d), lambda i, f: (0, 0)),
            pl.BlockSpec((d, tf), lambda i, f: (0, f)),
            pl.BlockSpec((d, tf), lambda i, f: (0, f)),
            pl.BlockSpec((tf, d), lambda i, f: (f, 0)),
            pl.BlockSpec((1, d), lambda i, f: (0, 0)),
        ],
        out_specs=pl.BlockSpec((tm, d), lambda i, f: (i, 0)),
        out_shape=jax.ShapeDtypeStruct((rows, d), F32),
        scratch_shapes=[pltpu.VMEM((tm, d), BF16)],
        compiler_params=pltpu.CompilerParams(
            dimension_semantics=("arbitrary", "arbitrary"), vmem_limit_bytes=VMEM_LIMIT),
        name="ffn",
    )(h2d, norm_w, w_gate, w_up, w_down, final_w)


def _pick_tile(n, pref):
    t = min(n, pref)
    while n % t:
        t //= 2
    return t


def kernel(x, meta_tokens, attn_norm_w, w_in, gdn_conv_w, gdn_a_log, gdn_dt_bias, gdn_norm_w,
           gla_gate_w2, gla_gate_b, gla_norm_w, w_out, ffn_norm_w, w_gate, w_up, w_down, final_norm_w):
    bsz, seq, d = x.shape
    step_rows = MIX_STEP_CHUNKS * CHUNK
    assert seq % step_rows == 0 and attn_norm_w.shape[0] == 1
    rows = bsz * seq
    x2d = x.reshape(rows, d)

    w_gdn = w_in[0].astype(BF16)
    off_small = 2 * GDN_QK + 2 * GDN_V
    off_gla = off_small + 2 * GDN_HEADS
    off_lr = off_gla + 2 * GLA_QK + 2 * GLA_V
    w_gla = w_gdn[:, off_gla:off_lr]
    w_small = jnp.concatenate(
        [w_gdn[:, off_small:off_gla], w_gdn[:, off_lr:],
         jnp.zeros((d, LANES - 2 * GDN_HEADS - GLA_GATE_RANK), BF16)], axis=1)
    assert off_small + w_gla.shape[1] == MAIN_COLS
    nw_attn = attn_norm_w[0].reshape(1, d)
    gparams = jnp.zeros((SUBLANES, LANES), F32)
    gparams = gparams.at[0, :GDN_HEADS].set(gdn_a_log[0]).at[1, :GDN_HEADS].set(gdn_dt_bias[0])
    w2_pad = jnp.zeros((LANES, GLA_QK), F32).at[2 * GDN_HEADS:2 * GDN_HEADS + GLA_GATE_RANK].set(gla_gate_w2[0])

    h_pre = jnp.concatenate([jnp.zeros((step_rows - N_META, d), x.dtype), meta_tokens.astype(x.dtype)], axis=0)

    kk = jnp.arange(SUBLANES * GLA_DK) // GLA_DK
    sel = (kk[None, :, None] + SUBLANES * jnp.arange(CHUNK // SUBLANES)[:, None, None]
           == jnp.arange(CHUNK)[None, None, :]).astype(BF16)

    conv_w = gdn_conv_w[0]
    mixer_params = (gparams, gdn_norm_w[0].reshape(1, GDN_DV), w2_pad.astype(BF16),
                    gla_gate_b[0].reshape(1, GLA_QK), gla_norm_w[0].reshape(1, GLA_DV), sel)
    proj_pre, small_pre, carry = _inproj(h_pre, nw_attn, w_gdn, w_gla, w_small, conv_w,
                                         jnp.zeros((SUBLANES, conv_w.shape[1]), F32), step_rows, 256, 1)
    _, gdn_s0, gla_s0 = _mixer(proj_pre, small_pre, *mixer_params,
                               jnp.zeros((GDN_HEADS, GDN_DK, GDN_DV), F32),
                               jnp.zeros((GLA_HEADS, GLA_DV, GLA_DK), F32), 1, 1)

    tm_in = _pick_tile(seq, 512)
    proj_main, small_main, _ = _inproj(x2d, nw_attn, w_gdn, w_gla, w_small, conv_w, carry, tm_in, 256,
                                       seq // tm_in)
    mixed, _, _ = _mixer(proj_main, small_main, *mixer_params, gdn_s0, gla_s0, bsz, seq // step_rows)

    h2d = _outproj(x2d, mixed, w_out[0].astype(BF16), _pick_tile(rows, 1024))
    out = _ffn(h2d, ffn_norm_w[0].reshape(1, d), w_gate[0].astype(BF16), w_up[0].astype(BF16),
               w_down[0].astype(BF16), final_norm_w.reshape(1, d), _pick_tile(rows, 1024), 512)
    return out.reshape(bsz, seq, d)
```

```python
import functools

import jax
import jax.numpy as jnp
from jax import lax
from jax.experimental import pallas as pl
from jax.experimental.pallas import tpu as pltpu

F32 = jnp.float32
BF16 = jnp.bfloat16

N_META = 16
CONV_K = 4
GDN_HEADS = 8
GDN_DK = 128
GDN_DV = 128
GLA_HEADS = 4
GLA_DK = 128
GLA_DV = 256
GLA_GATE_RANK = 16
GLA_GATE_NORMALIZER = 16.0
GDN_QK = GDN_HEADS * GDN_DK
GDN_V = GDN_HEADS * GDN_DV
GLA_QK = GLA_HEADS * GLA_DK
GLA_V = GLA_HEADS * GLA_DV
NORM_EPS = 1e-6

CHUNK = 64
MIX_STEP_CHUNKS = 8
LANES = 128
SUBLANES = 8
MAIN_COLS = 2 * GDN_QK + 2 * GDN_V + 2 * GLA_QK + 2 * GLA_V
VMEM_LIMIT = 56 * 1024 * 1024
LOG2E = 1.4426950408889634


def _dot(a, b):
    return jnp.dot(a.astype(BF16), b.astype(BF16), preferred_element_type=F32)


def _chunk_cumsum(x):
    rows, n = x.shape
    nchunks = rows // CHUNK
    tril = jnp.where(_iota2((CHUNK, CHUNK), 0) >= _iota2((CHUNK, CHUNK), 1), 1.0, 0.0).astype(BF16)
    tril = jnp.broadcast_to(tril, (nchunks, CHUNK, CHUNK))
    xh = x.astype(BF16)
    xl = (x - xh.astype(F32)).astype(BF16)
    out = (_bmm(tril, xh.reshape(nchunks, CHUNK, n)) + _bmm(tril, xl.reshape(nchunks, CHUNK, n)))
    return out.reshape(rows, n)


def _bmm(a, b):
    return jnp.einsum("gij,gjk->gik", a.astype(BF16), b.astype(BF16), preferred_element_type=F32)


def _bmm_nt(a, b):
    return jnp.einsum("gik,gjk->gij", a.astype(BF16), b.astype(BF16), preferred_element_type=F32)


def _sigmoid(x):
    return 1.0 / (1.0 + jnp.exp2(x * (-LOG2E)))


def _silu(x):
    return x * _sigmoid(x)


def _softplus(x):
    return jnp.maximum(x, 0.0) + jnp.log(1.0 + jnp.exp(-jnp.abs(x)))


def _iota2(shape, dim):
    return lax.broadcasted_iota(jnp.int32, shape, dim)


def _l2norm_heads(y, scale):
    outs = []
    for h in range(y.shape[1] // LANES):
        yh = y[:, h * LANES:(h + 1) * LANES]
        outs.append(yh * (lax.rsqrt(jnp.sum(yh * yh, axis=-1, keepdims=True) + NORM_EPS) * scale))
    return jnp.concatenate(outs, axis=1)


def _inproj_kernel(x_ref, nw_ref, wa_ref, wb_ref, ws_ref, cw_ref, carry_ref, o_ref, os_ref, tail_ref, tail_scr,
                   acc_scr, *, tn, steps_per_seq):
    i = pl.program_id(0)
    tm = x_ref.shape[0]

    @pl.when(i % steps_per_seq == 0)
    def _():
        tail_scr[...] = carry_ref[...]

    x = x_ref[...]
    ms = jnp.mean(x * x, axis=-1, keepdims=True)
    nb = ((x * lax.rsqrt(ms + NORM_EPS)) * nw_ref[...]).astype(BF16)
    os_ref[...] = jnp.dot(nb, ws_ref[...], preferred_element_type=F32)
    nqkv = 2 * GDN_QK + GDN_V
    gla_q0 = nqkv + GDN_V
    gla_r0 = gla_q0 + 2 * GLA_QK + GLA_V
    na = wa_ref.shape[1]
    ntile = (na + wb_ref.shape[1]) // tn

    hist = SUBLANES

    def matmul_tile(j):
        cs = slice(j * tn, (j + 1) * tn)
        w_tile = wa_ref[:, cs] if j * tn < na else wb_ref[:, j * tn - na:(j + 1) * tn - na]
        acc_scr[hist:hist + tm, :] = jnp.dot(nb, w_tile, preferred_element_type=F32)
        if j * tn < nqkv:
            acc_scr[0:hist, :] = tail_scr[:, cs]
            tail_scr[:, cs] = acc_scr[tm:tm + hist, :]

    def epilogue_tile(j):
        c0 = j * tn
        cs = slice(c0, c0 + tn)
        acc = acc_scr[hist:hist + tm, :]
        if c0 < nqkv:
            y = acc * cw_ref[CONV_K - 1:CONV_K, cs]
            for t in range(1, CONV_K):
                y = y + acc_scr[hist - t:hist - t + tm, :] * cw_ref[CONV_K - 1 - t:CONV_K - t, cs]
            y = _silu(y)
            if c0 < GDN_QK:
                y = _l2norm_heads(y, GDN_DK ** -0.5)
            elif c0 < 2 * GDN_QK:
                y = _l2norm_heads(y, 1.0)
        elif c0 < gla_q0 or c0 >= gla_r0:
            y = _silu(acc)
        elif c0 < gla_q0 + GLA_QK:
            y = acc * (GLA_DK ** -0.5)
        else:
            y = acc
        o_ref[:, cs] = y.astype(o_ref.dtype)

    for j in range(ntile):
        matmul_tile(j)
        epilogue_tile(j)
    tail_ref[...] = tail_scr[...]


def _inproj(x2d, norm_w, w_gdn, w_gla, w_small, conv_w, carry, tm, tn, steps_per_seq):
    rows, d = x2d.shape
    gdn_cols = 2 * GDN_QK + 2 * GDN_V
    ncol = gdn_cols + w_gla.shape[1]
    nqkv = 2 * GDN_QK + GDN_V
    assert GDN_QK % tn == 0 and GDN_V % tn == 0 and GLA_QK % tn == 0 and tn % LANES == 0
    resident = pl.Buffered(1)
    return pl.pallas_call(
        functools.partial(_inproj_kernel, tn=tn, steps_per_seq=steps_per_seq),
        grid=(rows // tm,),
        in_specs=[
            pl.BlockSpec((tm, d), lambda i: (i, 0)),
            pl.BlockSpec((1, d), lambda i: (0, 0), pipeline_mode=resident),
            pl.BlockSpec((d, gdn_cols), lambda i: (0, 0), pipeline_mode=resident),
            pl.BlockSpec((d, ncol - gdn_cols), lambda i: (0, 0), pipeline_mode=resident),
            pl.BlockSpec((d, LANES), lambda i: (0, 0), pipeline_mode=resident),
            pl.BlockSpec((CONV_K, nqkv), lambda i: (0, 0), pipeline_mode=resident),
            pl.BlockSpec((SUBLANES, nqkv), lambda i: (0, 0), pipeline_mode=resident),
        ],
        out_specs=[
            pl.BlockSpec((tm, ncol), lambda i: (i, 0)),
            pl.BlockSpec((tm, LANES), lambda i: (i, 0)),
            pl.BlockSpec((SUBLANES, nqkv), lambda i: (0, 0)),
        ],
        out_shape=[
            jax.ShapeDtypeStruct((rows, ncol), BF16),
            jax.ShapeDtypeStruct((rows, LANES), F32),
            jax.ShapeDtypeStruct((SUBLANES, nqkv), F32),
        ],
        scratch_shapes=[pltpu.VMEM((SUBLANES, nqkv), F32), pltpu.VMEM((SUBLANES + tm, tn), F32)],
        compiler_params=pltpu.CompilerParams(
            dimension_semantics=("arbitrary",), vmem_limit_bytes=VMEM_LIMIT),
        name="inproj",
    )(x2d, norm_w, w_gdn, w_gla, w_small, conv_w, carry)


def _pair_matmul(x, p):
    pb = p.astype(BF16)
    lane = _iota2((CHUNK, 2 * CHUNK), 1)
    zero = jnp.zeros_like(pb)
    blockdiag = jnp.concatenate([jnp.where(lane < CHUNK, pb, zero), jnp.where(lane >= CHUNK, pb, zero)], axis=1)
    return _bmm(x, blockdiag)


def _unit_lower_inverse(a, row, col):
    same16 = (row >> 4) == (col >> 4)
    same32 = (row >> 5) == (col >> 5)
    eye = jnp.where(row == col, 1.0, 0.0).astype(F32)
    a0 = jnp.where(same16, a, 0.0)
    x = eye - a0
    p = _pair_matmul(a0, a0)
    yield
    x = x + _pair_matmul(x, p)
    p = _pair_matmul(p, p)
    yield
    x = x + _pair_matmul(x, p)
    p = _pair_matmul(p, p)
    yield
    x = x + _pair_matmul(x, p)
    yield
    l1 = jnp.where(jnp.logical_and(same32, jnp.logical_not(same16)), a, 0.0)
    t = _pair_matmul(x, l1)
    yield
    x = x - _pair_matmul(t, x)
    yield
    l2 = jnp.where(same32, 0.0, a)
    t = _pair_matmul(x, l2)
    yield
    x = x - _pair_matmul(t, x)
    return x


def _gdn_kernel(proj_ref, small_ref, gp_ref, nw_ref, s0_ref, o_ref, sout_ref, s_scr):
    rows = proj_ref.shape[0]
    nchunks = rows // CHUNK

    @pl.when(pl.program_id(1) == 0)
    def _():
        s_scr[...] = s0_ref[...]

    proj = proj_ref[...]
    small = small_ref[...]
    nqkv = 2 * GDN_QK + GDN_V

    lane = _iota2((rows, LANES), 1)
    gp = gp_ref[...]
    g_all = -jnp.exp(gp[0:1, :]) * _softplus(small + gp[1:2, :])
    g_all = jnp.where(lane < GDN_HEADS, g_all, 0.0)
    gc_all = _chunk_cumsum(g_all) * LOG2E
    beta_all = _sigmoid(small)

    pairs = [(j, h) for j in range(nchunks) for h in range(GDN_HEADS)]

    def rs(j):
        return slice(j * CHUNK, (j + 1) * CHUNK)

    def gather(arr, col0, width):
        return jnp.stack([arr[rs(j), col0 + h * width:col0 + (h + 1) * width] for j, h in pairs])

    q = gather(proj, 0, GDN_DK).astype(F32)
    k = gather(proj, GDN_QK, GDN_DK).astype(F32)
    v = gather(proj, 2 * GDN_QK, GDN_DV).astype(F32)
    beta = jnp.broadcast_to(gather(beta_all, GDN_HEADS, 1), k.shape)
    gcol = jnp.broadcast_to(gather(gc_all, 0, 1), k.shape)
    glast = jnp.stack([gc_all[(j + 1) * CHUNK - 1:(j + 1) * CHUNK, h:h + 1] for j, h in pairs])
    kb = k * beta

    ng = len(pairs)
    npair = ng // 2
    lane_p = _iota2((CHUNK, 2 * CHUNK), 1)
    row = _iota2((CHUNK, 2 * CHUNK), 0)
    col = lane_p & (CHUNK - 1)
    first = lane_p < CHUNK

    def interleave(even, odd):
        return jnp.stack([even, odd], axis=1).reshape((ng,) + even.shape[1:])

    def pad_even(t):
        t = t.astype(BF16)
        return jnp.concatenate([t, jnp.zeros_like(t)], axis=0)

    def pad_odd(t):
        t = t.astype(BF16)
        return jnp.concatenate([jnp.zeros_like(t), t], axis=0)

    gc_t = [jnp.concatenate([gc_all[rs(j), :], gc_all[rs(j), :]], axis=0).T for j in range(nchunks)]
    gcol_p = jnp.stack([jnp.where(first, gcol[2 * p], gcol[2 * p + 1]) for p in range(npair)])
    grow_p = jnp.stack([jnp.where(first[0:1], gc_t[pairs[2 * p][0]][pairs[2 * p][1]:pairs[2 * p][1] + 1, :],
                                  gc_t[pairs[2 * p + 1][0]][pairs[2 * p + 1][1]:pairs[2 * p + 1][1] + 1, :])
                        for p in range(npair)])
    decay = jnp.exp2(jnp.where(row >= col, gcol_p - grow_p, -jnp.inf))
    zk = jnp.zeros((CHUNK, GDN_DK), BF16)
    q16, k16, kb16 = q.astype(BF16), k.astype(BF16), kb.astype(BF16)
    lhs = jnp.stack([jnp.concatenate([jnp.concatenate([q16[2 * p], kb16[2 * p]], axis=0),
                                      jnp.concatenate([q16[2 * p + 1], kb16[2 * p + 1]], axis=0)], axis=1)
                     for p in range(npair)])
    k_diag = jnp.stack([jnp.concatenate([jnp.concatenate([k16[2 * p], zk], axis=1),
                                         jnp.concatenate([zk, k16[2 * p + 1]], axis=1)], axis=0)
                        for p in range(npair)])
    qk_kk = _bmm_nt(lhs, k_diag)
    yield
    qk = qk_kk[:, :CHUNK] * decay
    a_low = jnp.where(row > col, qk_kk[:, CHUNK:] * decay, 0.0)
    tinv = yield from _unit_lower_inverse(a_low, row, col)
    egc = jnp.exp2(gcol)
    rhs = jnp.concatenate([v * beta, kb * egc], axis=2)
    uw = interleave(_bmm(tinv, jnp.stack([pad_even(rhs[2 * p]) for p in range(npair)])),
                    _bmm(tinv, jnp.stack([pad_odd(rhs[2 * p + 1]) for p in range(npair)])))
    yield
    kd = k * jnp.exp2(glast - gcol)
    kd_t = jnp.stack([kd[g].T for g in range(ng)])
    kd_uw = _bmm(kd_t, uw)
    yield
    qk_uw = interleave(_bmm(qk, jnp.stack([pad_even(uw[2 * p]) for p in range(npair)])),
                       _bmm(qk, jnp.stack([pad_odd(uw[2 * p + 1]) for p in range(npair)])))
    q_eff = q * egc - qk_uw[:, :, GDN_DV:]
    eg_last = jnp.exp2(glast)
    yield

    nw = nw_ref[...]
    s = s_scr[...]
    for j in range(nchunks):
        gs = slice(j * GDN_HEADS, (j + 1) * GDN_HEADS)
        sb = s.astype(BF16)
        o = _bmm(q_eff[gs], sb) + qk_uw[gs, :, :GDN_DV]
        s = s * eg_last[gs] - _bmm(kd_uw[gs, :, GDN_DV:], sb) + kd_uw[gs, :, :GDN_DV]
        on = o * lax.rsqrt(jnp.mean(o * o, axis=-1, keepdims=True) + NORM_EPS) * nw
        for h in range(GDN_HEADS):
            zg = proj[rs(j), nqkv + h * GDN_DV:nqkv + (h + 1) * GDN_DV].astype(F32)
            o_ref[rs(j), h * GDN_DV:(h + 1) * GDN_DV] = (on[h] * zg).astype(o_ref.dtype)
        yield
    s_scr[...] = s
    sout_ref[...] = s


def _gla_level(q, k, b, m):
    g = q.shape[0]
    ref = jnp.concatenate(
        [jnp.broadcast_to(b[:, t + m - 1:t + m, :], (g, 2 * m, GLA_DK)) for t in range(0, CHUNK, 2 * m)], axis=1)
    p = _bmm_nt(q * jnp.exp2(jnp.minimum(b - ref, 0.0)), k * jnp.exp2(jnp.minimum(ref - b, 0.0)))
    row = _iota2((CHUNK, CHUNK), 0)
    col = _iota2((CHUNK, CHUNK), 1)
    blk = 2 * m
    shift = blk.bit_length() - 1
    keep = jnp.logical_and((row >> shift) == (col >> shift),
                           jnp.logical_and((row & (blk - 1)) >= m, (col & (blk - 1)) < m))
    return jnp.where(keep, p, 0.0)


def _gla_kernel(q_ref, k_ref, v_ref, r_ref, small_ref, w2_ref, gb_ref, nw_ref, sel_ref, s0_ref,
                o_ref, sout_ref, s_scr):
    rows = q_ref.shape[0]
    nchunks = rows // CHUNK

    @pl.when(pl.program_id(1) == 0)
    def _():
        s_scr[...] = s0_ref[...]

    q_all = q_ref[...].astype(F32)
    k_all = k_ref[...].astype(F32)
    v_all = v_ref[...].astype(F32)
    r_all = r_ref[...].astype(F32)
    small = small_ref[...]

    gate = _dot(small, w2_ref[...]) + gb_ref[...]
    log_a = -_softplus(-gate) * (1.0 / GLA_GATE_NORMALIZER)
    b_all = _chunk_cumsum(log_a) * LOG2E

    pairs = [(j, h) for j in range(nchunks) for h in range(GLA_HEADS)]
    ng = len(pairs)

    def rs(j):
        return slice(j * CHUNK, (j + 1) * CHUNK)

    def gather(arr, width):
        return jnp.stack([arr[rs(j), h * width:(h + 1) * width] for j, h in pairs])

    q = gather(q_all, GLA_DK)
    k = gather(k_all, GLA_DK)
    b = gather(b_all, GLA_DK)
    v = gather(v_all, GLA_DV)

    yield
    scores = _gla_level(q, k, b, 32)
    yield
    scores = scores + _gla_level(q, k, b, 16)
    yield
    scores = scores + _gla_level(q, k, b, SUBLANES)
    yield

    diag = []
    nb8 = SUBLANES
    for blk in range(CHUNK // nb8):
        bs = b[:, blk * nb8:(blk + 1) * nb8, :]
        qs = q[:, blk * nb8:(blk + 1) * nb8, :]
        ks = k[:, blk * nb8:(blk + 1) * nb8, :]
        pj = [qs * (ks[:, j:j + 1, :] * jnp.exp2(jnp.minimum(bs - bs[:, j:j + 1, :], 0.0))) for j in range(nb8)]
        p = jnp.concatenate(pj, axis=2).astype(BF16).reshape(ng * nb8, nb8 * GLA_DK)
        d = jnp.dot(p, sel_ref[blk], preferred_element_type=F32)
        diag.append(d.reshape(ng, nb8, CHUNK))
        yield
    row = _iota2((CHUNK, CHUNK), 0)
    col = _iota2((CHUNK, CHUNK), 1)
    scores = scores + jnp.where(row >= col, jnp.concatenate(diag, axis=1), 0.0)

    blast = b[:, CHUNK - 1:CHUNK, :]
    qe = q * jnp.exp2(b)
    kdec = k * jnp.exp2(blast - b)
    v_t = jnp.stack([v[g].T for g in range(ng)])
    kv = _bmm(v_t, kdec)
    o_intra = _bmm(scores, v)
    eb_last = jnp.exp2(blast)
    yield

    nw = nw_ref[...]
    st = s_scr[...]
    for j in range(nchunks):
        gs = slice(j * GLA_HEADS, (j + 1) * GLA_HEADS)
        o = _bmm_nt(qe[gs], st) + o_intra[gs]
        st = st * eb_last[gs] + kv[gs]
        on = o * lax.rsqrt(jnp.mean(o * o, axis=-1, keepdims=True) + NORM_EPS) * nw
        for h in range(GLA_HEADS):
            rg = r_all[rs(j), h * GLA_DV:(h + 1) * GLA_DV]
            o_ref[rs(j), h * GLA_DV:(h + 1) * GLA_DV] = (on[h] * rg).astype(o_ref.dtype)
        yield
    s_scr[...] = st
    sout_ref[...] = st


def _mixer_kernel(proj_ref, small_ref, gp_ref, gnw_ref, w2_ref, gb_ref, lnw_ref, sel_ref, gs0_ref, ls0_ref,
                  o_ref, gsout_ref, lsout_ref, gdn_s, gla_s):
    gdn_cols = 2 * GDN_QK + 2 * GDN_V
    q0, k0, v0, r0 = gdn_cols, gdn_cols + GLA_QK, gdn_cols + 2 * GLA_QK, gdn_cols + 2 * GLA_QK + GLA_V
    streams = [
        _gdn_kernel(proj_ref.at[:, :gdn_cols], small_ref, gp_ref, gnw_ref, gs0_ref,
                    o_ref.at[:, :GDN_V], gsout_ref, gdn_s),
        _gla_kernel(proj_ref.at[:, q0:k0], proj_ref.at[:, k0:v0], proj_ref.at[:, v0:r0],
                    proj_ref.at[:, r0:r0 + GLA_V], small_ref, w2_ref, gb_ref, lnw_ref, sel_ref, ls0_ref,
                    o_ref.at[:, GDN_V:], lsout_ref, gla_s),
    ]
    while streams:
        for stream in list(streams):
            if next(stream, StopIteration) is StopIteration:
                streams.remove(stream)


def _mixer(proj, small, gparams, gdn_nw, w2_pad, gate_b, gla_nw, sel, gdn_s0, gla_s0, bsz, step_rows):
    rows, ncol = proj.shape
    nstep = rows // (bsz * step_rows)

    def rmap(b, c):
        return (b * nstep + c, 0)

    def const(*shape):
        return pl.BlockSpec(shape, lambda b, c: (0,) * len(shape))

    return pl.pallas_call(
        _mixer_kernel,
        grid=(bsz, nstep),
        in_specs=[
            pl.BlockSpec((step_rows, ncol), rmap),
            pl.BlockSpec((step_rows, LANES), rmap),
            const(SUBLANES, LANES),
            const(1, GDN_DV),
            const(LANES, GLA_QK),
            const(1, GLA_QK),
            const(1, GLA_DV),
            const(CHUNK // SUBLANES, SUBLANES * GLA_DK, CHUNK),
            const(GDN_HEADS, GDN_DK, GDN_DV),
            const(GLA_HEADS, GLA_DV, GLA_DK),
        ],
        out_specs=[
            pl.BlockSpec((step_rows, GDN_V + GLA_V), rmap),
            const(GDN_HEADS, GDN_DK, GDN_DV),
            const(GLA_HEADS, GLA_DV, GLA_DK),
        ],
        out_shape=[
            jax.ShapeDtypeStruct((rows, GDN_V + GLA_V), BF16),
            jax.ShapeDtypeStruct((GDN_HEADS, GDN_DK, GDN_DV), F32),
            jax.ShapeDtypeStruct((GLA_HEADS, GLA_DV, GLA_DK), F32),
        ],
        scratch_shapes=[pltpu.VMEM((GDN_HEADS, GDN_DK, GDN_DV), F32),
                        pltpu.VMEM((GLA_HEADS, GLA_DV, GLA_DK), F32)],
        compiler_params=pltpu.CompilerParams(
            dimension_semantics=("arbitrary", "arbitrary"), vmem_limit_bytes=VMEM_LIMIT),
        name="mixer",
    )(proj, small, gparams, gdn_nw, w2_pad, gate_b, gla_nw, sel, gdn_s0, gla_s0)


def _outproj_kernel(x_ref, m_ref, w_ref, o_ref):
    o_ref[...] = x_ref[...] + jnp.dot(m_ref[...], w_ref[...], preferred_element_type=F32)


def _outproj(x2d, mixed, wo, tm):
    rows, d = x2d.shape
    width = mixed.shape[1]
    return pl.pallas_call(
        _outproj_kernel,
        grid=(rows // tm,),
        in_specs=[
            pl.BlockSpec((tm, d), lambda i: (i, 0)),
            pl.BlockSpec((tm, width), lambda i: (i, 0)),
            pl.BlockSpec((width, d), lambda i: (0, 0), pipeline_mode=pl.Buffered(1)),
        ],
        out_specs=pl.BlockSpec((tm, d), lambda i: (i, 0)),
        out_shape=jax.ShapeDtypeStruct((rows, d), F32),
        compiler_params=pltpu.CompilerParams(
            dimension_semantics=("arbitrary",), vmem_limit_bytes=VMEM_LIMIT),
        name="outproj",
    )(x2d, mixed, wo)


def _ffn_kernel(h_ref, nw_ref, wg_ref, wu_ref, wd_ref, fw_ref, o_ref, n_scr):
    f = pl.program_id(1)

    @pl.when(f == 0)
    def _():
        h = h_ref[...]
        ms = jnp.mean(h * h, axis=-1, keepdims=True)
        n_scr[...] = ((h * lax.rsqrt(ms + NORM_EPS)) * nw_ref[...]).astype(BF16)
        o_ref[...] = h

    n = n_scr[...]
    g = jnp.dot(n, wg_ref[...], preferred_element_type=F32)
    u = jnp.dot(n, wu_ref[...], preferred_element_type=F32)
    act = (_silu(g) * u).astype(BF16)
    o_ref[...] += jnp.dot(act, wd_ref[...], preferred_element_type=F32)

    @pl.when(f == pl.num_programs(1) - 1)
    def _():
        y = o_ref[...]
        ms = jnp.mean(y * y, axis=-1, keepdims=True)
        o_ref[...] = (y * lax.rsqrt(ms + NORM_EPS)) * fw_ref[...]


def _ffn(h2d, norm_w, w_gate, w_up, w_down, final_w, tm, tf):
    rows, d = h2d.shape
    dff = w_gate.shape[1]
    return pl.pallas_call(
        _ffn_kernel,
        grid=(rows // tm, dff // tf),
        in_specs=[
            pl.BlockSpec((tm, d), lambda i, f: (i, 0)),
            pl.BlockSpec((1, d), lambda i, f: (0, 0)),
            pl.BlockSpec((d, tf), lambda i, f: (0, f)),
            pl.BlockSpec((d, tf), lambda i, f: (0, f)),
            pl.BlockSpec((tf, d), lambda i, f: (f, 0)),
            pl.BlockSpec((1, d), lambda i, f: (0, 0)),
        ],
        out_specs=pl.BlockSpec((tm, d), lambda i, f: (i, 0)),
        out_shape=jax.ShapeDtypeStruct((rows, d), F32),
        scratch_shapes=[pltpu.VMEM((tm, d), BF16)],
        compiler_params=pltpu.CompilerParams(
            dimension_semantics=("arbitrary", "arbitrary"), vmem_limit_bytes=VMEM_LIMIT),
        name="ffn",
    )(h2d, norm_w, w_gate, w_up, w_down, final_w)


def _pick_tile(n, pref):
    t = min(n, pref)
    while n % t:
        t //= 2
    return t


def kernel(x, meta_tokens, attn_norm_w, w_in, gdn_conv_w, gdn_a_log, gdn_dt_bias, gdn_norm_w,
           gla_gate_w2, gla_gate_b, gla_norm_w, w_out, ffn_norm_w, w_gate, w_up, w_down, final_norm_w):
    bsz, seq, d = x.shape
    step_rows = MIX_STEP_CHUNKS * CHUNK
    assert seq % step_rows == 0 and attn_norm_w.shape[0] == 1
    rows = bsz * seq
    x2d = x.reshape(rows, d)

    w_gdn = w_in[0].astype(BF16)
    off_small = 2 * GDN_QK + 2 * GDN_V
    off_gla = off_small + 2 * GDN_HEADS
    off_lr = off_gla + 2 * GLA_QK + 2 * GLA_V
    w_gla = w_gdn[:, off_gla:off_lr]
    w_small = jnp.concatenate(
        [w_gdn[:, off_small:off_gla], w_gdn[:, off_lr:],
         jnp.zeros((d, LANES - 2 * GDN_HEADS - GLA_GATE_RANK), BF16)], axis=1)
    assert off_small + w_gla.shape[1] == MAIN_COLS
    nw_attn = attn_norm_w[0].reshape(1, d)
    gparams = jnp.zeros((SUBLANES, LANES), F32)
    gparams = gparams.at[0, :GDN_HEADS].set(gdn_a_log[0]).at[1, :GDN_HEADS].set(gdn_dt_bias[0])
    w2_pad = jnp.zeros((LANES, GLA_QK), F32).at[2 * GDN_HEADS:2 * GDN_HEADS + GLA_GATE_RANK].set(gla_gate_w2[0])

    h_pre = jnp.concatenate([jnp.zeros((CHUNK - N_META, d), x.dtype), meta_tokens.astype(x.dtype)], axis=0)

    kk = jnp.arange(SUBLANES * GLA_DK) // GLA_DK
    sel = (kk[None, :, None] + SUBLANES * jnp.arange(CHUNK // SUBLANES)[:, None, None]
           == jnp.arange(CHUNK)[None, None, :]).astype(BF16)

    conv_w = gdn_conv_w[0]
    mixer_params = (gparams, gdn_norm_w[0].reshape(1, GDN_DV), w2_pad.astype(BF16),
                    gla_gate_b[0].reshape(1, GLA_QK), gla_norm_w[0].reshape(1, GLA_DV), sel)
    proj_pre, small_pre, carry = _inproj(h_pre, nw_attn, w_gdn, w_gla, w_small, conv_w,
                                         jnp.zeros((SUBLANES, conv_w.shape[1]), F32), CHUNK, 256, 1)
    _, gdn_s0, gla_s0 = _mixer(proj_pre, small_pre, *mixer_params,
                               jnp.zeros((GDN_HEADS, GDN_DK, GDN_DV), F32),
                               jnp.zeros((GLA_HEADS, GLA_DV, GLA_DK), F32), 1, CHUNK)

    tm_in = _pick_tile(seq, 512)
    proj_main, small_main, _ = _inproj(x2d, nw_attn, w_gdn, w_gla, w_small, conv_w, carry, tm_in, 256,
                                       seq // tm_in)
    mixed, _, _ = _mixer(proj_main, small_main, *mixer_params, gdn_s0, gla_s0, bsz, step_rows)

    h2d = _outproj(x2d, mixed, w_out[0].astype(BF16), _pick_tile(rows, 1024))
    out = _ffn(h2d, ffn_norm_w[0].reshape(1, d), w_gate[0].astype(BF16), w_up[0].astype(BF16),
               w_down[0].astype(BF16), final_norm_w.reshape(1, d), _pick_tile(rows, 1024), 512)
    return out.reshape(bsz, seq, d)
```

```python
import functools

import jax
import jax.numpy as jnp
from jax import lax
from jax.experimental import pallas as pl
from jax.experimental.pallas import tpu as pltpu

F32 = jnp.float32
BF16 = jnp.bfloat16

N_META = 16
CONV_K = 4
GDN_HEADS = 8
GDN_DK = 128
GDN_DV = 128
GLA_HEADS = 4
GLA_DK = 128
GLA_DV = 256
GLA_GATE_RANK = 16
GLA_GATE_NORMALIZER = 16.0
GDN_QK = GDN_HEADS * GDN_DK
GDN_V = GDN_HEADS * GDN_DV
GLA_QK = GLA_HEADS * GLA_DK
GLA_V = GLA_HEADS * GLA_DV
NORM_EPS = 1e-6

CHUNK = 64
MIX_STEP_CHUNKS = 4
LANES = 128
SUBLANES = 8
MAIN_COLS = 2 * GDN_QK + 2 * GDN_V + 2 * GLA_QK + 2 * GLA_V
VMEM_LIMIT = 56 * 1024 * 1024
LOG2E = 1.4426950408889634


def _dot(a, b):
    return jnp.dot(a.astype(BF16), b.astype(BF16), preferred_element_type=F32)


def _chunk_cumsum(x):
    rows, n = x.shape
    nchunks = rows // CHUNK
    tril = jnp.where(_iota2((CHUNK, CHUNK), 0) >= _iota2((CHUNK, CHUNK), 1), 1.0, 0.0).astype(BF16)
    tril = jnp.broadcast_to(tril, (nchunks, CHUNK, CHUNK))
    xh = x.astype(BF16)
    xl = (x - xh.astype(F32)).astype(BF16)
    out = (_bmm(tril, xh.reshape(nchunks, CHUNK, n)) + _bmm(tril, xl.reshape(nchunks, CHUNK, n)))
    return out.reshape(rows, n)


def _bmm(a, b):
    return jnp.einsum("gij,gjk->gik", a.astype(BF16), b.astype(BF16), preferred_element_type=F32)


def _bmm_nt(a, b):
    return jnp.einsum("gik,gjk->gij", a.astype(BF16), b.astype(BF16), preferred_element_type=F32)


def _sigmoid(x):
    return 1.0 / (1.0 + jnp.exp2(x * (-LOG2E)))


def _silu(x):
    return x * _sigmoid(x)


def _softplus(x):
    return jnp.maximum(x, 0.0) + jnp.log(1.0 + jnp.exp(-jnp.abs(x)))


def _iota2(shape, dim):
    return lax.broadcasted_iota(jnp.int32, shape, dim)


def _l2norm_heads(y, scale):
    outs = []
    for h in range(y.shape[1] // LANES):
        yh = y[:, h * LANES:(h + 1) * LANES]
        outs.append(yh * (lax.rsqrt(jnp.sum(yh * yh, axis=-1, keepdims=True) + NORM_EPS) * scale))
    return jnp.concatenate(outs, axis=1)


def _inproj_kernel(x_ref, nw_ref, wa_ref, wb_ref, ws_ref, cw_ref, carry_ref, o_ref, os_ref, tail_ref, tail_scr,
                   acc_scr, *, tn, steps_per_seq):
    i = pl.program_id(0)
    tm = x_ref.shape[0]

    @pl.when(i % steps_per_seq == 0)
    def _():
        tail_scr[...] = carry_ref[...]

    x = x_ref[...]
    ms = jnp.mean(x * x, axis=-1, keepdims=True)
    nb = ((x * lax.rsqrt(ms + NORM_EPS)) * nw_ref[...]).astype(BF16)
    os_ref[...] = jnp.dot(nb, ws_ref[...], preferred_element_type=F32)
    nqkv = 2 * GDN_QK + GDN_V
    gla_q0 = nqkv + GDN_V
    gla_r0 = gla_q0 + 2 * GLA_QK + GLA_V
    na = wa_ref.shape[1]
    ntile = (na + wb_ref.shape[1]) // tn

    hist = SUBLANES

    def matmul_tile(j):
        cs = slice(j * tn, (j + 1) * tn)
        w_tile = wa_ref[:, cs] if j * tn < na else wb_ref[:, j * tn - na:(j + 1) * tn - na]
        acc_scr[hist:hist + tm, :] = jnp.dot(nb, w_tile, preferred_element_type=F32)
        if j * tn < nqkv:
            acc_scr[0:hist, :] = tail_scr[:, cs]
            tail_scr[:, cs] = acc_scr[tm:tm + hist, :]

    def epilogue_tile(j):
        c0 = j * tn
        cs = slice(c0, c0 + tn)
        acc = acc_scr[hist:hist + tm, :]
        if c0 < nqkv:
            y = acc * cw_ref[CONV_K - 1:CONV_K, cs]
            for t in range(1, CONV_K):
                y = y + acc_scr[hist - t:hist - t + tm, :] * cw_ref[CONV_K - 1 - t:CONV_K - t, cs]
            y = _silu(y)
            if c0 < GDN_QK:
                y = _l2norm_heads(y, GDN_DK ** -0.5)
            elif c0 < 2 * GDN_QK:
                y = _l2norm_heads(y, 1.0)
        elif c0 < gla_q0 or c0 >= gla_r0:
            y = _silu(acc)
        elif c0 < gla_q0 + GLA_QK:
            y = acc * (GLA_DK ** -0.5)
        else:
            y = acc
        o_ref[:, cs] = y.astype(o_ref.dtype)

    for j in range(ntile):
        matmul_tile(j)
        epilogue_tile(j)
    tail_ref[...] = tail_scr[...]


def _inproj(x2d, norm_w, w_gdn, w_gla, w_small, conv_w, carry, tm, tn, steps_per_seq):
    rows, d = x2d.shape
    gdn_cols = 2 * GDN_QK + 2 * GDN_V
    ncol = gdn_cols + w_gla.shape[1]
    nqkv = 2 * GDN_QK + GDN_V
    assert GDN_QK % tn == 0 and GDN_V % tn == 0 and GLA_QK % tn == 0 and tn % LANES == 0
    resident = pl.Buffered(1)
    return pl.pallas_call(
        functools.partial(_inproj_kernel, tn=tn, steps_per_seq=steps_per_seq),
        grid=(rows // tm,),
        in_specs=[
            pl.BlockSpec((tm, d), lambda i: (i, 0)),
            pl.BlockSpec((1, d), lambda i: (0, 0), pipeline_mode=resident),
            pl.BlockSpec((d, gdn_cols), lambda i: (0, 0), pipeline_mode=resident),
            pl.BlockSpec((d, ncol - gdn_cols), lambda i: (0, 0), pipeline_mode=resident),
            pl.BlockSpec((d, LANES), lambda i: (0, 0), pipeline_mode=resident),
            pl.BlockSpec((CONV_K, nqkv), lambda i: (0, 0), pipeline_mode=resident),
            pl.BlockSpec((SUBLANES, nqkv), lambda i: (0, 0), pipeline_mode=resident),
        ],
        out_specs=[
            pl.BlockSpec((tm, ncol), lambda i: (i, 0)),
            pl.BlockSpec((tm, LANES), lambda i: (i, 0)),
            pl.BlockSpec((SUBLANES, nqkv), lambda i: (0, 0)),
        ],
        out_shape=[
            jax.ShapeDtypeStruct((rows, ncol), BF16),
            jax.ShapeDtypeStruct((rows, LANES), F32),
            jax.ShapeDtypeStruct((SUBLANES, nqkv), F32),
        ],
        scratch_shapes=[pltpu.VMEM((SUBLANES, nqkv), F32), pltpu.VMEM((SUBLANES + tm, tn), F32)],
        compiler_params=pltpu.CompilerParams(
            dimension_semantics=("arbitrary",), vmem_limit_bytes=VMEM_LIMIT),
        name="inproj",
    )(x2d, norm_w, w_gdn, w_gla, w_small, conv_w, carry)


def _pair_matmul(x, p):
    pb = p.astype(BF16)
    lane = _iota2((CHUNK, 2 * CHUNK), 1)
    zero = jnp.zeros_like(pb)
    blockdiag = jnp.concatenate([jnp.where(lane < CHUNK, pb, zero), jnp.where(lane >= CHUNK, pb, zero)], axis=1)
    return _bmm(x, blockdiag)


def _unit_lower_inverse(a, row, col):
    same16 = (row >> 4) == (col >> 4)
    same32 = (row >> 5) == (col >> 5)
    eye = jnp.where(row == col, 1.0, 0.0).astype(F32)
    a0 = jnp.where(same16, a, 0.0)
    x = eye - a0
    p = _pair_matmul(a0, a0)
    yield
    x = x + _pair_matmul(x, p)
    p = _pair_matmul(p, p)
    yield
    x = x + _pair_matmul(x, p)
    p = _pair_matmul(p, p)
    yield
    x = x + _pair_matmul(x, p)
    yield
    l1 = jnp.where(jnp.logical_and(same32, jnp.logical_not(same16)), a, 0.0)
    t = _pair_matmul(x, l1)
    yield
    x = x - _pair_matmul(t, x)
    yield
    l2 = jnp.where(same32, 0.0, a)
    t = _pair_matmul(x, l2)
    yield
    x = x - _pair_matmul(t, x)
    return x


def _gdn_kernel(is_first, proj_ref, small_ref, gp_ref, nw_ref, s0_ref, o_ref, sout_ref, s_scr):
    rows = proj_ref.shape[0]
    nchunks = rows // CHUNK

    @pl.when(is_first)
    def _():
        s_scr[...] = s0_ref[...]

    proj = proj_ref[...]
    small = small_ref[...]
    nqkv = 2 * GDN_QK + GDN_V

    lane = _iota2((rows, LANES), 1)
    gp = gp_ref[...]
    g_all = -jnp.exp(gp[0:1, :]) * _softplus(small + gp[1:2, :])
    g_all = jnp.where(lane < GDN_HEADS, g_all, 0.0)
    gc_all = _chunk_cumsum(g_all) * LOG2E
    beta_all = _sigmoid(small)

    pairs = [(j, h) for j in range(nchunks) for h in range(GDN_HEADS)]

    def rs(j):
        return slice(j * CHUNK, (j + 1) * CHUNK)

    def gather(arr, col0, width):
        return jnp.stack([arr[rs(j), col0 + h * width:col0 + (h + 1) * width] for j, h in pairs])

    q = gather(proj, 0, GDN_DK).astype(F32)
    k = gather(proj, GDN_QK, GDN_DK).astype(F32)
    v = gather(proj, 2 * GDN_QK, GDN_DV).astype(F32)
    beta = jnp.broadcast_to(gather(beta_all, GDN_HEADS, 1), k.shape)
    gcol = jnp.broadcast_to(gather(gc_all, 0, 1), k.shape)
    glast = jnp.stack([gc_all[(j + 1) * CHUNK - 1:(j + 1) * CHUNK, h:h + 1] for j, h in pairs])
    kb = k * beta

    ng = len(pairs)
    npair = ng // 2
    lane_p = _iota2((CHUNK, 2 * CHUNK), 1)
    row = _iota2((CHUNK, 2 * CHUNK), 0)
    col = lane_p & (CHUNK - 1)
    first = lane_p < CHUNK

    def interleave(even, odd):
        return jnp.stack([even, odd], axis=1).reshape((ng,) + even.shape[1:])

    def pad_even(t):
        t = t.astype(BF16)
        return jnp.concatenate([t, jnp.zeros_like(t)], axis=0)

    def pad_odd(t):
        t = t.astype(BF16)
        return jnp.concatenate([jnp.zeros_like(t), t], axis=0)

    gc_t = [jnp.concatenate([gc_all[rs(j), :], gc_all[rs(j), :]], axis=0).T for j in range(nchunks)]
    gcol_p = jnp.stack([jnp.where(first, gcol[2 * p], gcol[2 * p + 1]) for p in range(npair)])
    grow_p = jnp.stack([jnp.where(first[0:1], gc_t[pairs[2 * p][0]][pairs[2 * p][1]:pairs[2 * p][1] + 1, :],
                                  gc_t[pairs[2 * p + 1][0]][pairs[2 * p + 1][1]:pairs[2 * p + 1][1] + 1, :])
                        for p in range(npair)])
    decay = jnp.exp2(jnp.where(row >= col, gcol_p - grow_p, -jnp.inf))
    zk = jnp.zeros((CHUNK, GDN_DK), BF16)
    q16, k16, kb16 = q.astype(BF16), k.astype(BF16), kb.astype(BF16)
    lhs = jnp.stack([jnp.concatenate([jnp.concatenate([q16[2 * p], kb16[2 * p]], axis=0),
                                      jnp.concatenate([q16[2 * p + 1], kb16[2 * p + 1]], axis=0)], axis=1)
                     for p in range(npair)])
    k_diag = jnp.stack([jnp.concatenate([jnp.concatenate([k16[2 * p], zk], axis=1),
                                         jnp.concatenate([zk, k16[2 * p + 1]], axis=1)], axis=0)
                        for p in range(npair)])
    qk_kk = _bmm_nt(lhs, k_diag)
    yield
    qk = qk_kk[:, :CHUNK] * decay
    a_low = jnp.where(row > col, qk_kk[:, CHUNK:] * decay, 0.0)
    tinv = yield from _unit_lower_inverse(a_low, row, col)
    egc = jnp.exp2(gcol)
    rhs = jnp.concatenate([v * beta, kb * egc], axis=2)
    uw = interleave(_bmm(tinv, jnp.stack([pad_even(rhs[2 * p]) for p in range(npair)])),
                    _bmm(tinv, jnp.stack([pad_odd(rhs[2 * p + 1]) for p in range(npair)])))
    yield
    kd = k * jnp.exp2(glast - gcol)
    kd_t = jnp.stack([kd[g].T for g in range(ng)])
    kd_uw = _bmm(kd_t, uw)
    yield
    qk_uw = interleave(_bmm(qk, jnp.stack([pad_even(uw[2 * p]) for p in range(npair)])),
                       _bmm(qk, jnp.stack([pad_odd(uw[2 * p + 1]) for p in range(npair)])))
    q_eff = q * egc - qk_uw[:, :, GDN_DV:]
    eg_last = jnp.exp2(glast)
    yield

    nw = nw_ref[...]
    s = s_scr[...]
    for j in range(nchunks):
        gs = slice(j * GDN_HEADS, (j + 1) * GDN_HEADS)
        sb = s.astype(BF16)
        o = _bmm(q_eff[gs], sb) + qk_uw[gs, :, :GDN_DV]
        s = s * eg_last[gs] - _bmm(kd_uw[gs, :, GDN_DV:], sb) + kd_uw[gs, :, :GDN_DV]
        on = o * lax.rsqrt(jnp.mean(o * o, axis=-1, keepdims=True) + NORM_EPS) * nw
        for h in range(GDN_HEADS):
            zg = proj[rs(j), nqkv + h * GDN_DV:nqkv + (h + 1) * GDN_DV].astype(F32)
            o_ref[rs(j), h * GDN_DV:(h + 1) * GDN_DV] = (on[h] * zg).astype(o_ref.dtype)
        yield
    s_scr[...] = s
    sout_ref[...] = s


def _gla_level(q, k, b, m):
    g = q.shape[0]
    ref = jnp.concatenate(
        [jnp.broadcast_to(b[:, t + m - 1:t + m, :], (g, 2 * m, GLA_DK)) for t in range(0, CHUNK, 2 * m)], axis=1)
    p = _bmm_nt(q * jnp.exp2(jnp.minimum(b - ref, 0.0)), k * jnp.exp2(jnp.minimum(ref - b, 0.0)))
    row = _iota2((CHUNK, CHUNK), 0)
    col = _iota2((CHUNK, CHUNK), 1)
    blk = 2 * m
    shift = blk.bit_length() - 1
    keep = jnp.logical_and((row >> shift) == (col >> shift),
                           jnp.logical_and((row & (blk - 1)) >= m, (col & (blk - 1)) < m))
    return jnp.where(keep, p, 0.0)


def _gla_kernel(is_first, q_ref, k_ref, v_ref, r_ref, small_ref, w2_ref, gb_ref, nw_ref, sel_ref, s0_ref,
                o_ref, sout_ref, s_scr):
    rows = q_ref.shape[0]
    nchunks = rows // CHUNK

    @pl.when(is_first)
    def _():
        s_scr[...] = s0_ref[...]

    q_all = q_ref[...].astype(F32)
    k_all = k_ref[...].astype(F32)
    v_all = v_ref[...].astype(F32)
    r_all = r_ref[...].astype(F32)
    small = small_ref[...]

    gate = _dot(small, w2_ref[...]) + gb_ref[...]
    log_a = -_softplus(-gate) * (1.0 / GLA_GATE_NORMALIZER)
    b_all = _chunk_cumsum(log_a) * LOG2E

    pairs = [(j, h) for j in range(nchunks) for h in range(GLA_HEADS)]
    ng = len(pairs)

    def rs(j):
        return slice(j * CHUNK, (j + 1) * CHUNK)

    def gather(arr, width):
        return jnp.stack([arr[rs(j), h * width:(h + 1) * width] for j, h in pairs])

    q = gather(q_all, GLA_DK)
    k = gather(k_all, GLA_DK)
    b = gather(b_all, GLA_DK)
    v = gather(v_all, GLA_DV)

    yield
    scores = _gla_level(q, k, b, 32)
    yield
    scores = scores + _gla_level(q, k, b, 16)
    yield
    scores = scores + _gla_level(q, k, b, SUBLANES)
    yield

    diag = []
    nb8 = SUBLANES
    for blk in range(CHUNK // nb8):
        bs = b[:, blk * nb8:(blk + 1) * nb8, :]
        qs = q[:, blk * nb8:(blk + 1) * nb8, :]
        ks = k[:, blk * nb8:(blk + 1) * nb8, :]
        pj = [qs * (ks[:, j:j + 1, :] * jnp.exp2(jnp.minimum(bs - bs[:, j:j + 1, :], 0.0))) for j in range(nb8)]
        p = jnp.concatenate(pj, axis=2).astype(BF16).reshape(ng * nb8, nb8 * GLA_DK)
        d = jnp.dot(p, sel_ref[blk], preferred_element_type=F32)
        diag.append(d.reshape(ng, nb8, CHUNK))
        yield
    row = _iota2((CHUNK, CHUNK), 0)
    col = _iota2((CHUNK, CHUNK), 1)
    scores = scores + jnp.where(row >= col, jnp.concatenate(diag, axis=1), 0.0)

    blast = b[:, CHUNK - 1:CHUNK, :]
    qe = q * jnp.exp2(b)
    kdec = k * jnp.exp2(blast - b)
    v_t = jnp.stack([v[g].T for g in range(ng)])
    kv = _bmm(v_t, kdec)
    o_intra = _bmm(scores, v)
    eb_last = jnp.exp2(blast)
    yield

    nw = nw_ref[...]
    st = s_scr[...]
    for j in range(nchunks):
        gs = slice(j * GLA_HEADS, (j + 1) * GLA_HEADS)
        o = _bmm_nt(qe[gs], st) + o_intra[gs]
        st = st * eb_last[gs] + kv[gs]
        on = o * lax.rsqrt(jnp.mean(o * o, axis=-1, keepdims=True) + NORM_EPS) * nw
        for h in range(GLA_HEADS):
            rg = r_all[rs(j), h * GLA_DV:(h + 1) * GLA_DV]
            o_ref[rs(j), h * GLA_DV:(h + 1) * GLA_DV] = (on[h] * rg).astype(o_ref.dtype)
        yield
    s_scr[...] = st
    sout_ref[...] = st


def _outproj_stream(x_ref, mixed_ref, w_ref, h_ref, tn):
    m = mixed_ref[...]
    for j in range(w_ref.shape[1] // tn):
        cs = slice(j * tn, (j + 1) * tn)
        h_ref[:, cs] = x_ref[:, cs] + jnp.dot(m, w_ref[:, cs], preferred_element_type=F32)
        yield


def _mixer_kernel(proj_ref, small_ref, gp_ref, gnw_ref, w2_ref, gb_ref, lnw_ref, sel_ref, gs0_ref, ls0_ref,
                  *rest, nstep, fused):
    s = pl.program_id(0)
    is_first = lax.rem(s, nstep) == 0
    if fused:
        x_ref, wo_ref, h_ref, gsout_ref, lsout_ref, gdn_s, gla_s, mixed_scr = rest

        @pl.when(s == 0)
        def _():
            mixed_scr[...] = jnp.zeros_like(mixed_scr)

        o_ref = mixed_scr
    else:
        o_ref, gsout_ref, lsout_ref, gdn_s, gla_s = rest
    gdn_cols = 2 * GDN_QK + 2 * GDN_V
    q0, k0, v0, r0 = gdn_cols, gdn_cols + GLA_QK, gdn_cols + 2 * GLA_QK, gdn_cols + 2 * GLA_QK + GLA_V
    streams = []
    if fused:
        streams.append(_outproj_stream(x_ref, mixed_scr, wo_ref, h_ref, 2 * LANES))
    streams += [
        _gdn_kernel(is_first, proj_ref.at[:, :gdn_cols], small_ref, gp_ref, gnw_ref, gs0_ref,
                    o_ref.at[:, :GDN_V], gsout_ref, gdn_s),
        _gla_kernel(is_first, proj_ref.at[:, q0:k0], proj_ref.at[:, k0:v0], proj_ref.at[:, v0:r0],
                    proj_ref.at[:, r0:r0 + GLA_V], small_ref, w2_ref, gb_ref, lnw_ref, sel_ref, ls0_ref,
                    o_ref.at[:, GDN_V:], lsout_ref, gla_s),
    ]
    while streams:
        for stream in list(streams):
            if next(stream, StopIteration) is StopIteration:
                streams.remove(stream)


def _mixer(proj, small, gparams, gdn_nw, w2_pad, gate_b, gla_nw, sel, gdn_s0, gla_s0, bsz, step_rows,
           x2d=None, w_out=None):
    rows, ncol = proj.shape
    nstep = rows // (bsz * step_rows)
    total = bsz * nstep
    fused = x2d is not None
    width = GDN_V + GLA_V

    def cur(s):
        return (jnp.minimum(s, total - 1), 0)

    def prev(s):
        return (jnp.maximum(s - 1, 0), 0)

    def const(*shape, **kw):
        return pl.BlockSpec(shape, lambda s: (0,) * len(shape), **kw)

    in_specs = [
        pl.BlockSpec((step_rows, ncol), cur),
        pl.BlockSpec((step_rows, LANES), cur),
        const(SUBLANES, LANES),
        const(1, GDN_DV),
        const(LANES, GLA_QK),
        const(1, GLA_QK),
        const(1, GLA_DV),
        const(CHUNK // SUBLANES, SUBLANES * GLA_DK, CHUNK),
        const(GDN_HEADS, GDN_DK, GDN_DV),
        const(GLA_HEADS, GLA_DV, GLA_DK),
    ]
    operands = [proj, small, gparams, gdn_nw, w2_pad, gate_b, gla_nw, sel, gdn_s0, gla_s0]
    scratch = [pltpu.VMEM((GDN_HEADS, GDN_DK, GDN_DV), F32), pltpu.VMEM((GLA_HEADS, GLA_DV, GLA_DK), F32)]
    if fused:
        d = x2d.shape[1]
        in_specs += [pl.BlockSpec((step_rows, d), prev), const(width, d, pipeline_mode=pl.Buffered(1))]
        operands += [x2d, w_out]
        main_out = pl.BlockSpec((step_rows, d), prev)
        main_shape = jax.ShapeDtypeStruct((rows, d), F32)
        scratch.append(pltpu.VMEM((step_rows, width), BF16))
    else:
        main_out = pl.BlockSpec((step_rows, width), cur)
        main_shape = jax.ShapeDtypeStruct((rows, width), BF16)

    return pl.pallas_call(
        functools.partial(_mixer_kernel, nstep=nstep, fused=fused),
        grid=(total + (1 if fused else 0),),
        in_specs=in_specs,
        out_specs=[main_out, const(GDN_HEADS, GDN_DK, GDN_DV), const(GLA_HEADS, GLA_DV, GLA_DK)],
        out_shape=[
            main_shape,
            jax.ShapeDtypeStruct((GDN_HEADS, GDN_DK, GDN_DV), F32),
            jax.ShapeDtypeStruct((GLA_HEADS, GLA_DV, GLA_DK), F32),
        ],
        scratch_shapes=scratch,
        compiler_params=pltpu.CompilerParams(
            dimension_semantics=("arbitrary",), vmem_limit_bytes=VMEM_LIMIT),
        name="mixer",
    )(*operands)


def _outproj_kernel(x_ref, m_ref, w_ref, o_ref):
    o_ref[...] = x_ref[...] + jnp.dot(m_ref[...], w_ref[...], preferred_element_type=F32)


def _outproj(x2d, mixed, wo, tm):
    rows, d = x2d.shape
    width = mixed.shape[1]
    return pl.pallas_call(
        _outproj_kernel,
        grid=(rows // tm,),
        in_specs=[
            pl.BlockSpec((tm, d), lambda i: (i, 0)),
            pl.BlockSpec((tm, width), lambda i: (i, 0)),
            pl.BlockSpec((width, d), lambda i: (0, 0), pipeline_mode=pl.Buffered(1)),
        ],
        out_specs=pl.BlockSpec((tm, d), lambda i: (i, 0)),
        out_shape=jax.ShapeDtypeStruct((rows, d), F32),
        compiler_params=pltpu.CompilerParams(
            dimension_semantics=("arbitrary",), vmem_limit_bytes=VMEM_LIMIT),
        name="outproj",
    )(x2d, mixed, wo)


def _ffn_kernel(h_ref, nw_ref, wg_ref, wu_ref, wd_ref, fw_ref, o_ref, n_scr):
    f = pl.program_id(1)

    @pl.when(f == 0)
    def _():
        h = h_ref[...]
        ms = jnp.mean(h * h, axis=-1, keepdims=True)
        n_scr[...] = ((h * lax.rsqrt(ms + NORM_EPS)) * nw_ref[...]).astype(BF16)
        o_ref[...] = h

    n = n_scr[...]
    g = jnp.dot(n, wg_ref[...], preferred_element_type=F32)
    u = jnp.dot(n, wu_ref[...], preferred_element_type=F32)
    act = (_silu(g) * u).astype(BF16)
    o_ref[...] += jnp.dot(act, wd_ref[...], preferred_element_type=F32)

    @pl.when(f == pl.num_programs(1) - 1)
    def _():
        y = o_ref[...]
        ms = jnp.mean(y * y, axis=-1, keepdims=True)
        o_ref[...] = (y * lax.rsqrt(ms + NORM_EPS)) * fw_ref[...]


def _ffn(h2d, norm_w, w_gate, w_up, w_down, final_w, tm, tf):
    rows, d = h2d.shape
    dff = w_gate.shape[1]
    return pl.pallas_call(
        _ffn_kernel,
        grid=(rows // tm, dff // tf),
        in_specs=[
            pl.BlockSpec((tm, d), lambda i, f: (i, 0)),
            pl.BlockSpec((1, d), lambda i, f: (0, 0)),
            pl.BlockSpec((d, tf), lambda i, f: (0, f)),
            pl.BlockSpec((d, tf), lambda i, f: (0, f)),
            pl.BlockSpec((tf, d), lambda i, f: (f, 0)),
            pl.BlockSpec((1, d), lambda i, f: (0, 0)),
        ],
        out_specs=pl.BlockSpec((tm, d), lambda i, f: (i, 0)),
        out_shape=jax.ShapeDtypeStruct((rows, d), F32),
        scratch_shapes=[pltpu.VMEM((tm, d), BF16)],
        compiler_params=pltpu.CompilerParams(
            dimension_semantics=("arbitrary", "arbitrary"), vmem_limit_bytes=VMEM_LIMIT),
        name="ffn",
    )(h2d, norm_w, w_gate, w_up, w_down, final_w)


def _pick_tile(n, pref):
    t = min(n, pref)
    while n % t:
        t //= 2
    return t


def kernel(x, meta_tokens, attn_norm_w, w_in, gdn_conv_w, gdn_a_log, gdn_dt_bias, gdn_norm_w,
           gla_gate_w2, gla_gate_b, gla_norm_w, w_out, ffn_norm_w, w_gate, w_up, w_down, final_norm_w):
    bsz, seq, d = x.shape
    step_rows = MIX_STEP_CHUNKS * CHUNK
    assert seq % step_rows == 0 and attn_norm_w.shape[0] == 1
    rows = bsz * seq
    x2d = x.reshape(rows, d)

    w_gdn = w_in[0].astype(BF16)
    off_small = 2 * GDN_QK + 2 * GDN_V
    off_gla = off_small + 2 * GDN_HEADS
    off_lr = off_gla + 2 * GLA_QK + 2 * GLA_V
    w_gla = w_gdn[:, off_gla:off_lr]
    w_small = jnp.concatenate(
        [w_gdn[:, off_small:off_gla], w_gdn[:, off_lr:],
         jnp.zeros((d, LANES - 2 * GDN_HEADS - GLA_GATE_RANK), BF16)], axis=1)
    assert off_small + w_gla.shape[1] == MAIN_COLS
    nw_attn = attn_norm_w[0].reshape(1, d)
    gparams = jnp.zeros((SUBLANES, LANES), F32)
    gparams = gparams.at[0, :GDN_HEADS].set(gdn_a_log[0]).at[1, :GDN_HEADS].set(gdn_dt_bias[0])
    w2_pad = jnp.zeros((LANES, GLA_QK), F32).at[2 * GDN_HEADS:2 * GDN_HEADS + GLA_GATE_RANK].set(gla_gate_w2[0])

    h_pre = jnp.concatenate([jnp.zeros((CHUNK - N_META, d), x.dtype), meta_tokens.astype(x.dtype)], axis=0)

    kk = jnp.arange(SUBLANES * GLA_DK) // GLA_DK
    sel = (kk[None, :, None] + SUBLANES * jnp.arange(CHUNK // SUBLANES)[:, None, None]
           == jnp.arange(CHUNK)[None, None, :]).astype(BF16)

    conv_w = gdn_conv_w[0]
    mixer_params = (gparams, gdn_norm_w[0].reshape(1, GDN_DV), w2_pad.astype(BF16),
                    gla_gate_b[0].reshape(1, GLA_QK), gla_norm_w[0].reshape(1, GLA_DV), sel)
    proj_pre, small_pre, carry = _inproj(h_pre, nw_attn, w_gdn, w_gla, w_small, conv_w,
                                         jnp.zeros((SUBLANES, conv_w.shape[1]), F32), CHUNK, 256, 1)
    _, gdn_s0, gla_s0 = _mixer(proj_pre, small_pre, *mixer_params,
                               jnp.zeros((GDN_HEADS, GDN_DK, GDN_DV), F32),
                               jnp.zeros((GLA_HEADS, GLA_DV, GLA_DK), F32), 1, CHUNK)

    tm_in = _pick_tile(seq, 512)
    proj_main, small_main, _ = _inproj(x2d, nw_attn, w_gdn, w_gla, w_small, conv_w, carry, tm_in, 256,
                                       seq // tm_in)
    h2d, _, _ = _mixer(proj_main, small_main, *mixer_params, gdn_s0, gla_s0, bsz, step_rows,
                       x2d=x2d, w_out=w_out[0].astype(BF16))
    out = _ffn(h2d, ffn_norm_w[0].reshape(1, d), w_gate[0].astype(BF16), w_up[0].astype(BF16),
               w_down[0].astype(BF16), final_norm_w.reshape(1, d), _pick_tile(rows, 1024), 512)
    return out.reshape(bsz, seq, d)
```

```python
import functools

import jax
import jax.numpy as jnp
from jax import lax
from jax.experimental import pallas as pl
from jax.experimental.pallas import tpu as pltpu

F32 = jnp.float32
BF16 = jnp.bfloat16

N_META = 16
CONV_K = 4
GDN_HEADS = 8
GDN_DK = 128
GDN_DV = 128
GLA_HEADS = 4
GLA_DK = 128
GLA_DV = 256
GLA_GATE_RANK = 16
GLA_GATE_NORMALIZER = 16.0
GDN_QK = GDN_HEADS * GDN_DK
GDN_V = GDN_HEADS * GDN_DV
GLA_QK = GLA_HEADS * GLA_DK
GLA_V = GLA_HEADS * GLA_DV
NORM_EPS = 1e-6

CHUNK = 64
MIX_STEP_CHUNKS = 8
LANES = 128
SUBLANES = 8
MAIN_COLS = 2 * GDN_QK + 2 * GDN_V + 2 * GLA_QK + 2 * GLA_V
VMEM_LIMIT = 56 * 1024 * 1024
LOG2E = 1.4426950408889634


def _dot(a, b):
    return jnp.dot(a.astype(BF16), b.astype(BF16), preferred_element_type=F32)


def _chunk_cumsum(x):
    rows, n = x.shape
    nchunks = rows // CHUNK
    tril = jnp.where(_iota2((CHUNK, CHUNK), 0) >= _iota2((CHUNK, CHUNK), 1), 1.0, 0.0).astype(BF16)
    tril = jnp.broadcast_to(tril, (nchunks, CHUNK, CHUNK))
    xh = x.astype(BF16)
    xl = (x - xh.astype(F32)).astype(BF16)
    out = (_bmm(tril, xh.reshape(nchunks, CHUNK, n)) + _bmm(tril, xl.reshape(nchunks, CHUNK, n)))
    return out.reshape(rows, n)


def _bmm(a, b):
    return jnp.einsum("gij,gjk->gik", a.astype(BF16), b.astype(BF16), preferred_element_type=F32)


def _bmm_nt(a, b):
    return jnp.einsum("gik,gjk->gij", a.astype(BF16), b.astype(BF16), preferred_element_type=F32)


def _sigmoid(x):
    return 1.0 / (1.0 + jnp.exp2(x * (-LOG2E)))


def _silu(x):
    return x * _sigmoid(x)


def _softplus(x):
    return jnp.maximum(x, 0.0) + jnp.log(1.0 + jnp.exp(-jnp.abs(x)))


def _iota2(shape, dim):
    return lax.broadcasted_iota(jnp.int32, shape, dim)


def _l2norm_heads(y, scale):
    outs = []
    for h in range(y.shape[1] // LANES):
        yh = y[:, h * LANES:(h + 1) * LANES]
        outs.append(yh * (lax.rsqrt(jnp.sum(yh * yh, axis=-1, keepdims=True) + NORM_EPS) * scale))
    return jnp.concatenate(outs, axis=1)


def _inproj_kernel(x_ref, nw_ref, wa_ref, wb_ref, ws_ref, cw_ref, carry_ref, o_ref, os_ref, tail_ref, tail_scr,
                   acc_scr, *, tn, steps_per_seq):
    i = pl.program_id(0)
    tm = x_ref.shape[0]

    @pl.when(i % steps_per_seq == 0)
    def _():
        tail_scr[...] = carry_ref[...]

    x = x_ref[...]
    ms = jnp.mean(x * x, axis=-1, keepdims=True)
    nb = ((x * lax.rsqrt(ms + NORM_EPS)) * nw_ref[...]).astype(BF16)
    os_ref[...] = jnp.dot(nb, ws_ref[...], preferred_element_type=F32)
    nqkv = 2 * GDN_QK + GDN_V
    gla_q0 = nqkv + GDN_V
    gla_r0 = gla_q0 + 2 * GLA_QK + GLA_V
    na = wa_ref.shape[1]
    ntile = (na + wb_ref.shape[1]) // tn

    hist = SUBLANES

    def matmul_tile(j):
        cs = slice(j * tn, (j + 1) * tn)
        w_tile = wa_ref[:, cs] if j * tn < na else wb_ref[:, j * tn - na:(j + 1) * tn - na]
        acc_scr[hist:hist + tm, :] = jnp.dot(nb, w_tile, preferred_element_type=F32)
        if j * tn < nqkv:
            acc_scr[0:hist, :] = tail_scr[:, cs]
            tail_scr[:, cs] = acc_scr[tm:tm + hist, :]

    def epilogue_tile(j):
        c0 = j * tn
        cs = slice(c0, c0 + tn)
        acc = acc_scr[hist:hist + tm, :]
        if c0 < nqkv:
            y = acc * cw_ref[CONV_K - 1:CONV_K, cs]
            for t in range(1, CONV_K):
                y = y + acc_scr[hist - t:hist - t + tm, :] * cw_ref[CONV_K - 1 - t:CONV_K - t, cs]
            y = _silu(y)
            if c0 < GDN_QK:
                y = _l2norm_heads(y, GDN_DK ** -0.5)
            elif c0 < 2 * GDN_QK:
                y = _l2norm_heads(y, 1.0)
        elif c0 < gla_q0 or c0 >= gla_r0:
            y = _silu(acc)
        elif c0 < gla_q0 + GLA_QK:
            y = acc * (GLA_DK ** -0.5)
        else:
            y = acc
        o_ref[:, cs] = y.astype(o_ref.dtype)

    for j in range(ntile):
        matmul_tile(j)
        epilogue_tile(j)
    tail_ref[...] = tail_scr[...]


def _inproj(x2d, norm_w, w_gdn, w_gla, w_small, conv_w, carry, tm, tn, steps_per_seq):
    rows, d = x2d.shape
    gdn_cols = 2 * GDN_QK + 2 * GDN_V
    ncol = gdn_cols + w_gla.shape[1]
    nqkv = 2 * GDN_QK + GDN_V
    assert GDN_QK % tn == 0 and GDN_V % tn == 0 and GLA_QK % tn == 0 and tn % LANES == 0
    resident = pl.Buffered(1)
    return pl.pallas_call(
        functools.partial(_inproj_kernel, tn=tn, steps_per_seq=steps_per_seq),
        grid=(rows // tm,),
        in_specs=[
            pl.BlockSpec((tm, d), lambda i: (i, 0)),
            pl.BlockSpec((1, d), lambda i: (0, 0), pipeline_mode=resident),
            pl.BlockSpec((d, gdn_cols), lambda i: (0, 0), pipeline_mode=resident),
            pl.BlockSpec((d, ncol - gdn_cols), lambda i: (0, 0), pipeline_mode=resident),
            pl.BlockSpec((d, LANES), lambda i: (0, 0), pipeline_mode=resident),
            pl.BlockSpec((CONV_K, nqkv), lambda i: (0, 0), pipeline_mode=resident),
            pl.BlockSpec((SUBLANES, nqkv), lambda i: (0, 0), pipeline_mode=resident),
        ],
        out_specs=[
            pl.BlockSpec((tm, ncol), lambda i: (i, 0)),
            pl.BlockSpec((tm, LANES), lambda i: (i, 0)),
            pl.BlockSpec((SUBLANES, nqkv), lambda i: (0, 0)),
        ],
        out_shape=[
            jax.ShapeDtypeStruct((rows, ncol), BF16),
            jax.ShapeDtypeStruct((rows, LANES), F32),
            jax.ShapeDtypeStruct((SUBLANES, nqkv), F32),
        ],
        scratch_shapes=[pltpu.VMEM((SUBLANES, nqkv), F32), pltpu.VMEM((SUBLANES + tm, tn), F32)],
        compiler_params=pltpu.CompilerParams(
            dimension_semantics=("arbitrary",), vmem_limit_bytes=VMEM_LIMIT),
        name="inproj",
    )(x2d, norm_w, w_gdn, w_gla, w_small, conv_w, carry)


def _pair_matmul(x, p):
    pb = p.astype(BF16)
    lane = _iota2((CHUNK, 2 * CHUNK), 1)
    zero = jnp.zeros_like(pb)
    blockdiag = jnp.concatenate([jnp.where(lane < CHUNK, pb, zero), jnp.where(lane >= CHUNK, pb, zero)], axis=1)
    return _bmm(x, blockdiag)


def _unit_lower_inverse(a, row, col):
    same16 = (row >> 4) == (col >> 4)
    same32 = (row >> 5) == (col >> 5)
    eye = jnp.where(row == col, 1.0, 0.0).astype(F32)
    a0 = jnp.where(same16, a, 0.0)
    x = eye - a0
    p = _pair_matmul(a0, a0)
    yield
    x = x + _pair_matmul(x, p)
    p = _pair_matmul(p, p)
    yield
    x = x + _pair_matmul(x, p)
    p = _pair_matmul(p, p)
    yield
    x = x + _pair_matmul(x, p)
    yield
    l1 = jnp.where(jnp.logical_and(same32, jnp.logical_not(same16)), a, 0.0)
    t = _pair_matmul(x, l1)
    yield
    x = x - _pair_matmul(t, x)
    yield
    l2 = jnp.where(same32, 0.0, a)
    t = _pair_matmul(x, l2)
    yield
    x = x - _pair_matmul(t, x)
    return x


def _gdn_kernel(proj_ref, small_ref, gp_ref, nw_ref, s0_ref, o_ref, sout_ref, s_scr):
    rows = proj_ref.shape[0]
    nchunks = rows // CHUNK

    @pl.when(pl.program_id(1) == 0)
    def _():
        s_scr[...] = s0_ref[...]

    proj = proj_ref[...]
    small = small_ref[...]
    nqkv = 2 * GDN_QK + GDN_V

    lane = _iota2((rows, LANES), 1)
    gp = gp_ref[...]
    g_all = -jnp.exp(gp[0:1, :]) * _softplus(small + gp[1:2, :])
    g_all = jnp.where(lane < GDN_HEADS, g_all, 0.0)
    gc_all = _chunk_cumsum(g_all) * LOG2E
    beta_all = _sigmoid(small)

    pairs = [(j, h) for j in range(nchunks) for h in range(GDN_HEADS)]

    def rs(j):
        return slice(j * CHUNK, (j + 1) * CHUNK)

    def gather(arr, col0, width):
        return jnp.stack([arr[rs(j), col0 + h * width:col0 + (h + 1) * width] for j, h in pairs])

    q = gather(proj, 0, GDN_DK).astype(F32)
    k = gather(proj, GDN_QK, GDN_DK).astype(F32)
    v = gather(proj, 2 * GDN_QK, GDN_DV).astype(F32)
    beta = jnp.broadcast_to(gather(beta_all, GDN_HEADS, 1), k.shape)
    gcol = jnp.broadcast_to(gather(gc_all, 0, 1), k.shape)
    glast = jnp.stack([gc_all[(j + 1) * CHUNK - 1:(j + 1) * CHUNK, h:h + 1] for j, h in pairs])
    kb = k * beta

    ng = len(pairs)
    npair = ng // 2
    lane_p = _iota2((CHUNK, 2 * CHUNK), 1)
    row = _iota2((CHUNK, 2 * CHUNK), 0)
    col = lane_p & (CHUNK - 1)
    first = lane_p < CHUNK

    def interleave(even, odd):
        return jnp.stack([even, odd], axis=1).reshape((ng,) + even.shape[1:])

    def pad_even(t):
        t = t.astype(BF16)
        return jnp.concatenate([t, jnp.zeros_like(t)], axis=0)

    def pad_odd(t):
        t = t.astype(BF16)
        return jnp.concatenate([jnp.zeros_like(t), t], axis=0)

    gc_t = [jnp.concatenate([gc_all[rs(j), :], gc_all[rs(j), :]], axis=0).T for j in range(nchunks)]
    gcol_p = jnp.stack([jnp.where(first, gcol[2 * p], gcol[2 * p + 1]) for p in range(npair)])
    grow_p = jnp.stack([jnp.where(first[0:1], gc_t[pairs[2 * p][0]][pairs[2 * p][1]:pairs[2 * p][1] + 1, :],
                                  gc_t[pairs[2 * p + 1][0]][pairs[2 * p + 1][1]:pairs[2 * p + 1][1] + 1, :])
                        for p in range(npair)])
    decay = jnp.exp2(jnp.where(row >= col, gcol_p - grow_p, -jnp.inf))
    zk = jnp.zeros((CHUNK, GDN_DK), BF16)
    q16, k16, kb16 = q.astype(BF16), k.astype(BF16), kb.astype(BF16)
    lhs = jnp.stack([jnp.concatenate([jnp.concatenate([q16[2 * p], kb16[2 * p]], axis=0),
                                      jnp.concatenate([q16[2 * p + 1], kb16[2 * p + 1]], axis=0)], axis=1)
                     for p in range(npair)])
    k_diag = jnp.stack([jnp.concatenate([jnp.concatenate([k16[2 * p], zk], axis=1),
                                         jnp.concatenate([zk, k16[2 * p + 1]], axis=1)], axis=0)
                        for p in range(npair)])
    qk_kk = _bmm_nt(lhs, k_diag)
    yield
    qk = qk_kk[:, :CHUNK] * decay
    a_low = jnp.where(row > col, qk_kk[:, CHUNK:] * decay, 0.0)
    tinv = yield from _unit_lower_inverse(a_low, row, col)
    egc = jnp.exp2(gcol)
    rhs = jnp.concatenate([v * beta, kb * egc], axis=2)
    uw = interleave(_bmm(tinv, jnp.stack([pad_even(rhs[2 * p]) for p in range(npair)])),
                    _bmm(tinv, jnp.stack([pad_odd(rhs[2 * p + 1]) for p in range(npair)])))
    yield
    kd = k * jnp.exp2(glast - gcol)
    kd_t = jnp.stack([kd[g].T for g in range(ng)])
    kd_uw = _bmm(kd_t, uw)
    yield
    qk_uw = interleave(_bmm(qk, jnp.stack([pad_even(uw[2 * p]) for p in range(npair)])),
                       _bmm(qk, jnp.stack([pad_odd(uw[2 * p + 1]) for p in range(npair)])))
    q_eff = q * egc - qk_uw[:, :, GDN_DV:]
    eg_last = jnp.exp2(glast)
    yield

    nw = nw_ref[...]
    s = s_scr[...]
    for j in range(nchunks):
        gs = slice(j * GDN_HEADS, (j + 1) * GDN_HEADS)
        sb = s.astype(BF16)
        o = _bmm(q_eff[gs], sb) + qk_uw[gs, :, :GDN_DV]
        s = s * eg_last[gs] - _bmm(kd_uw[gs, :, GDN_DV:], sb) + kd_uw[gs, :, :GDN_DV]
        on = o * lax.rsqrt(jnp.mean(o * o, axis=-1, keepdims=True) + NORM_EPS) * nw
        for h in range(GDN_HEADS):
            zg = proj[rs(j), nqkv + h * GDN_DV:nqkv + (h + 1) * GDN_DV].astype(F32)
            o_ref[rs(j), h * GDN_DV:(h + 1) * GDN_DV] = (on[h] * zg).astype(o_ref.dtype)
        yield
    s_scr[...] = s
    sout_ref[...] = s


def _gla_level(q, k, b, m):
    g = q.shape[0]
    ref = jnp.concatenate(
        [jnp.broadcast_to(b[:, t + m - 1:t + m, :], (g, 2 * m, GLA_DK)) for t in range(0, CHUNK, 2 * m)], axis=1)
    p = _bmm_nt(q * jnp.exp2(jnp.minimum(b - ref, 0.0)), k * jnp.exp2(jnp.minimum(ref - b, 0.0)))
    row = _iota2((CHUNK, CHUNK), 0)
    col = _iota2((CHUNK, CHUNK), 1)
    blk = 2 * m
    shift = blk.bit_length() - 1
    keep = jnp.logical_and((row >> shift) == (col >> shift),
                           jnp.logical_and((row & (blk - 1)) >= m, (col & (blk - 1)) < m))
    return jnp.where(keep, p, 0.0)


def _gla_kernel(q_ref, k_ref, v_ref, r_ref, small_ref, w2_ref, gb_ref, nw_ref, sel_ref, s0_ref,
                o_ref, sout_ref, s_scr):
    rows = q_ref.shape[0]
    nchunks = rows // CHUNK

    @pl.when(pl.program_id(1) == 0)
    def _():
        s_scr[...] = s0_ref[...]

    q_all = q_ref[...].astype(F32)
    k_all = k_ref[...].astype(F32)
    v_all = v_ref[...].astype(F32)
    r_all = r_ref[...].astype(F32)
    small = small_ref[...]

    gate = _dot(small, w2_ref[...]) + gb_ref[...]
    log_a = -_softplus(-gate) * (1.0 / GLA_GATE_NORMALIZER)
    b_all = _chunk_cumsum(log_a) * LOG2E

    pairs = [(j, h) for j in range(nchunks) for h in range(GLA_HEADS)]
    ng = len(pairs)

    def rs(j):
        return slice(j * CHUNK, (j + 1) * CHUNK)

    def gather(arr, width):
        return jnp.stack([arr[rs(j), h * width:(h + 1) * width] for j, h in pairs])

    q = gather(q_all, GLA_DK)
    k = gather(k_all, GLA_DK)
    b = gather(b_all, GLA_DK)
    v = gather(v_all, GLA_DV)

    yield
    scores = _gla_level(q, k, b, 32)
    yield
    scores = scores + _gla_level(q, k, b, 16)
    yield
    scores = scores + _gla_level(q, k, b, SUBLANES)
    yield

    diag = []
    nb8 = SUBLANES
    for blk in range(CHUNK // nb8):
        bs = b[:, blk * nb8:(blk + 1) * nb8, :]
        qs = q[:, blk * nb8:(blk + 1) * nb8, :]
        ks = k[:, blk * nb8:(blk + 1) * nb8, :]
        pj = [qs * (ks[:, j:j + 1, :] * jnp.exp2(jnp.minimum(bs - bs[:, j:j + 1, :], 0.0))) for j in range(nb8)]
        p = jnp.concatenate(pj, axis=2).astype(BF16).reshape(ng * nb8, nb8 * GLA_DK)
        d = jnp.dot(p, sel_ref[blk], preferred_element_type=F32)
        diag.append(d.reshape(ng, nb8, CHUNK))
        yield
    row = _iota2((CHUNK, CHUNK), 0)
    col = _iota2((CHUNK, CHUNK), 1)
    scores = scores + jnp.where(row >= col, jnp.concatenate(diag, axis=1), 0.0)

    blast = b[:, CHUNK - 1:CHUNK, :]
    qe = q * jnp.exp2(b)
    kdec = k * jnp.exp2(blast - b)
    v_t = jnp.stack([v[g].T for g in range(ng)])
    kv = _bmm(v_t, kdec)
    o_intra = _bmm(scores, v)
    eb_last = jnp.exp2(blast)
    yield

    nw = nw_ref[...]
    st = s_scr[...]
    for j in range(nchunks):
        gs = slice(j * GLA_HEADS, (j + 1) * GLA_HEADS)
        o = _bmm_nt(qe[gs], st) + o_intra[gs]
        st = st * eb_last[gs] + kv[gs]
        on = o * lax.rsqrt(jnp.mean(o * o, axis=-1, keepdims=True) + NORM_EPS) * nw
        for h in range(GLA_HEADS):
            rg = r_all[rs(j), h * GLA_DV:(h + 1) * GLA_DV]
            o_ref[rs(j), h * GLA_DV:(h + 1) * GLA_DV] = (on[h] * rg).astype(o_ref.dtype)
        yield
    s_scr[...] = st
    sout_ref[...] = st


def _mixer_kernel(proj_ref, small_ref, gp_ref, gnw_ref, w2_ref, gb_ref, lnw_ref, sel_ref, gs0_ref, ls0_ref,
                  o_ref, gsout_ref, lsout_ref, gdn_s, gla_s):
    gdn_cols = 2 * GDN_QK + 2 * GDN_V
    q0, k0, v0, r0 = gdn_cols, gdn_cols + GLA_QK, gdn_cols + 2 * GLA_QK, gdn_cols + 2 * GLA_QK + GLA_V
    streams = [
        _gdn_kernel(proj_ref.at[:, :gdn_cols], small_ref, gp_ref, gnw_ref, gs0_ref,
                    o_ref.at[:, :GDN_V], gsout_ref, gdn_s),
        _gla_kernel(proj_ref.at[:, q0:k0], proj_ref.at[:, k0:v0], proj_ref.at[:, v0:r0],
                    proj_ref.at[:, r0:r0 + GLA_V], small_ref, w2_ref, gb_ref, lnw_ref, sel_ref, ls0_ref,
                    o_ref.at[:, GDN_V:], lsout_ref, gla_s),
    ]
    while streams:
        for stream in list(streams):
            if next(stream, StopIteration) is StopIteration:
                streams.remove(stream)


def _mixer(proj, small, gparams, gdn_nw, w2_pad, gate_b, gla_nw, sel, gdn_s0, gla_s0, bsz, step_rows):
    rows, ncol = proj.shape
    nstep = rows // (bsz * step_rows)

    def rmap(b, c):
        return (b * nstep + c, 0)

    def const(*shape):
        return pl.BlockSpec(shape, lambda b, c: (0,) * len(shape))

    return pl.pallas_call(
        _mixer_kernel,
        grid=(bsz, nstep),
        in_specs=[
            pl.BlockSpec((step_rows, ncol), rmap),
            pl.BlockSpec((step_rows, LANES), rmap),
            const(SUBLANES, LANES),
            const(1, GDN_DV),
            const(LANES, GLA_QK),
            const(1, GLA_QK),
            const(1, GLA_DV),
            const(CHUNK // SUBLANES, SUBLANES * GLA_DK, CHUNK),
            const(GDN_HEADS, GDN_DK, GDN_DV),
            const(GLA_HEADS, GLA_DV, GLA_DK),
        ],
        out_specs=[
            pl.BlockSpec((step_rows, GDN_V + GLA_V), rmap),
            const(GDN_HEADS, GDN_DK, GDN_DV),
            const(GLA_HEADS, GLA_DV, GLA_DK),
        ],
        out_shape=[
            jax.ShapeDtypeStruct((rows, GDN_V + GLA_V), BF16),
            jax.ShapeDtypeStruct((GDN_HEADS, GDN_DK, GDN_DV), F32),
            jax.ShapeDtypeStruct((GLA_HEADS, GLA_DV, GLA_DK), F32),
        ],
        scratch_shapes=[pltpu.VMEM((GDN_HEADS, GDN_DK, GDN_DV), F32),
                        pltpu.VMEM((GLA_HEADS, GLA_DV, GLA_DK), F32)],
        compiler_params=pltpu.CompilerParams(
            dimension_semantics=("arbitrary", "arbitrary"), vmem_limit_bytes=VMEM_LIMIT),
        name="mixer",
    )(proj, small, gparams, gdn_nw, w2_pad, gate_b, gla_nw, sel, gdn_s0, gla_s0)


def _outproj_kernel(x_ref, m_ref, w_ref, o_ref):
    o_ref[...] = x_ref[...] + jnp.dot(m_ref[...], w_ref[...], preferred_element_type=F32)


def _outproj(x2d, mixed, wo, tm):
    rows, d = x2d.shape
    width = mixed.shape[1]
    return pl.pallas_call(
        _outproj_kernel,
        grid=(rows // tm,),
        in_specs=[
            pl.BlockSpec((tm, d), lambda i: (i, 0)),
            pl.BlockSpec((tm, width), lambda i: (i, 0)),
            pl.BlockSpec((width, d), lambda i: (0, 0), pipeline_mode=pl.Buffered(1)),
        ],
        out_specs=pl.BlockSpec((tm, d), lambda i: (i, 0)),
        out_shape=jax.ShapeDtypeStruct((rows, d), F32),
        compiler_params=pltpu.CompilerParams(
            dimension_semantics=("parallel",), vmem_limit_bytes=VMEM_LIMIT),
        name="outproj",
    )(x2d, mixed, wo)


def _ffn_kernel(h_ref, nw_ref, wg_ref, wu_ref, wd_ref, fw_ref, o_ref, n_scr):
    f = pl.program_id(1)

    @pl.when(f == 0)
    def _():
        h = h_ref[...]
        ms = jnp.mean(h * h, axis=-1, keepdims=True)
        n_scr[...] = ((h * lax.rsqrt(ms + NORM_EPS)) * nw_ref[...]).astype(BF16)
        o_ref[...] = h

    n = n_scr[...]
    g = jnp.dot(n, wg_ref[...], preferred_element_type=F32)
    u = jnp.dot(n, wu_ref[...], preferred_element_type=F32)
    act = (_silu(g) * u).astype(BF16)
    o_ref[...] += jnp.dot(act, wd_ref[...], preferred_element_type=F32)

    @pl.when(f == pl.num_programs(1) - 1)
    def _():
        y = o_ref[...]
        ms = jnp.mean(y * y, axis=-1, keepdims=True)
        o_ref[...] = (y * lax.rsqrt(ms + NORM_EPS)) * fw_ref[...]


def _ffn(h2d, norm_w, w_gate, w_up, w_down, final_w, tm, tf):
    rows, d = h2d.shape
    dff = w_gate.shape[1]
    return pl.pallas_call(
        _ffn_kernel,
        grid=(rows // tm, dff // tf),
        in_specs=[
            pl.BlockSpec((tm, d), lambda i, f: (i, 0)),
            pl.BlockSpec((1, d), lambda i, f: (0, 0)),
            pl.BlockSpec((d, tf), lambda i, f: (0, f)),
            pl.BlockSpec((d, tf), lambda i, f: (0, f)),
            pl.BlockSpec((tf, d), lambda i, f: (f, 0)),
            pl.BlockSpec((1, d), lambda i, f: (0, 0)),
        ],
        out_specs=pl.BlockSpec((tm, d), lambda i, f: (i, 0)),
        out_shape=jax.ShapeDtypeStruct((rows, d), F32),
        scratch_shapes=[pltpu.VMEM((tm, d), BF16)],
        compiler_params=pltpu.CompilerParams(
            dimension_semantics=("parallel", "arbitrary"), vmem_limit_bytes=VMEM_LIMIT),
        name="ffn",
    )(h2d, norm_w, w_gate, w_up, w_down, final_w)


def _pick_tile(n, pref):
    t = min(n, pref)
    while n % t:
        t //= 2
    return t


def kernel(x, meta_tokens, attn_norm_w, w_in, gdn_conv_w, gdn_a_log, gdn_dt_bias, gdn_norm_w,
           gla_gate_w2, gla_gate_b, gla_norm_w, w_out, ffn_norm_w, w_gate, w_up, w_down, final_norm_w):
    bsz, seq, d = x.shape
    step_rows = MIX_STEP_CHUNKS * CHUNK
    assert seq % step_rows == 0 and attn_norm_w.shape[0] == 1
    rows = bsz * seq
    x2d = x.reshape(rows, d)

    w_gdn = w_in[0].astype(BF16)
    off_small = 2 * GDN_QK + 2 * GDN_V
    off_gla = off_small + 2 * GDN_HEADS
    off_lr = off_gla + 2 * GLA_QK + 2 * GLA_V
    w_gla = w_gdn[:, off_gla:off_lr]
    w_small = jnp.concatenate(
        [w_gdn[:, off_small:off_gla], w_gdn[:, off_lr:],
         jnp.zeros((d, LANES - 2 * GDN_HEADS - GLA_GATE_RANK), BF16)], axis=1)
    assert off_small + w_gla.shape[1] == MAIN_COLS
    nw_attn = attn_norm_w[0].reshape(1, d)
    gparams = jnp.zeros((SUBLANES, LANES), F32)
    gparams = gparams.at[0, :GDN_HEADS].set(gdn_a_log[0]).at[1, :GDN_HEADS].set(gdn_dt_bias[0])
    w2_pad = jnp.zeros((LANES, GLA_QK), F32).at[2 * GDN_HEADS:2 * GDN_HEADS + GLA_GATE_RANK].set(gla_gate_w2[0])

    h_pre = jnp.concatenate([jnp.zeros((CHUNK - N_META, d), x.dtype), meta_tokens.astype(x.dtype)], axis=0)

    kk = jnp.arange(SUBLANES * GLA_DK) // GLA_DK
    sel = (kk[None, :, None] + SUBLANES * jnp.arange(CHUNK // SUBLANES)[:, None, None]
           == jnp.arange(CHUNK)[None, None, :]).astype(BF16)

    conv_w = gdn_conv_w[0]
    mixer_params = (gparams, gdn_norm_w[0].reshape(1, GDN_DV), w2_pad.astype(BF16),
                    gla_gate_b[0].reshape(1, GLA_QK), gla_norm_w[0].reshape(1, GLA_DV), sel)
    proj_pre, small_pre, carry = _inproj(h_pre, nw_attn, w_gdn, w_gla, w_small, conv_w,
                                         jnp.zeros((SUBLANES, conv_w.shape[1]), F32), CHUNK, 256, 1)
    _, gdn_s0, gla_s0 = _mixer(proj_pre, small_pre, *mixer_params,
                               jnp.zeros((GDN_HEADS, GDN_DK, GDN_DV), F32),
                               jnp.zeros((GLA_HEADS, GLA_DV, GLA_DK), F32), 1, CHUNK)

    tm_in = _pick_tile(seq, 512)
    proj_main, small_main, _ = _inproj(x2d, nw_attn, w_gdn, w_gla, w_small, conv_w, carry, tm_in, 256,
                                       seq // tm_in)
    mixed, _, _ = _mixer(proj_main, small_main, *mixer_params, gdn_s0, gla_s0, bsz, step_rows)

    h2d = _outproj(x2d, mixed, w_out[0].astype(BF16), _pick_tile(rows, 1024))
    out = _ffn(h2d, ffn_norm_w[0].reshape(1, d), w_gate[0].astype(BF16), w_up[0].astype(BF16),
               w_down[0].astype(BF16), final_norm_w.reshape(1, d), _pick_tile(rows, 1024), 512)
    return out.reshape(bsz, seq, d)
```
